```python
import math
import jax
import jax.numpy as jnp
from jax import lax
import numpy as np

D_MODEL = 4096
BATCH = 8
SEQ = 2048
DEPTH = 2

N_A_LAYERS = DEPTH // 2
N_B_LAYERS = DEPTH - N_A_LAYERS
RMS_EPS = 1e-6

RET_DK = 256
RET_HEADS = D_MODEL // RET_DK
RET_DV = 2 * RET_DK
RET_CHUNK = 128
ROPE_BASE = 10000.0

NSA_DK = 128
NSA_HEADS = D_MODEL // NSA_DK
NSA_GROUPS = 4
NSA_HPG = NSA_HEADS // NSA_GROUPS
CMP_LEN = 32
CMP_STRIDE = 16
SEL_BLOCK = 64
SEL_TOPK = 16
WINDOW = 512
Q_BLOCK = 128
SEL_SUB = 16
FORCED_SCORE = 1e4
INVALID_SCORE = -1e9

REL_BUCKETS = 32
REL_MAX_DIST = 128

MOE_GROUPS = 8
MOE_EXPERTS_PER_GROUP = 8
MOE_EXPERTS = MOE_GROUPS * MOE_EXPERTS_PER_GROUP
MOE_TOPK = 2
MOE_FF = 256
MOE_BLOCK = 128

kernel_name = 'yoco_retention_nsa_hier_moe'


def rmsnorm(x, g):
    xf = x.astype(jnp.float32)
    y = xf * lax.rsqrt(jnp.mean(xf * xf, axis=-1, keepdims=True) + RMS_EPS)
    return (y * g.astype(jnp.float32)).astype(x.dtype)


def masked_softmax(s, mask):
    maskf = mask.astype(jnp.float32)
    s = jnp.where(mask, s.astype(jnp.float32), -1e30)
    m = jnp.max(s, axis=-1, keepdims=True)
    p = jnp.exp(s - m) * maskf
    return p / jnp.maximum(jnp.sum(p, axis=-1, keepdims=True), 1e-30)


def t5_bucket(dist):
    n = jnp.maximum(dist, 0)
    exact = REL_BUCKETS // 2
    nf = jnp.maximum(n, 1).astype(jnp.float32)
    large = exact + (jnp.log(nf / exact) / math.log(REL_MAX_DIST / exact)
                     * (REL_BUCKETS - exact)).astype(jnp.int32)
    return jnp.where(n < exact, n, jnp.minimum(large, REL_BUCKETS - 1))


def rotary(x, pos):
    half = x.shape[-1] // 2
    inv = 1.0 / (ROPE_BASE ** (jnp.arange(half, dtype=jnp.float32) / half))
    ang = pos.astype(jnp.float32)[:, None] * inv[None, :]
    cos = jnp.cos(ang)[None, :, None, :]
    sin = jnp.sin(ang)[None, :, None, :]
    x1 = x[..., :half].astype(jnp.float32)
    x2 = x[..., half:].astype(jnp.float32)
    return jnp.concatenate([x1 * cos - x2 * sin, x1 * sin + x2 * cos], axis=-1).astype(x.dtype)


def retention(h, w_in, w_o):
    B, T, _ = h.shape
    H, dk, dv, C = RET_HEADS, RET_DK, RET_DV, RET_CHUNK
    proj = h @ w_in
    q, k, v, gate = jnp.split(proj, [H * dk, 2 * H * dk, 2 * H * dk + H * dv], axis=-1)
    pos = jnp.arange(T)
    q = rotary(q.reshape(B, T, H, dk), pos)
    k = rotary(k.reshape(B, T, H, dk), pos) * (dk ** -0.5)
    v = v.reshape(B, T, H, dv)
    n_chunks = T // C

    def to_chunks(a):
        return a.astype(jnp.float32).reshape(B, n_chunks, C, H, a.shape[-1]).transpose(1, 0, 3, 2, 4)

    qc, kc, vc = to_chunks(q), to_chunks(k), to_chunks(v)
    log_gamma = jnp.log1p(-jnp.exp2(-5.0 - jnp.arange(H, dtype=jnp.float32)))
    idx = jnp.arange(C, dtype=jnp.float32)
    diff = idx[:, None] - idx[None, :]
    decay_intra = jnp.where(diff >= 0, jnp.exp(log_gamma[:, None, None] * jnp.maximum(diff, 0.0)), 0.0)
    decay_q = jnp.exp(log_gamma[:, None] * (idx + 1.0))[:, :, None]
    decay_k = jnp.exp(log_gamma[:, None] * (C - 1.0 - idx))[:, :, None]
    decay_chunk = jnp.exp(log_gamma * C)[:, None, None]

    def step(state, inp):
        qi, ki, vi = inp
        scores = jnp.einsum('bhid,bhjd->bhij', qi, ki) * decay_intra
        out = (jnp.einsum('bhij,bhje->bhie', scores, vi)
               + jnp.einsum('bhid,bhde->bhie', qi, state) * decay_q)
        state = decay_chunk * state + jnp.einsum('bhjd,bhje->bhde', ki * decay_k, vi)
        return state, out

    state0 = jnp.zeros((B, H, dk, dv), jnp.float32)
    _, out = lax.scan(step, state0, (qc, kc, vc))
    out = out.transpose(1, 0, 3, 2, 4).reshape(B, T, H, dv)
    out = out * lax.rsqrt(jnp.mean(out * out, axis=-1, keepdims=True) + RMS_EPS)
    y = jax.nn.silu(gate.astype(jnp.float32)) * out.reshape(B, T, H * dv)
    return y.astype(h.dtype) @ w_o


def nsa_shared_kv(x, kv_norm_g, w_kv, cmp_k_pe, cmp_k_w1, cmp_k_w2, cmp_v_pe, cmp_v_w1, cmp_v_w2):
    B, T, _ = x.shape
    G, dk = NSA_GROUPS, NSA_DK
    h = rmsnorm(x, kv_norm_g)
    kv = (h @ w_kv).reshape(B, T, 6, G, dk)
    k_ct, v_ct, k_s, v_s, k_w, v_w = [kv[:, :, i] for i in range(6)]
    n_cmp = (T - CMP_LEN) // CMP_STRIDE + 1
    gidx = jnp.arange(n_cmp)[:, None] * CMP_STRIDE + jnp.arange(CMP_LEN)[None, :]

    def compress(tok, pe, w1, w2):
        blk = tok[:, gidx] + pe[None, None, :, None, :]
        hid = jax.nn.gelu(jnp.einsum('bnlgd,lde->bnge', blk, w1))
        return jnp.einsum('bnge,ef->bngf', hid, w2)

    k_c = compress(k_ct, cmp_k_pe, cmp_k_w1, cmp_k_w2)
    v_c = compress(v_ct, cmp_v_pe, cmp_v_w1, cmp_v_w2)
    return (k_c, v_c, k_s, v_s, k_w, v_w)


def nsa_attention(h, shared, rel_bias, w_q, w_o):
    k_c, v_c, k_s, v_s, k_w, v_w = shared
    B, T, _ = h.shape
    G, Hg, H, dk = NSA_GROUPS, NSA_HPG, NSA_HEADS, NSA_DK
    proj = h @ w_q
    q = proj[..., :H * dk].reshape(B, T, G, Hg, dk) * (dk ** -0.5)
    gates = jax.nn.sigmoid(proj[..., H * dk:].astype(jnp.float32)).reshape(B, T, G, Hg, 3)
    n_cmp = k_c.shape[1]
    n_sblk = T // SEL_BLOCK
    n_sel = min(SEL_TOPK, n_sblk)
    n_qb = T // Q_BLOCK
    n_sub = Q_BLOCK // SEL_SUB
    cs = np.arange(n_cmp) * CMP_STRIDE
    ss = np.arange(n_sblk) * SEL_BLOCK
    ov = np.clip(np.minimum(cs[:, None] + CMP_LEN, ss[None, :] + SEL_BLOCK)
                 - np.maximum(cs[:, None], ss[None, :]), 0, None) / CMP_LEN
    overlap = jnp.asarray(ov, jnp.float32)
    cmp_end = jnp.arange(n_cmp) * CMP_STRIDE + CMP_LEN - 1
    table_g = rel_bias.astype(jnp.float32).reshape(REL_BUCKETS, G, Hg).transpose(1, 0, 2)
    k_sb = k_s.reshape(B, n_sblk, SEL_BLOCK, G, dk).transpose(0, 3, 1, 2, 4)
    v_sb = v_s.reshape(B, n_sblk, SEL_BLOCK, G, dk).transpose(0, 3, 1, 2, 4)
    pad = ((0, 0), (WINDOW, 0), (0, 0), (0, 0))
    k_wp = jnp.pad(k_w, pad)
    v_wp = jnp.pad(v_w, pad)
    b_ix = jnp.arange(B)[:, None, None, None]
    g_ix = jnp.arange(G)[None, :, None, None]
    g_ix5 = jnp.arange(G)[None, :, None, None, None]
    blk_id = jnp.arange(n_sblk)

    def block(args):
        qb, q_blk, g_blk = args
        t = qb * Q_BLOCK + jnp.arange(Q_BLOCK)
        dist_c = t[:, None] - cmp_end[None, :]
        bias_c = jnp.moveaxis(table_g[:, t5_bucket(dist_c)], -1, 1)
        s_c = jnp.einsum('bqghd,bngd->bghqn', q_blk, k_c).astype(jnp.float32) + bias_c
        p_c = masked_softmax(s_c, dist_c >= 0)
        o_c = jnp.einsum('bghqn,bngd->bqghd', p_c.astype(v_c.dtype), v_c)
        imp = jnp.einsum('bghqn,ns->bgqs', p_c, overlap)
        cur = (t // SEL_BLOCK)[:, None]
        valid = blk_id[None, :] * SEL_BLOCK <= t[:, None]
        forced = (blk_id[None, :] == 0) | (blk_id[None, :] == cur) | (blk_id[None, :] == cur - 1)
        score = jnp.where(valid, jnp.where(forced, FORCED_SCORE, imp), INVALID_SCORE)
        _, sel = lax.top_k(score, n_sel)
        q_sub = q_blk.reshape(B, n_sub, SEL_SUB, G, Hg, dk).transpose(1, 0, 2, 3, 4, 5)
        sel_sub = sel.reshape(B, G, n_sub, SEL_SUB, n_sel).transpose(2, 0, 1, 3, 4)
        t_sub = t.reshape(n_sub, SEL_SUB)

        def sel_attend(a):
            qs, ids, ts = a
            kg = k_sb[b_ix, g_ix, ids]
            vg = v_sb[b_ix, g_ix, ids]
            pos = ids[..., None] * SEL_BLOCK + jnp.arange(SEL_BLOCK)
            dist = ts[None, None, :, None, None] - pos
            bias = jnp.moveaxis(table_g[g_ix5, t5_bucket(dist)], -1, 2)
            s = jnp.einsum('bqghd,bgqnld->bghqnl', qs, kg).astype(jnp.float32) + bias
            L = n_sel * SEL_BLOCK
            p = masked_softmax(s.reshape(B, G, Hg, SEL_SUB, L), (dist >= 0).reshape(B, G, 1, SEL_SUB, L))
            return jnp.einsum('bghqk,bgqkd->bqghd', p.astype(vg.dtype), vg.reshape(B, G, SEL_SUB, L, dk))

        o_s = lax.map(sel_attend, (q_sub, sel_sub, t_sub))
        o_s = o_s.transpose(1, 0, 2, 3, 4, 5).reshape(B, Q_BLOCK, G, Hg, dk)
        start = qb * Q_BLOCK
        kw = lax.dynamic_slice_in_dim(k_wp, start, Q_BLOCK + WINDOW, axis=1)
        vw = lax.dynamic_slice_in_dim(v_wp, start, Q_BLOCK + WINDOW, axis=1)
        kpos = start - WINDOW + jnp.arange(Q_BLOCK + WINDOW)
        dist_w = t[:, None] - kpos[None, :]
        mask_w = (dist_w >= 0) & (dist_w < WINDOW) & (kpos[None, :] >= 0)
        bias_w = jnp.moveaxis(table_g[:, t5_bucket(dist_w)], -1, 1)
        s_w = jnp.einsum('bqghd,bkgd->bghqk', q_blk, kw).astype(jnp.float32) + bias_w
        p_w = masked_softmax(s_w, mask_w)
        o_w = jnp.einsum('bghqk,bkgd->bqghd', p_w.astype(vw.dtype), vw)
        return g_blk[..., 0:1] * o_c + g_blk[..., 1:2] * o_s + g_blk[..., 2:3] * o_w

    q_blocks = q.reshape(B, n_qb, Q_BLOCK, G, Hg, dk).transpose(1, 0, 2, 3, 4, 5)
    g_blocks = gates.reshape(B, n_qb, Q_BLOCK, G, Hg, 3).transpose(1, 0, 2, 3, 4, 5)
    o = lax.map(block, (jnp.arange(n_qb), q_blocks, g_blocks))
    o = o.transpose(1, 0, 2, 3, 4, 5).reshape(B, T, H * dk).astype(h.dtype)
    return o @ w_o


def hier_moe(h, w_router_group, w_router_expert, w_gate, w_up, w_down):
    B, T, D = h.shape
    N = B * T
    K, E, EPG, BLK = MOE_TOPK, MOE_EXPERTS, MOE_EXPERTS_PER_GROUP, MOE_BLOCK
    xf = h.reshape(N, D)
    lg = (xf @ w_router_group).astype(jnp.float32)
    pg = jax.nn.softmax(lg, axis=-1)
    _, grp = lax.top_k(lg, 1)
    p_grp = jnp.take_along_axis(pg, grp, axis=-1)
    le = (xf @ w_router_expert).astype(jnp.float32).reshape(N, MOE_GROUPS, EPG)
    le = le[jnp.arange(N), grp[:, 0]]
    top_l, top_i = lax.top_k(le, K)
    weight = p_grp * jax.nn.softmax(top_l, axis=-1)
    expert = grp * EPG + top_i
    M = N * K
    flat_e = expert.reshape(M)
    flat_w = weight.reshape(M)
    order = jnp.argsort(flat_e)
    se = flat_e[order]
    counts = jnp.bincount(flat_e, length=E)
    padded = (counts + BLK - 1) // BLK * BLK
    pad_end = jnp.cumsum(padded)
    pad_start = pad_end - padded
    start = jnp.cumsum(counts) - counts
    dest = pad_start[se] + jnp.arange(M) - start[se]
    n_blocks = (M + E * (BLK - 1) + BLK - 1) // BLK
    P = n_blocks * BLK
    row_tok = jnp.full((P,), N, jnp.int32).at[dest].set((order // K).astype(jnp.int32))
    row_w = jnp.zeros((P,), jnp.float32).at[dest].set(flat_w[order])
    blk_e = jnp.minimum(jnp.searchsorted(pad_end, jnp.arange(n_blocks) * BLK, side='right'), E - 1)
    x_pad = jnp.concatenate([xf, jnp.zeros((1, D), xf.dtype)], axis=0)

    def run(a):
        rows, e = a
        xb = x_pad[rows]
        hid = jax.nn.silu(xb @ w_gate[e]) * (xb @ w_up[e])
        return hid @ w_down[e]

    y_rows = lax.map(run, (row_tok.reshape(n_blocks, BLK), blk_e)).reshape(P, D)
    y = jnp.zeros((N + 1, D), jnp.float32).at[row_tok].add(y_rows.astype(jnp.float32) * row_w[:, None])
    return y[:N].astype(h.dtype).reshape(B, T, D)


def _normal(key, shape, scale):
    return jax.random.normal(key, shape, jnp.float32) * scale


def setup_inputs(seed: int = 0) -> dict:
    key = jax.random.key(seed)
    ks = jax.random.split(key, 24)
    D = D_MODEL
    G, dk, H = NSA_GROUPS, NSA_DK, NSA_HEADS
    E, F = MOE_EXPERTS, MOE_FF
    ret_in_cols = 2 * RET_HEADS * RET_DK + 2 * RET_HEADS * RET_DV
    ret_v = RET_HEADS * RET_DV
    return {
        'x': _normal(ks[0], (BATCH, SEQ, D), 1.0),
        'ret_norm_g': 1.0 + _normal(ks[1], (N_A_LAYERS, D), 0.01),
        'ret_w_in': _normal(ks[2], (N_A_LAYERS, D, ret_in_cols), D ** -0.5),
        'ret_w_o': _normal(ks[3], (N_A_LAYERS, ret_v, D), ret_v ** -0.5),
        'kv_norm_g': 1.0 + _normal(ks[4], (D,), 0.01),
        'nsa_w_kv': _normal(ks[5], (D, 6 * G * dk), D ** -0.5),
        'cmp_k_pe': _normal(ks[6], (CMP_LEN, dk), 0.5),
        'cmp_k_w1': _normal(ks[7], (CMP_LEN, dk, dk), (CMP_LEN * dk) ** -0.5),
        'cmp_k_w2': _normal(ks[8], (dk, dk), dk ** -0.5),
        'cmp_v_pe': _normal(ks[9], (CMP_LEN, dk), 0.5),
        'cmp_v_w1': _normal(ks[10], (CMP_LEN, dk, dk), (CMP_LEN * dk) ** -0.5),
        'cmp_v_w2': _normal(ks[11], (dk, dk), dk ** -0.5),
        'rel_bias': _normal(ks[12], (REL_BUCKETS, H), 0.5),
        'nsa_norm_g': 1.0 + _normal(ks[13], (N_B_LAYERS, D), 0.01),
        'nsa_w_q': _normal(ks[14], (N_B_LAYERS, D, H * dk + 3 * H), D ** -0.5),
        'nsa_w_o': _normal(ks[15], (N_B_LAYERS, H * dk, D), (H * dk) ** -0.5),
        'moe_norm_g': 1.0 + _normal(ks[16], (DEPTH, D), 0.01),
        'moe_w_router_group': _normal(ks[17], (DEPTH, D, MOE_GROUPS), D ** -0.5),
        'moe_w_router_expert': _normal(ks[18], (DEPTH, D, E), D ** -0.5),
        'moe_w_gate': _normal(ks[19], (DEPTH, E, D, F), D ** -0.5),
        'moe_w_up': _normal(ks[20], (DEPTH, E, D, F), D ** -0.5),
        'moe_w_down': _normal(ks[21], (DEPTH, E, F, D), F ** -0.5),
        'final_norm_g': 1.0 + _normal(ks[22], (D,), 0.01),
    }


def reference(x, ret_norm_g, ret_w_in, ret_w_o, kv_norm_g, nsa_w_kv, cmp_k_pe, cmp_k_w1, cmp_k_w2,
              cmp_v_pe, cmp_v_w1, cmp_v_w2, rel_bias, nsa_norm_g, nsa_w_q, nsa_w_o, moe_norm_g,
              moe_w_router_group, moe_w_router_expert, moe_w_gate, moe_w_up, moe_w_down, final_norm_g):
    shared = None
    for layer in range(DEPTH):
        if layer < N_A_LAYERS:
            x = x + retention(rmsnorm(x, ret_norm_g[layer]), ret_w_in[layer], ret_w_o[layer])
        else:
            if layer == N_A_LAYERS:
                shared = nsa_shared_kv(x, kv_norm_g, nsa_w_kv, cmp_k_pe, cmp_k_w1, cmp_k_w2,
                                       cmp_v_pe, cmp_v_w1, cmp_v_w2)
            j = layer - N_A_LAYERS
            x = x + nsa_attention(rmsnorm(x, nsa_norm_g[j]), shared, rel_bias, nsa_w_q[j], nsa_w_o[j])
        x = x + hier_moe(rmsnorm(x, moe_norm_g[layer]), moe_w_router_group[layer],
                         moe_w_router_expert[layer], moe_w_gate[layer], moe_w_up[layer], moe_w_down[layer])
    return rmsnorm(x, final_norm_g)
```

```python
import functools
import math

import numpy as np
import jax
import jax.numpy as jnp
from jax import lax
from jax.experimental import pallas as pl
from jax.experimental.pallas import tpu as pltpu

RMS_EPS = 1e-6
RET_DK = 256
RET_DV = 2 * RET_DK
ROPE_BASE = 10000.0
NSA_DK = 128
NSA_GROUPS = 4
CMP_LEN = 32
CMP_STRIDE = 16
SEL_BLOCK = 64
SEL_TOPK = 16
WINDOW = 512
FORCED_SCORE = 1e4
INVALID_SCORE = -1e9
REL_BUCKETS = 32
REL_MAX_DIST = 128
MOE_GROUPS = 8
MOE_EPG = 8
MOE_EXPERTS = MOE_GROUPS * MOE_EPG
MOE_TOPK = 2

LANES = 128
V7X_VMEM_LIMIT = 56 * 1024 * 1024
NEG = -1e30

BF16 = jnp.bfloat16
F32 = jnp.float32


def _tile(dim, pref):
    t = min(dim, pref)
    while dim % t:
        t //= 2
    return t


def _cparams(sem, vmem=None):
    return pltpu.CompilerParams(dimension_semantics=sem, vmem_limit_bytes=vmem)


def _norm_kernel(x_ref, g_ref, o_ref):
    x = x_ref[...]
    y = x * lax.rsqrt(jnp.mean(x * x, axis=-1, keepdims=True) + RMS_EPS)
    o_ref[...] = (y * g_ref[...]).astype(o_ref.dtype)


def _rmsnorm(x2, g, out_dtype=BF16):
    m, d = x2.shape
    tm = _tile(m, 256)
    return pl.pallas_call(
        _norm_kernel,
        grid=(m // tm,),
        in_specs=[pl.BlockSpec((tm, d), lambda i: (i, 0)),
                  pl.BlockSpec((1, d), lambda i: (0, 0))],
        out_specs=pl.BlockSpec((tm, d), lambda i: (i, 0)),
        out_shape=jax.ShapeDtypeStruct((m, d), out_dtype),
        compiler_params=_cparams(("parallel",), V7X_VMEM_LIMIT),
        name="rmsnorm",
    )(x2, g.reshape(1, d))


def _mm_kernel(a_ref, w_ref, *rest, n_extra, epilogue):
    extra = rest[:n_extra]
    o_ref = rest[n_extra]
    wbf_ref = rest[n_extra + 1]

    @pl.when(pl.program_id(1) == 0)
    def _():
        wbf_ref[...] = w_ref[...].astype(BF16)

    acc = jnp.dot(a_ref[...], wbf_ref[...], preferred_element_type=F32)
    epilogue(acc, o_ref, *extra)


def _matmul(a, w, *, col0, ncols, tm, tn, out_dtype, epilogue, extra=(), name):
    m, k = a.shape
    assert col0 % tn == 0 and ncols % tn == 0 and m % tm == 0
    joff = col0 // tn
    in_specs = [pl.BlockSpec((tm, k), lambda j, i: (i, 0)),
                pl.BlockSpec((k, tn), lambda j, i: (0, j + joff))]
    args = [a, w]
    for arr, bshape, imap in extra:
        in_specs.append(pl.BlockSpec(bshape, imap))
        args.append(arr)
    return pl.pallas_call(
        functools.partial(_mm_kernel, n_extra=len(extra), epilogue=epilogue),
        grid=(ncols // tn, m // tm),
        in_specs=in_specs,
        out_specs=pl.BlockSpec((tm, tn), lambda j, i: (i, j)),
        out_shape=jax.ShapeDtypeStruct((m, ncols), out_dtype),
        scratch_shapes=[pltpu.VMEM((k, tn), BF16)],
        compiler_params=_cparams(("arbitrary", "arbitrary"), V7X_VMEM_LIMIT),
        name=name,
    )(*args)


def _ep_store(acc, o_ref):
    o_ref[...] = acc.astype(o_ref.dtype)


def _ep_residual(acc, o_ref, r_ref):
    o_ref[...] = (r_ref[...] + acc).astype(o_ref.dtype)


def _ep_scale(acc, o_ref, *, scale):
    o_ref[...] = (acc * scale).astype(o_ref.dtype)


def _ep_sigmoid(acc, o_ref):
    o_ref[...] = jax.nn.sigmoid(acc).astype(o_ref.dtype)


def _ep_rotary(acc, o_ref, cos_ref, sin_ref, *, k_first_tile, k_scale):
    scale = jnp.where(pl.program_id(0) >= k_first_tile, k_scale, 1.0).astype(F32)
    cos = cos_ref[...] * scale
    sin = sin_ref[...] * scale
    half = RET_DK // 2
    for hd in range(acc.shape[1] // RET_DK):
        c0 = hd * RET_DK
        x1 = acc[:, c0:c0 + half]
        x2 = acc[:, c0 + half:c0 + RET_DK]
        o_ref[:, c0:c0 + half] = (x1 * cos - x2 * sin).astype(o_ref.dtype)
        o_ref[:, c0 + half:c0 + RET_DK] = (x1 * sin + x2 * cos).astype(o_ref.dtype)


def _ret_kernel(dchunk_ref, q_ref, k_ref, v_ref, g_ref, dintra_ref, dq_ref, dk_ref, y_ref, state_ref):
    h = pl.program_id(1)
    c = pl.program_id(2)

    @pl.when(c == 0)
    def _():
        state_ref[...] = jnp.zeros_like(state_ref)

    q = q_ref[...]
    k = k_ref[...]
    v = v_ref[...]
    scores = lax.dot_general(q, k, (((1,), (1,)), ((), ())), preferred_element_type=F32)
    scores = scores * dintra_ref[0]
    intra = jnp.dot(scores.astype(BF16), v, preferred_element_type=F32)
    cross = jnp.dot(q, state_ref[...].astype(BF16), preferred_element_type=F32)
    dq = dq_ref[0]
    dv = v.shape[1]
    out = jnp.concatenate(
        [intra[:, j:j + LANES] + cross[:, j:j + LANES] * dq for j in range(0, dv, LANES)], axis=1)
    kt = (k.astype(F32).T * dk_ref[0]).astype(BF16)
    state_ref[...] = dchunk_ref[h] * state_ref[...] + jnp.dot(kt, v, preferred_element_type=F32)
    out = out * lax.rsqrt(jnp.mean(out * out, axis=-1, keepdims=True) + RMS_EPS)
    gate = g_ref[...].astype(F32)
    y_ref[...] = (jax.nn.silu(gate) * out).astype(y_ref.dtype)


def _retention_core(qk, vg, batch, seq, heads):
    m = batch * seq
    c = _tile(seq, 256)
    nc = seq // c
    log_gamma = jnp.log1p(-jnp.exp2(-5.0 - jnp.arange(heads, dtype=F32)))
    idx = jnp.arange(c, dtype=F32)
    diff = idx[:, None] - idx[None, :]
    dintra = jnp.where(diff >= 0, jnp.exp(log_gamma[:, None, None] * jnp.maximum(diff, 0.0)), 0.0)
    dq = jnp.broadcast_to(jnp.exp(log_gamma[:, None] * (idx + 1.0))[:, :, None], (heads, c, LANES))
    dk = jnp.exp(log_gamma[:, None] * (c - 1.0 - idx))[:, None, :]
    dchunk = jnp.exp(log_gamma * c)
    grid_spec = pltpu.PrefetchScalarGridSpec(
        num_scalar_prefetch=1,
        grid=(batch, heads, nc),
        in_specs=[
            pl.BlockSpec((c, RET_DK), lambda b, h, i, s: (b * nc + i, h)),
            pl.BlockSpec((c, RET_DK), lambda b, h, i, s: (b * nc + i, heads + h)),
            pl.BlockSpec((c, RET_DV), lambda b, h, i, s: (b * nc + i, h)),
            pl.BlockSpec((c, RET_DV), lambda b, h, i, s: (b * nc + i, heads + h)),
            pl.BlockSpec((1, c, c), lambda b, h, i, s: (h, 0, 0)),
            pl.BlockSpec((1, c, LANES), lambda b, h, i, s: (h, 0, 0)),
            pl.BlockSpec((1, 1, c), lambda b, h, i, s: (h, 0, 0)),
        ],
        out_specs=pl.BlockSpec((c, RET_DV), lambda b, h, i, s: (b * nc + i, h)),
        scratch_shapes=[pltpu.VMEM((RET_DK, RET_DV), F32)],
    )
    return pl.pallas_call(
        _ret_kernel,
        grid_spec=grid_spec,
        out_shape=jax.ShapeDtypeStruct((m, heads * RET_DV), BF16),
        compiler_params=_cparams(("parallel", "parallel", "arbitrary"), V7X_VMEM_LIMIT),
        name="retention_core",
    )(dchunk, qk, qk, vg, vg, dintra, dq, dk)


def _retention_layer(x2, norm_g, w_in, w_o, batch, seq):
    m, d = x2.shape
    heads = d // RET_DK
    hq = heads * RET_DK
    h = _rmsnorm(x2, norm_g)
    pos = jnp.arange(seq, dtype=F32)
    half = RET_DK // 2
    inv = 1.0 / (ROPE_BASE ** (jnp.arange(half, dtype=F32) / half))
    ang = pos[:, None] * inv[None, :]
    cos, sin = jnp.cos(ang), jnp.sin(ang)
    tm = _tile(seq, 512)
    tn = _tile(hq, 512)
    nrow = seq // tm
    rot = functools.partial(_ep_rotary, k_first_tile=hq // tn, k_scale=RET_DK ** -0.5)
    qk = _matmul(h, w_in, col0=0, ncols=2 * hq, tm=tm, tn=tn, out_dtype=BF16, epilogue=rot,
                 extra=((cos, (tm, half), lambda j, i: (i % nrow, 0)),
                        (sin, (tm, half), lambda j, i: (i % nrow, 0))),
                 name="ret_proj_qk")
    vg = _matmul(h, w_in, col0=2 * hq, ncols=2 * heads * RET_DV, tm=tm, tn=tn, out_dtype=BF16,
                 epilogue=_ep_store, name="ret_proj_vg")
    y = _retention_core(qk, vg, batch, seq, heads)
    tn_o = _tile(d, 256)
    return _matmul(y, w_o, col0=0, ncols=d, tm=tm, tn=tn_o, out_dtype=F32, epilogue=_ep_residual,
                   extra=((x2, (tm, tn_o), lambda j, i: (i, j)),), name="ret_out_proj")


def _route_kernel(x_ref, g_ref, wr_ref, h_ref, r_ref):
    x = x_ref[...]
    hn = x * lax.rsqrt(jnp.mean(x * x, axis=-1, keepdims=True) + RMS_EPS) * g_ref[...]
    h_ref[...] = hn
    logits = jnp.dot(hn, wr_ref[...], preferred_element_type=F32, precision=lax.Precision.HIGHEST)
    lane = lax.broadcasted_iota(jnp.int32, logits.shape, 1)
    is_g = lane < MOE_GROUPS
    lg = jnp.where(is_g, logits, -jnp.inf)
    mg = jnp.max(lg, axis=-1, keepdims=True)
    grp = jnp.min(jnp.where(lg == mg, lane, LANES), axis=-1, keepdims=True)
    p_grp = 1.0 / jnp.sum(jnp.where(is_g, jnp.exp(lg - mg), 0.0), axis=-1, keepdims=True)
    e_lane = lane - MOE_GROUPS
    is_e = (e_lane >= 0) & (e_lane < MOE_EXPERTS) & ((e_lane >> 3) == grp)
    le = jnp.where(is_e, logits, -jnp.inf)
    m1 = jnp.max(le, axis=-1, keepdims=True)
    i1 = jnp.min(jnp.where(le == m1, lane, LANES), axis=-1, keepdims=True)
    le2 = jnp.where(lane == i1, -jnp.inf, le)
    m2 = jnp.max(le2, axis=-1, keepdims=True)
    i2 = jnp.min(jnp.where(le2 == m2, lane, LANES), axis=-1, keepdims=True)
    t = jnp.exp(m2 - m1)
    w1 = p_grp / (1.0 + t)
    w2 = p_grp * t / (1.0 + t)
    e1 = (i1 - MOE_GROUPS).astype(F32)
    e2 = (i2 - MOE_GROUPS).astype(F32)
    r_ref[...] = jnp.where(lane == 0, e1, jnp.where(lane == 1, e2,
                           jnp.where(lane == 2, w1, jnp.where(lane == 3, w2, 0.0))))


def _route(x2, norm_g, w_rg, w_re):
    m, d = x2.shape
    tm = _tile(m, 256)
    wr = jnp.concatenate(
        [w_rg, w_re, jnp.zeros((d, LANES - MOE_GROUPS - MOE_EXPERTS), F32)], axis=1)
    return pl.pallas_call(
        _route_kernel,
        grid=(m // tm,),
        in_specs=[pl.BlockSpec((tm, d), lambda i: (i, 0)),
                  pl.BlockSpec((1, d), lambda i: (0, 0)),
                  pl.BlockSpec((d, LANES), lambda i: (0, 0))],
        out_specs=[pl.BlockSpec((tm, d), lambda i: (i, 0)),
                   pl.BlockSpec((tm, LANES), lambda i: (i, 0))],
        out_shape=[jax.ShapeDtypeStruct((m, d), F32),
                   jax.ShapeDtypeStruct((m, LANES), F32)],
        compiler_params=_cparams(("parallel",), V7X_VMEM_LIMIT),
        name="moe_route",
    )(x2, norm_g.reshape(1, d), wr)


def _ffn_kernel(blk_e_ref, nvalid_ref, nused_ref,
                src_cur_ref, src_nxt_ref, dst_ref, roww_ref, wg_ref, wu_ref, wd_ref, h_hbm,
                pairs_hbm,
                wgb_ref, wub_ref, wdb_ref, xbuf_ref, ybuf_ref, gsem, ssem, *, tb):
    i = pl.program_id(0)
    n_used = nused_ref[0]
    slot = i % 2

    def gather_copy(src_ref, r, s):
        return pltpu.make_async_copy(h_hbm.at[pl.ds(src_ref[0, 0, r], 1)],
                                     xbuf_ref.at[s, pl.ds(r, 1)], gsem.at[s])

    def scatter_copy(d_row, r, s):
        return pltpu.make_async_copy(ybuf_ref.at[s, pl.ds(r, 1)],
                                     pairs_hbm.at[pl.ds(d_row, 1)], ssem.at[s])

    def start_gather(src_ref, s):
        def body(r, carry):
            gather_copy(src_ref, r, s).start()
            return carry
        lax.fori_loop(0, tb, body, 0)

    def wait_rows(make, n):
        def body(r, carry):
            make(r).wait()
            return carry
        lax.fori_loop(0, n, body, 0)

    @pl.when(i == 0)
    def _():
        start_gather(src_cur_ref, 0)

    @pl.when(i < n_used)
    def _():
        @pl.when(i + 1 < n_used)
        def _():
            start_gather(src_nxt_ref, 1 - slot)

        first = jnp.logical_or(i == 0, blk_e_ref[i] != blk_e_ref[jnp.maximum(i - 1, 0)])

        @pl.when(first)
        def _():
            wgb_ref[...] = wg_ref[0].astype(BF16)
            wub_ref[...] = wu_ref[0].astype(BF16)
            wdb_ref[...] = wd_ref[0].astype(BF16)

        wait_rows(lambda r: gather_copy(src_cur_ref, r, slot), tb)
        xb = xbuf_ref[slot].astype(BF16)
        hid = (jax.nn.silu(jnp.dot(xb, wgb_ref[...], preferred_element_type=F32))
               * jnp.dot(xb, wub_ref[...], preferred_element_type=F32))
        y = jnp.dot(hid.astype(BF16), wdb_ref[...], preferred_element_type=F32)

        @pl.when(i >= 2)
        def _():
            wait_rows(lambda r: scatter_copy(0, r, slot), nvalid_ref[jnp.maximum(i - 2, 0)])

        ybuf_ref[slot] = y * roww_ref[:, 0:1]
        nv = nvalid_ref[i]

        def sbody(r, carry):
            scatter_copy(dst_ref[0, 0, r], r, slot).start()
            return carry
        lax.fori_loop(0, nv, sbody, 0)

        @pl.when(i == n_used - 1)
        def _():
            @pl.when(i >= 1)
            def _():
                wait_rows(lambda r: scatter_copy(0, r, 1 - slot), nvalid_ref[jnp.maximum(i - 1, 0)])
            wait_rows(lambda r: scatter_copy(0, r, slot), nv)


def _moe_ffn(h, route, w_gate, w_up, w_down, tb):
    n, d = h.shape
    e, _, f = w_gate.shape
    k = MOE_TOPK
    mcopies = n * k
    expert = route[:, 0:2].astype(jnp.int32)
    weight = route[:, 2:4]
    flat_e = expert.reshape(mcopies)
    flat_w = weight.reshape(mcopies)
    order = jnp.argsort(flat_e)
    se = flat_e[order]
    counts = jnp.bincount(flat_e, length=e)
    padded = (counts + tb - 1) // tb * tb
    pad_end = jnp.cumsum(padded)
    pad_start = pad_end - padded
    start = jnp.cumsum(counts) - counts
    dest = pad_start[se] + jnp.arange(mcopies) - start[se]
    nb = (mcopies + e * (tb - 1) + tb - 1) // tb
    p = nb * tb
    src_tok = jnp.zeros((p,), jnp.int32).at[dest].set((order // k).astype(jnp.int32))
    dst_row = jnp.zeros((p,), jnp.int32).at[dest].set(((order % k) * n + order // k).astype(jnp.int32))
    row_w = jnp.zeros((p,), F32).at[dest].set(flat_w[order])
    blk_start = jnp.arange(nb) * tb
    blk_e = jnp.minimum(jnp.searchsorted(pad_end, blk_start, side='right'), e - 1).astype(jnp.int32)
    nvalid = jnp.clip(pad_start[blk_e] + counts[blk_e] - blk_start, 0, tb).astype(jnp.int32)
    n_used = (pad_end[-1] // tb).astype(jnp.int32).reshape(1)
    src3 = src_tok.reshape(nb, 1, tb)
    dst3 = dst_row.reshape(nb, 1, tb)
    roww = jnp.broadcast_to(row_w[:, None], (p, LANES))

    smem = functools.partial(pl.BlockSpec, memory_space=pltpu.SMEM)
    grid_spec = pltpu.PrefetchScalarGridSpec(
        num_scalar_prefetch=3,
        grid=(nb,),
        in_specs=[
            smem((1, 1, tb), lambda i, be, nv, nu: (i, 0, 0)),
            smem((1, 1, tb), lambda i, be, nv, nu: (jnp.minimum(i + 1, nb - 1), 0, 0)),
            smem((1, 1, tb), lambda i, be, nv, nu: (i, 0, 0)),
            pl.BlockSpec((tb, LANES), lambda i, be, nv, nu: (i, 0)),
            pl.BlockSpec((1, d, f), lambda i, be, nv, nu: (be[i], 0, 0)),
            pl.BlockSpec((1, d, f), lambda i, be, nv, nu: (be[i], 0, 0)),
            pl.BlockSpec((1, f, d), lambda i, be, nv, nu: (be[i], 0, 0)),
            pl.BlockSpec(memory_space=pl.ANY),
        ],
        out_specs=pl.BlockSpec(memory_space=pl.ANY),
        scratch_shapes=[
            pltpu.VMEM((d, f), BF16), pltpu.VMEM((d, f), BF16), pltpu.VMEM((f, d), BF16),
            pltpu.VMEM((2, tb, d), F32), pltpu.VMEM((2, tb, d), F32),
            pltpu.SemaphoreType.DMA((2,)), pltpu.SemaphoreType.DMA((2,)),
        ],
    )
    return pl.pallas_call(
        functools.partial(_ffn_kernel, tb=tb),
        grid_spec=grid_spec,
        out_shape=jax.ShapeDtypeStruct((k * n, d), F32),
        compiler_params=_cparams(("arbitrary",), V7X_VMEM_LIMIT),
        name="moe_ffn",
    )(blk_e, nvalid, n_used, src3, src3, dst3, roww, w_gate, w_up, w_down, h)


def _combine_kernel(x_ref, p0_ref, p1_ref, *rest, n_norm, want_x):
    g_refs = rest[:n_norm]
    outs = rest[n_norm:]
    x = x_ref[...] + (p0_ref[...] + p1_ref[...])
    oi = 0
    if want_x:
        outs[0][...] = x
        oi = 1
    r = lax.rsqrt(jnp.mean(x * x, axis=-1, keepdims=True) + RMS_EPS)
    y = x * r
    for g_ref, o_ref in zip(g_refs, outs[oi:]):
        o_ref[...] = (y * g_ref[...]).astype(o_ref.dtype)


def _combine(x2, pairs, gains, out_dtypes, want_x):
    m, d = x2.shape
    tm = _tile(m, 256)
    nrow = m // tm
    in_specs = [pl.BlockSpec((tm, d), lambda i: (i, 0)),
                pl.BlockSpec((tm, d), lambda i: (i, 0)),
                pl.BlockSpec((tm, d), lambda i: (i + nrow, 0))]
    in_specs += [pl.BlockSpec((1, d), lambda i: (0, 0)) for _ in gains]
    out_shape = ([jax.ShapeDtypeStruct((m, d), F32)] if want_x else []) + [
        jax.ShapeDtypeStruct((m, d), dt) for dt in out_dtypes]
    out_specs = [pl.BlockSpec((tm, d), lambda i: (i, 0)) for _ in out_shape]
    return pl.pallas_call(
        functools.partial(_combine_kernel, n_norm=len(gains), want_x=want_x),
        grid=(nrow,),
        in_specs=in_specs,
        out_specs=out_specs,
        out_shape=out_shape,
        compiler_params=_cparams(("parallel",), V7X_VMEM_LIMIT),
        name="moe_combine",
    )(x2, pairs, pairs, *[g.reshape(1, d) for g in gains])


def _t5_bucket(dist):
    n = jnp.maximum(dist, 0)
    exact = REL_BUCKETS // 2
    nf = jnp.maximum(n, 1).astype(F32)
    large = exact + (jnp.log(nf / exact) / math.log(REL_MAX_DIST / exact)
                     * (REL_BUCKETS - exact)).astype(jnp.int32)
    return jnp.where(n < exact, n, jnp.minimum(large, REL_BUCKETS - 1))


def _cmp_kernel(rk_ref, rv_ref, pe_ref, w1_ref, w2_ref, o_ref):
    half = w1_ref.shape[1] // 2
    for s, r_ref in enumerate((rk_ref, rv_ref)):
        r = r_ref[0, 0].astype(F32)
        ra = (r + pe_ref[s, :, :half]).astype(BF16)
        rb = (r + pe_ref[s, :, half:]).astype(BF16)
        ha = jnp.dot(ra, w1_ref[s, :half, :], preferred_element_type=F32)
        hb = jnp.dot(rb, w1_ref[s, half:, :], preferred_element_type=F32)
        hb_next = jnp.concatenate([hb[1:], hb[:1]], axis=0)
        hid = jax.nn.gelu(ha + hb_next)
        o_ref[0, 0, s] = jnp.dot(hid.astype(BF16), w2_ref[s], preferred_element_type=F32).astype(o_ref.dtype)


def _compress(kv, batch, seq, pes, w1s, w2s):
    g, dk = NSA_GROUPS, NSA_DK
    nr = seq // CMP_STRIDE
    ct = kv[:, :2 * g * dk].reshape(batch, nr, CMP_STRIDE, 2, g, dk)
    r = ct.transpose(3, 0, 4, 1, 2, 5).reshape(2, batch, g, nr, CMP_STRIDE * dk)
    pe = jnp.stack(pes).reshape(2, 1, CMP_LEN * dk)
    w1 = jnp.stack(w1s).reshape(2, CMP_LEN * dk, dk).astype(BF16)
    w2 = jnp.stack(w2s).astype(BF16)
    rspec = pl.BlockSpec((1, 1, nr, CMP_STRIDE * dk), lambda b, gi: (b, gi, 0, 0))
    return pl.pallas_call(
        _cmp_kernel,
        grid=(batch, g),
        in_specs=[rspec, rspec,
                  pl.BlockSpec((2, 1, CMP_LEN * dk), lambda b, gi: (0, 0, 0)),
                  pl.BlockSpec((2, CMP_LEN * dk, dk), lambda b, gi: (0, 0, 0)),
                  pl.BlockSpec((2, dk, dk), lambda b, gi: (0, 0, 0))],
        out_specs=pl.BlockSpec((1, 1, 2, nr, dk), lambda b, gi: (b, gi, 0, 0, 0)),
        out_shape=jax.ShapeDtypeStruct((batch, g, 2, nr, dk), BF16),
        compiler_params=_cparams(("parallel", "parallel"), V7X_VMEM_LIMIT),
        name="nsa_compress",
    )(r[0], r[1], pe, w1, w2)


def _nsa_kernel(q_ref, gate_ref, kvc_ref, ks_ref, vs_ref, kw_ref, vw_ref, biasc_ref, bias_ref,
                ov_ref, ex_ref, o_ref,
                selmask_ref, m_ref, l_ref, acc_ref, *, hg, tq, n_cmp, n_sblk):
    qb = pl.program_id(2)
    dk = NSA_DK
    rows = hg * tq
    q8 = jnp.concatenate([q_ref[:, h * dk:(h + 1) * dk] for h in range(hg)], axis=0)
    qi = lax.broadcasted_iota(jnp.int32, (tq, LANES), 0)
    kj = lax.broadcasted_iota(jnp.int32, (tq, LANES), 1)
    t_abs = qb * tq + qi

    kc = kvc_ref[0, 0, 0]
    vc = kvc_ref[0, 0, 1]
    s = lax.dot_general(q8, kc, (((1,), (1,)), ((), ())), preferred_element_type=F32)
    s = s.reshape(hg, tq, LANES) + biasc_ref[...]
    valid_c = (kj * CMP_STRIDE + (CMP_LEN - 1) <= t_abs) & (kj < n_cmp)
    s = jnp.where(valid_c[None], s, NEG)
    mx = jnp.max(s, axis=-1, keepdims=True)
    p = jnp.where(valid_c[None], jnp.exp(s - mx), 0.0)
    p = p / jnp.maximum(jnp.sum(p, axis=-1, keepdims=True), 1e-30)
    o_c = jnp.dot(p.reshape(rows, LANES).astype(BF16), vc, preferred_element_type=F32)
    psum = jnp.sum(p, axis=0)
    p_hi = psum.astype(BF16)
    p_lo = (psum - p_hi.astype(F32)).astype(BF16)
    imp = (jnp.dot(p_hi, ov_ref[...], preferred_element_type=F32)
           + jnp.dot(p_lo, ov_ref[...], preferred_element_type=F32))

    cur = t_abs // SEL_BLOCK
    valid_b = (kj * SEL_BLOCK <= t_abs) & (kj < n_sblk)
    forced = (kj == 0) | (kj == cur) | (kj == cur - 1)
    score = jnp.where(valid_b, jnp.where(forced, FORCED_SCORE, imp), INVALID_SCORE)
    score = jnp.where(kj < n_sblk, score, -jnp.inf)
    rank = jnp.zeros((tq, LANES), F32)
    for j in range(n_sblk):
        cj = score[:, j:j + 1]
        ahead = (cj > score) | ((cj == score) & (kj > j))
        rank = rank + jnp.where(ahead, 1.0, 0.0)
    sel = jnp.where(rank < min(SEL_TOPK, n_sblk), 1.0, 0.0).astype(BF16)
    selmask = jnp.dot(sel, ex_ref[...], preferred_element_type=F32)
    for kb in range(selmask_ref.shape[0]):
        selmask_ref[kb] = selmask[:, kb * LANES:(kb + 1) * LANES]

    def attend(kt, vt, bias, valid, br):
        sc = lax.dot_general(q8, kt, (((1,), (1,)), ((), ())), preferred_element_type=F32)
        sc = sc.reshape(hg, tq, LANES) + bias
        sc = jnp.where(valid[None], sc, NEG)
        m_old = m_ref[br]
        m_new = jnp.maximum(m_old, jnp.max(sc, axis=-1, keepdims=True))
        alpha = jnp.exp(m_old - m_new)
        pt = jnp.where(valid[None], jnp.exp(sc - m_new), 0.0)
        l_ref[br] = alpha * l_ref[br] + jnp.sum(pt, axis=-1, keepdims=True)
        pv = jnp.dot(pt.reshape(rows, LANES).astype(BF16), vt, preferred_element_type=F32)
        acc_ref[br] = alpha * acc_ref[br] + pv.reshape(hg, tq, dk)
        m_ref[br] = m_new

    m_ref[...] = jnp.full(m_ref.shape, NEG, F32)
    l_ref[...] = jnp.zeros(l_ref.shape, F32)
    acc_ref[...] = jnp.zeros(acc_ref.shape, F32)
    n_far = (bias_ref.shape[0] - 1)

    def sel_body(kb, carry):
        off = pl.multiple_of(kb * LANES, LANES)
        dist = (qb - kb) * tq + qi - kj
        valid = (dist >= 0) & (selmask_ref[kb] > 0.5)
        attend(ks_ref[pl.ds(off, LANES), :], vs_ref[pl.ds(off, LANES), :],
               bias_ref[jnp.minimum(qb - kb, n_far)], valid, 0)
        return carry
    lax.fori_loop(0, qb + 1, sel_body, 0)

    def win_body(kb, carry):
        off = pl.multiple_of(kb * LANES, LANES)
        dist = (qb - kb) * tq + qi - kj
        valid = (dist >= 0) & (dist < WINDOW)
        attend(kw_ref[pl.ds(off, LANES), :], vw_ref[pl.ds(off, LANES), :],
               bias_ref[jnp.minimum(qb - kb, n_far)], valid, 1)
        return carry
    lax.fori_loop(jnp.maximum(qb - WINDOW // LANES, 0), qb + 1, win_body, 0)

    o_s = acc_ref[0] / jnp.maximum(l_ref[0], 1e-30)
    o_w = acc_ref[1] / jnp.maximum(l_ref[1], 1e-30)
    o_c = o_c.reshape(hg, tq, dk)
    gate = gate_ref[...]
    for h in range(hg):
        out = (gate[:, 3 * h:3 * h + 1] * o_c[h] + gate[:, 3 * h + 1:3 * h + 2] * o_s[h]
               + gate[:, 3 * h + 2:3 * h + 3] * o_w[h])
        o_ref[:, h * dk:(h + 1) * dk] = out.astype(o_ref.dtype)


def _nsa_attention(q, gates, kv, kvc, rel_bias, batch, seq, heads):
    g, dk = NSA_GROUPS, NSA_DK
    hg = heads // g
    tq = LANES
    nqb = seq // tq
    n_cmp = (seq - CMP_LEN) // CMP_STRIDE + 1
    n_sblk = seq // SEL_BLOCK
    nr = seq // CMP_STRIDE
    assert nr == LANES and n_sblk <= LANES and tq == LANES
    table = rel_bias.astype(F32).T
    tt = jnp.arange(seq)
    cmp_end = jnp.arange(LANES) * CMP_STRIDE + CMP_LEN - 1
    bias_c = table[:, _t5_bucket(tt[:, None] - cmp_end[None, :])]
    ii = jnp.arange(tq)
    n_far = (REL_MAX_DIST + tq - 1) // tq + 1
    dist = jnp.arange(n_far + 1)[:, None, None] * tq + ii[None, :, None] - ii[None, None, :]
    bias_t = table[:, _t5_bucket(dist)].transpose(1, 0, 2, 3)
    cs = np.arange(LANES) * CMP_STRIDE
    ss = np.arange(LANES) * SEL_BLOCK
    ov = np.clip(np.minimum(cs[:, None] + CMP_LEN, ss[None, :] + SEL_BLOCK)
                 - np.maximum(cs[:, None], ss[None, :]), 0, None) / CMP_LEN
    ov[n_cmp:, :] = 0.0
    ov[:, n_sblk:] = 0.0
    overlap = jnp.asarray(ov, BF16)
    ex = (np.arange(LANES)[:, None] == (np.arange(seq)[None, :] // SEL_BLOCK)).astype(np.float32)
    expand = jnp.asarray(ex, BF16)

    gdk = g * dk
    kvspec = lambda blk: pl.BlockSpec((seq, dk), lambda b, gi, qb: (b, blk * g + gi))
    return pl.pallas_call(
        functools.partial(_nsa_kernel, hg=hg, tq=tq, n_cmp=n_cmp, n_sblk=n_sblk),
        grid=(batch, g, nqb),
        in_specs=[
            pl.BlockSpec((tq, hg * dk), lambda b, gi, qb: (b * nqb + qb, gi)),
            pl.BlockSpec((tq, LANES), lambda b, gi, qb: (b * nqb + qb, gi)),
            pl.BlockSpec((1, 1, 2, nr, dk), lambda b, gi, qb: (b, gi, 0, 0, 0)),
            kvspec(2), kvspec(3), kvspec(4), kvspec(5),
            pl.BlockSpec((hg, tq, LANES), lambda b, gi, qb: (gi, qb, 0)),
            pl.BlockSpec((n_far + 1, hg, tq, tq), lambda b, gi, qb: (0, gi, 0, 0)),
            pl.BlockSpec((LANES, LANES), lambda b, gi, qb: (0, 0)),
            pl.BlockSpec((LANES, seq), lambda b, gi, qb: (0, 0)),
        ],
        out_specs=pl.BlockSpec((tq, hg * dk), lambda b, gi, qb: (b * nqb + qb, gi)),
        out_shape=jax.ShapeDtypeStruct((batch * seq, heads * dk), BF16),
        scratch_shapes=[pltpu.VMEM((seq // LANES, tq, LANES), F32),
                        pltpu.VMEM((2, hg, tq, 1), F32), pltpu.VMEM((2, hg, tq, 1), F32),
                        pltpu.VMEM((2, hg, tq, dk), F32)],
        compiler_params=_cparams(("parallel", "parallel", "arbitrary"), V7X_VMEM_LIMIT),
        name="nsa_attention",
    )(q, gates, kvc, kv, kv, kv, kv, bias_c, bias_t, overlap, expand)


def _nsa_layer(x2, h_kv, h_q, w_kv, cmp_params, rel_bias, w_q, w_o, batch, seq):
    m, d = x2.shape
    g, dk = NSA_GROUPS, NSA_DK
    heads = d // dk
    hg = heads // g
    tm = _tile(seq, 512)
    kv = _matmul(h_kv, w_kv, col0=0, ncols=6 * g * dk, tm=tm, tn=_tile(6 * g * dk, 512),
                 out_dtype=BF16, epilogue=_ep_store, name="nsa_proj_kv")
    (k_pe, k_w1, k_w2, v_pe, v_w1, v_w2) = cmp_params
    kvc = _compress(kv, batch, seq, (k_pe, v_pe), (k_w1, v_w1), (k_w2, v_w2))
    q = _matmul(h_q, w_q, col0=0, ncols=heads * dk, tm=tm, tn=_tile(heads * dk, 512), out_dtype=BF16,
                epilogue=functools.partial(_ep_scale, scale=dk ** -0.5), name="nsa_proj_q")
    wg = w_q[:, heads * dk:].reshape(d, g, hg * 3)
    wg = jnp.pad(wg, ((0, 0), (0, 0), (0, LANES - hg * 3))).reshape(d, g * LANES)
    gates = _matmul(h_q, wg, col0=0, ncols=g * LANES, tm=tm, tn=g * LANES, out_dtype=F32,
                    epilogue=_ep_sigmoid, name="nsa_proj_gate")
    o = _nsa_attention(q, gates, kv, kvc, rel_bias, batch, seq, heads)
    tn_o = _tile(d, 512)
    return _matmul(o, w_o, col0=0, ncols=d, tm=tm, tn=tn_o, out_dtype=F32, epilogue=_ep_residual,
                   extra=((x2, (tm, tn_o), lambda j, i: (i, j)),), name="nsa_out_proj")


def kernel(x, ret_norm_g, ret_w_in, ret_w_o, kv_norm_g, nsa_w_kv, cmp_k_pe, cmp_k_w1, cmp_k_w2, cmp_v_pe, cmp_v_w1, cmp_v_w2, rel_bias, nsa_norm_g, nsa_w_q, nsa_w_o, moe_norm_g, moe_w_router_group, moe_w_router_expert, moe_w_gate, moe_w_up, moe_w_down, final_norm_g):
    batch, seq, d = x.shape
    depth = moe_norm_g.shape[0]
    n_a = ret_norm_g.shape[0]
    x2 = x.reshape(batch * seq, d)
    cmp_params = (cmp_k_pe, cmp_k_w1, cmp_k_w2, cmp_v_pe, cmp_v_w1, cmp_v_w2)
    h_kv = h_q = None
    out = None
    for layer in range(depth):
        if layer < n_a:
            x2 = _retention_layer(x2, ret_norm_g[layer], ret_w_in[layer], ret_w_o[layer], batch, seq)
        else:
            j = layer - n_a
            if h_q is None:
                h_kv = _rmsnorm(x2, kv_norm_g)
                h_q = _rmsnorm(x2, nsa_norm_g[j])
            x2 = _nsa_layer(x2, h_kv, h_q, nsa_w_kv, cmp_params, rel_bias, nsa_w_q[j], nsa_w_o[j],
                            batch, seq)
        hm, route = _route(x2, moe_norm_g[layer], moe_w_router_group[layer], moe_w_router_expert[layer])
        pairs = _moe_ffn(hm, route, moe_w_gate[layer], moe_w_up[layer], moe_w_down[layer], tb=LANES)
        if layer == depth - 1:
            (out,) = _combine(x2, pairs, [final_norm_g], [F32], want_x=False)
        elif layer + 1 == n_a:
            j = layer + 1 - n_a
            x2, h_kv, h_q = _combine(x2, pairs, [kv_norm_g, nsa_norm_g[j]], [BF16, BF16], want_x=True)
        elif layer + 1 < n_a:
            (x2,) = _combine(x2, pairs, [], [], want_x=True)
        else:
            j = layer + 1 - n_a
            x2, h_q = _combine(x2, pairs, [nsa_norm_g[j]], [BF16], want_x=True)
    return out.reshape(batch, seq, d)
```

```python
import functools
import math

import numpy as np
import jax
import jax.numpy as jnp
from jax import lax
from jax.experimental import pallas as pl
from jax.experimental.pallas import tpu as pltpu

RMS_EPS = 1e-6
RET_DK = 256
RET_DV = 2 * RET_DK
ROPE_BASE = 10000.0
NSA_DK = 128
NSA_GROUPS = 4
CMP_LEN = 32
CMP_STRIDE = 16
SEL_BLOCK = 64
SEL_TOPK = 16
WINDOW = 512
FORCED_SCORE = 1e4
INVALID_SCORE = -1e9
REL_BUCKETS = 32
REL_MAX_DIST = 128
MOE_GROUPS = 8
MOE_EPG = 8
MOE_EXPERTS = MOE_GROUPS * MOE_EPG
MOE_TOPK = 2

LANES = 128
V7X_VMEM_LIMIT = 56 * 1024 * 1024
NEG = -1e30

BF16 = jnp.bfloat16
F32 = jnp.float32


def _tile(dim, pref):
    t = min(dim, pref)
    while dim % t:
        t //= 2
    return t


def _cparams(sem, vmem=None):
    return pltpu.CompilerParams(dimension_semantics=sem, vmem_limit_bytes=vmem)


def _norm_kernel(x_ref, g_ref, o_ref):
    x = x_ref[...]
    y = x * lax.rsqrt(jnp.mean(x * x, axis=-1, keepdims=True) + RMS_EPS)
    o_ref[...] = (y * g_ref[...]).astype(o_ref.dtype)


def _rmsnorm(x2, g, out_dtype=BF16):
    m, d = x2.shape
    tm = _tile(m, 256)
    return pl.pallas_call(
        _norm_kernel,
        grid=(m // tm,),
        in_specs=[pl.BlockSpec((tm, d), lambda i: (i, 0)),
                  pl.BlockSpec((1, d), lambda i: (0, 0))],
        out_specs=pl.BlockSpec((tm, d), lambda i: (i, 0)),
        out_shape=jax.ShapeDtypeStruct((m, d), out_dtype),
        compiler_params=_cparams(("parallel",), V7X_VMEM_LIMIT),
        name="rmsnorm",
    )(x2, g.reshape(1, d))


def _mm_kernel(a_ref, w_ref, *rest, n_extra, epilogue):
    extra = rest[:n_extra]
    o_ref = rest[n_extra]
    wbf_ref = rest[n_extra + 1]

    @pl.when(pl.program_id(1) == 0)
    def _():
        wbf_ref[...] = w_ref[...].astype(BF16)

    acc = jnp.dot(a_ref[...], wbf_ref[...], preferred_element_type=F32)
    epilogue(acc, o_ref, *extra)


def _matmul(a, w, layer, *, col0, ncols, tm, tn, out_dtype, epilogue, extra=(), name):
    m, k = a.shape
    assert col0 % tn == 0 and ncols % tn == 0 and m % tm == 0
    joff = col0 // tn
    in_specs = [pl.BlockSpec((tm, k), lambda j, i: (i, 0)),
                pl.BlockSpec((None, k, tn), lambda j, i: (layer, 0, j + joff))]
    args = [a, w]
    for arr, bshape, imap in extra:
        in_specs.append(pl.BlockSpec(bshape, imap))
        args.append(arr)
    return pl.pallas_call(
        functools.partial(_mm_kernel, n_extra=len(extra), epilogue=epilogue),
        grid=(ncols // tn, m // tm),
        in_specs=in_specs,
        out_specs=pl.BlockSpec((tm, tn), lambda j, i: (i, j)),
        out_shape=jax.ShapeDtypeStruct((m, ncols), out_dtype),
        scratch_shapes=[pltpu.VMEM((k, tn), BF16)],
        compiler_params=_cparams(("arbitrary", "arbitrary"), V7X_VMEM_LIMIT),
        name=name,
    )(*args)


def _ep_store(acc, o_ref):
    o_ref[...] = acc.astype(o_ref.dtype)


def _ep_residual(acc, o_ref, r_ref):
    o_ref[...] = (r_ref[...] + acc).astype(o_ref.dtype)


def _ep_scale(acc, o_ref, *, scale):
    o_ref[...] = (acc * scale).astype(o_ref.dtype)


def _ep_sigmoid(acc, o_ref):
    o_ref[...] = jax.nn.sigmoid(acc).astype(o_ref.dtype)


def _ep_rotary(acc, o_ref, cos_ref, sin_ref, *, k_first_tile, k_scale):
    scale = jnp.where(pl.program_id(0) >= k_first_tile, k_scale, 1.0).astype(F32)
    cos = cos_ref[...] * scale
    sin = sin_ref[...] * scale
    half = RET_DK // 2
    for hd in range(acc.shape[1] // RET_DK):
        c0 = hd * RET_DK
        x1 = acc[:, c0:c0 + half]
        x2 = acc[:, c0 + half:c0 + RET_DK]
        o_ref[:, c0:c0 + half] = (x1 * cos - x2 * sin).astype(o_ref.dtype)
        o_ref[:, c0 + half:c0 + RET_DK] = (x1 * sin + x2 * cos).astype(o_ref.dtype)


def _ret_kernel(dchunk_ref, q_ref, k_ref, v_ref, g_ref, dintra_ref, dq_ref, dk_ref, y_ref, state_ref):
    h = pl.program_id(1)
    c = pl.program_id(2)

    @pl.when(c == 0)
    def _():
        state_ref[...] = jnp.zeros_like(state_ref)

    q = q_ref[...]
    k = k_ref[...]
    v = v_ref[...]
    scores = lax.dot_general(q, k, (((1,), (1,)), ((), ())), preferred_element_type=F32)
    scores = scores * dintra_ref[0]
    intra = jnp.dot(scores.astype(BF16), v, preferred_element_type=F32)
    cross = jnp.dot(q, state_ref[...].astype(BF16), preferred_element_type=F32)
    dq = dq_ref[0]
    dv = v.shape[1]
    out = jnp.concatenate(
        [intra[:, j:j + LANES] + cross[:, j:j + LANES] * dq for j in range(0, dv, LANES)], axis=1)
    kt = (k.astype(F32).T * dk_ref[0]).astype(BF16)
    state_ref[...] = dchunk_ref[h] * state_ref[...] + jnp.dot(kt, v, preferred_element_type=F32)
    out = out * lax.rsqrt(jnp.mean(out * out, axis=-1, keepdims=True) + RMS_EPS)
    gate = g_ref[...].astype(F32)
    y_ref[...] = (jax.nn.silu(gate) * out).astype(y_ref.dtype)


def _retention_core(qk, vg, batch, seq, heads):
    m = batch * seq
    c = _tile(seq, 256)
    nc = seq // c
    log_gamma = jnp.log1p(-jnp.exp2(-5.0 - jnp.arange(heads, dtype=F32)))
    idx = jnp.arange(c, dtype=F32)
    diff = idx[:, None] - idx[None, :]
    dintra = jnp.where(diff >= 0, jnp.exp(log_gamma[:, None, None] * jnp.maximum(diff, 0.0)), 0.0)
    dq = jnp.broadcast_to(jnp.exp(log_gamma[:, None] * (idx + 1.0))[:, :, None], (heads, c, LANES))
    dk = jnp.exp(log_gamma[:, None] * (c - 1.0 - idx))[:, None, :]
    dchunk = jnp.exp(log_gamma * c)
    grid_spec = pltpu.PrefetchScalarGridSpec(
        num_scalar_prefetch=1,
        grid=(batch, heads, nc),
        in_specs=[
            pl.BlockSpec((c, RET_DK), lambda b, h, i, s: (b * nc + i, h)),
            pl.BlockSpec((c, RET_DK), lambda b, h, i, s: (b * nc + i, heads + h)),
            pl.BlockSpec((c, RET_DV), lambda b, h, i, s: (b * nc + i, h)),
            pl.BlockSpec((c, RET_DV), lambda b, h, i, s: (b * nc + i, heads + h)),
            pl.BlockSpec((1, c, c), lambda b, h, i, s: (h, 0, 0)),
            pl.BlockSpec((1, c, LANES), lambda b, h, i, s: (h, 0, 0)),
            pl.BlockSpec((1, 1, c), lambda b, h, i, s: (h, 0, 0)),
        ],
        out_specs=pl.BlockSpec((c, RET_DV), lambda b, h, i, s: (b * nc + i, h)),
        scratch_shapes=[pltpu.VMEM((RET_DK, RET_DV), F32)],
    )
    return pl.pallas_call(
        _ret_kernel,
        grid_spec=grid_spec,
        out_shape=jax.ShapeDtypeStruct((m, heads * RET_DV), BF16),
        compiler_params=_cparams(("parallel", "parallel", "arbitrary"), V7X_VMEM_LIMIT),
        name="retention_core",
    )(dchunk, qk, qk, vg, vg, dintra, dq, dk)


def _retention_layer(x2, norm_g, w_in, w_o, layer, batch, seq):
    m, d = x2.shape
    heads = d // RET_DK
    hq = heads * RET_DK
    h = _rmsnorm(x2, norm_g)
    pos = jnp.arange(seq, dtype=F32)
    half = RET_DK // 2
    inv = 1.0 / (ROPE_BASE ** (jnp.arange(half, dtype=F32) / half))
    ang = pos[:, None] * inv[None, :]
    cos, sin = jnp.cos(ang), jnp.sin(ang)
    tm = _tile(seq, 512)
    tn = _tile(hq, 512)
    nrow = seq // tm
    rot = functools.partial(_ep_rotary, k_first_tile=hq // tn, k_scale=RET_DK ** -0.5)
    qk = _matmul(h, w_in, layer, col0=0, ncols=2 * hq, tm=tm, tn=tn, out_dtype=BF16, epilogue=rot,
                 extra=((cos, (tm, half), lambda j, i: (i % nrow, 0)),
                        (sin, (tm, half), lambda j, i: (i % nrow, 0))),
                 name="ret_proj_qk")
    vg = _matmul(h, w_in, layer, col0=2 * hq, ncols=2 * heads * RET_DV, tm=tm, tn=tn, out_dtype=BF16,
                 epilogue=_ep_store, name="ret_proj_vg")
    y = _retention_core(qk, vg, batch, seq, heads)
    tn_o = _tile(d, 256)
    return _matmul(y, w_o, layer, col0=0, ncols=d, tm=tm, tn=tn_o, out_dtype=F32, epilogue=_ep_residual,
                   extra=((x2, (tm, tn_o), lambda j, i: (i, j)),), name="ret_out_proj")


def _route_kernel(x_ref, g_ref, wr_ref, h_ref, r_ref):
    x = x_ref[...]
    hn = x * lax.rsqrt(jnp.mean(x * x, axis=-1, keepdims=True) + RMS_EPS) * g_ref[...]
    h_ref[...] = hn
    logits = jnp.dot(hn, wr_ref[...], preferred_element_type=F32, precision=lax.Precision.HIGHEST)
    lane = lax.broadcasted_iota(jnp.int32, logits.shape, 1)
    is_g = lane < MOE_GROUPS
    lg = jnp.where(is_g, logits, -jnp.inf)
    mg = jnp.max(lg, axis=-1, keepdims=True)
    grp = jnp.min(jnp.where(lg == mg, lane, LANES), axis=-1, keepdims=True)
    p_grp = 1.0 / jnp.sum(jnp.where(is_g, jnp.exp(lg - mg), 0.0), axis=-1, keepdims=True)
    e_lane = lane - MOE_GROUPS
    is_e = (e_lane >= 0) & (e_lane < MOE_EXPERTS) & ((e_lane >> 3) == grp)
    le = jnp.where(is_e, logits, -jnp.inf)
    m1 = jnp.max(le, axis=-1, keepdims=True)
    i1 = jnp.min(jnp.where(le == m1, lane, LANES), axis=-1, keepdims=True)
    le2 = jnp.where(lane == i1, -jnp.inf, le)
    m2 = jnp.max(le2, axis=-1, keepdims=True)
    i2 = jnp.min(jnp.where(le2 == m2, lane, LANES), axis=-1, keepdims=True)
    t = jnp.exp(m2 - m1)
    w1 = p_grp / (1.0 + t)
    w2 = p_grp * t / (1.0 + t)
    e1 = (i1 - MOE_GROUPS).astype(F32)
    e2 = (i2 - MOE_GROUPS).astype(F32)
    r_ref[...] = jnp.where(lane == 0, e1, jnp.where(lane == 1, e2,
                           jnp.where(lane == 2, w1, jnp.where(lane == 3, w2, 0.0))))


def _route(x2, norm_g, w_rg, w_re):
    m, d = x2.shape
    tm = _tile(m, 256)
    wr = jnp.concatenate(
        [w_rg, w_re, jnp.zeros((d, LANES - MOE_GROUPS - MOE_EXPERTS), F32)], axis=1)
    return pl.pallas_call(
        _route_kernel,
        grid=(m // tm,),
        in_specs=[pl.BlockSpec((tm, d), lambda i: (i, 0)),
                  pl.BlockSpec((1, d), lambda i: (0, 0)),
                  pl.BlockSpec((d, LANES), lambda i: (0, 0))],
        out_specs=[pl.BlockSpec((tm, d), lambda i: (i, 0)),
                   pl.BlockSpec((tm, LANES), lambda i: (i, 0))],
        out_shape=[jax.ShapeDtypeStruct((m, d), F32),
                   jax.ShapeDtypeStruct((m, LANES), F32)],
        compiler_params=_cparams(("parallel",), V7X_VMEM_LIMIT),
        name="moe_route",
    )(x2, norm_g.reshape(1, d), wr)


def _ffn_kernel(blk_e_ref, nvalid_ref, nused_ref,
                src_cur_ref, src_nxt_ref, dst_ref, roww_ref, wg_ref, wu_ref, wd_ref, h_hbm,
                pairs_hbm,
                wgb_ref, wub_ref, wdb_ref, xbuf_ref, ybuf_ref, gsem, ssem, *, tb):
    i = pl.program_id(0)
    n_used = nused_ref[0]
    slot = i % 2

    def gather_copy(src_ref, r, s):
        return pltpu.make_async_copy(h_hbm.at[pl.ds(src_ref[0, 0, r], 1)],
                                     xbuf_ref.at[s, pl.ds(r, 1)], gsem.at[s])

    def scatter_copy(d_row, r, s):
        return pltpu.make_async_copy(ybuf_ref.at[s, pl.ds(r, 1)],
                                     pairs_hbm.at[pl.ds(d_row, 1)], ssem.at[s])

    def start_gather(src_ref, s):
        for r in range(tb):
            gather_copy(src_ref, r, s).start()

    def wait_gather(s):
        pltpu.make_async_copy(h_hbm.at[pl.ds(0, tb)], xbuf_ref.at[s], gsem.at[s]).wait()

    def start_scatter(s, n):
        @pl.when(n == tb)
        def _():
            for r in range(tb):
                scatter_copy(dst_ref[0, 0, r], r, s).start()

        @pl.when(n != tb)
        def _():
            def body(r, carry):
                scatter_copy(dst_ref[0, 0, r], r, s).start()
                return carry
            lax.fori_loop(0, n, body, 0)

    def wait_scatter(s, n):
        @pl.when(n == tb)
        def _():
            pltpu.make_async_copy(ybuf_ref.at[s], pairs_hbm.at[pl.ds(0, tb)], ssem.at[s]).wait()

        @pl.when(n != tb)
        def _():
            def body(r, carry):
                scatter_copy(0, r, s).wait()
                return carry
            lax.fori_loop(0, n, body, 0)

    @pl.when(i == 0)
    def _():
        start_gather(src_cur_ref, 0)

    @pl.when(i < n_used)
    def _():
        @pl.when(i + 1 < n_used)
        def _():
            start_gather(src_nxt_ref, 1 - slot)

        first = jnp.logical_or(i == 0, blk_e_ref[i] != blk_e_ref[jnp.maximum(i - 1, 0)])

        @pl.when(first)
        def _():
            wgb_ref[...] = wg_ref[...].astype(BF16)
            wub_ref[...] = wu_ref[...].astype(BF16)
            wdb_ref[...] = wd_ref[...].astype(BF16)

        wait_gather(slot)
        xb = xbuf_ref[slot].astype(BF16)
        hid = (jax.nn.silu(jnp.dot(xb, wgb_ref[...], preferred_element_type=F32))
               * jnp.dot(xb, wub_ref[...], preferred_element_type=F32))
        y = jnp.dot(hid.astype(BF16), wdb_ref[...], preferred_element_type=F32)

        @pl.when(i >= 2)
        def _():
            wait_scatter(slot, nvalid_ref[jnp.maximum(i - 2, 0)])

        ybuf_ref[slot] = y * roww_ref[:, 0:1]
        nv = nvalid_ref[i]
        start_scatter(slot, nv)

        @pl.when(i == n_used - 1)
        def _():
            @pl.when(i >= 1)
            def _():
                wait_scatter(1 - slot, nvalid_ref[jnp.maximum(i - 1, 0)])
            wait_scatter(slot, nv)


def _moe_ffn(h, route, w_gate, w_up, w_down, layer, tb):
    n, d = h.shape
    _, e, _, f = w_gate.shape
    k = MOE_TOPK
    mcopies = n * k
    flat_e = route[:, 0:2].astype(jnp.int32).reshape(mcopies)
    flat_w = route[:, 2:4].reshape(mcopies)
    order = jnp.argsort(flat_e).astype(jnp.int32)
    counts = jnp.sum((flat_e[:, None] == jnp.arange(e)[None, :]).astype(jnp.int32), axis=0)
    padded = (counts + tb - 1) // tb * tb
    pad_end = jnp.cumsum(padded)
    pad_start = pad_end - padded
    start = jnp.cumsum(counts) - counts
    nb = (mcopies + e * (tb - 1) + tb - 1) // tb
    p = nb * tb
    blk_start = jnp.arange(nb) * tb
    blk_e = jnp.minimum(jnp.sum((blk_start[:, None] >= pad_end[None, :]).astype(jnp.int32), axis=1),
                        e - 1).astype(jnp.int32)
    blk_off = blk_start - pad_start[blk_e]
    nvalid = jnp.clip(counts[blk_e] - blk_off, 0, tb).astype(jnp.int32)
    n_used = (pad_end[-1] // tb).astype(jnp.int32).reshape(1)
    rows = jnp.arange(tb)[None, :]
    valid = rows < nvalid[:, None]
    sorted_idx = jnp.clip((start[blk_e] + blk_off)[:, None] + rows, 0, mcopies - 1)
    copy_id = order[sorted_idx]
    tok = copy_id // k
    src3 = jnp.where(valid, tok, 0).reshape(nb, 1, tb)
    dst3 = jnp.where(valid, (copy_id % k) * n + tok, 0).reshape(nb, 1, tb)
    row_w = jnp.where(valid, flat_w[copy_id], 0.0).reshape(p)
    roww = jnp.broadcast_to(row_w[:, None], (p, LANES))

    smem = functools.partial(pl.BlockSpec, memory_space=pltpu.SMEM)
    grid_spec = pltpu.PrefetchScalarGridSpec(
        num_scalar_prefetch=3,
        grid=(nb,),
        in_specs=[
            smem((1, 1, tb), lambda i, be, nv, nu: (i, 0, 0)),
            smem((1, 1, tb), lambda i, be, nv, nu: (jnp.minimum(i + 1, nb - 1), 0, 0)),
            smem((1, 1, tb), lambda i, be, nv, nu: (i, 0, 0)),
            pl.BlockSpec((tb, LANES), lambda i, be, nv, nu: (i, 0)),
            pl.BlockSpec((None, None, d, f), lambda i, be, nv, nu: (layer, be[i], 0, 0)),
            pl.BlockSpec((None, None, d, f), lambda i, be, nv, nu: (layer, be[i], 0, 0)),
            pl.BlockSpec((None, None, f, d), lambda i, be, nv, nu: (layer, be[i], 0, 0)),
            pl.BlockSpec(memory_space=pl.ANY),
        ],
        out_specs=pl.BlockSpec(memory_space=pl.ANY),
        scratch_shapes=[
            pltpu.VMEM((d, f), BF16), pltpu.VMEM((d, f), BF16), pltpu.VMEM((f, d), BF16),
            pltpu.VMEM((2, tb, d), F32), pltpu.VMEM((2, tb, d), F32),
            pltpu.SemaphoreType.DMA((2,)), pltpu.SemaphoreType.DMA((2,)),
        ],
    )
    return pl.pallas_call(
        functools.partial(_ffn_kernel, tb=tb),
        grid_spec=grid_spec,
        out_shape=jax.ShapeDtypeStruct((k * n, d), F32),
        compiler_params=_cparams(("arbitrary",), V7X_VMEM_LIMIT),
        name="moe_ffn",
    )(blk_e, nvalid, n_used, src3, src3, dst3, roww, w_gate, w_up, w_down, h)


def _combine_kernel(x_ref, p0_ref, p1_ref, *rest, n_norm, want_x):
    g_refs = rest[:n_norm]
    outs = rest[n_norm:]
    x = x_ref[...] + (p0_ref[...] + p1_ref[...])
    oi = 0
    if want_x:
        outs[0][...] = x
        oi = 1
    r = lax.rsqrt(jnp.mean(x * x, axis=-1, keepdims=True) + RMS_EPS)
    y = x * r
    for g_ref, o_ref in zip(g_refs, outs[oi:]):
        o_ref[...] = (y * g_ref[...]).astype(o_ref.dtype)


def _combine(x2, pairs, gains, out_dtypes, want_x):
    m, d = x2.shape
    tm = _tile(m, 256)
    nrow = m // tm
    in_specs = [pl.BlockSpec((tm, d), lambda i: (i, 0)),
                pl.BlockSpec((tm, d), lambda i: (i, 0)),
                pl.BlockSpec((tm, d), lambda i: (i + nrow, 0))]
    in_specs += [pl.BlockSpec((1, d), lambda i: (0, 0)) for _ in gains]
    out_shape = ([jax.ShapeDtypeStruct((m, d), F32)] if want_x else []) + [
        jax.ShapeDtypeStruct((m, d), dt) for dt in out_dtypes]
    out_specs = [pl.BlockSpec((tm, d), lambda i: (i, 0)) for _ in out_shape]
    return pl.pallas_call(
        functools.partial(_combine_kernel, n_norm=len(gains), want_x=want_x),
        grid=(nrow,),
        in_specs=in_specs,
        out_specs=out_specs,
        out_shape=out_shape,
        compiler_params=_cparams(("parallel",), V7X_VMEM_LIMIT),
        name="moe_combine",
    )(x2, pairs, pairs, *[g.reshape(1, d) for g in gains])


def _t5_bucket(dist):
    n = jnp.maximum(dist, 0)
    exact = REL_BUCKETS // 2
    nf = jnp.maximum(n, 1).astype(F32)
    large = exact + (jnp.log(nf / exact) / math.log(REL_MAX_DIST / exact)
                     * (REL_BUCKETS - exact)).astype(jnp.int32)
    return jnp.where(n < exact, n, jnp.minimum(large, REL_BUCKETS - 1))


def _bias_lookup(table, bucket):
    onehot = jax.nn.one_hot(bucket, REL_BUCKETS, dtype=F32)
    return jnp.einsum('...b,bh->...h', onehot, table, precision=lax.Precision.HIGHEST)


def _cmp_kernel(rk_ref, rv_ref, pe_ref, w1_ref, w2_ref, o_ref):
    half = w1_ref.shape[1] // 2
    for s, r_ref in enumerate((rk_ref, rv_ref)):
        r = r_ref[0, 0].astype(F32)
        ra = (r + pe_ref[s, :, :half]).astype(BF16)
        rb = (r + pe_ref[s, :, half:]).astype(BF16)
        ha = jnp.dot(ra, w1_ref[s, :half, :], preferred_element_type=F32)
        hb = jnp.dot(rb, w1_ref[s, half:, :], preferred_element_type=F32)
        hb_next = jnp.concatenate([hb[1:], hb[:1]], axis=0)
        hid = jax.nn.gelu(ha + hb_next)
        res = jnp.dot(hid.astype(BF16), w2_ref[s], preferred_element_type=F32)
        o_ref[0, 0, s] = (res if s == 0 else res.T).astype(o_ref.dtype)


def _compress(kv, batch, seq, pes, w1s, w2s):
    g, dk = NSA_GROUPS, NSA_DK
    nr = seq // CMP_STRIDE
    assert nr == dk
    ct = kv[:, :2 * g * dk].reshape(batch, nr, CMP_STRIDE, 2, g, dk)
    r = ct.transpose(3, 0, 4, 1, 2, 5).reshape(2, batch, g, nr, CMP_STRIDE * dk)
    pe = jnp.stack(pes).reshape(2, 1, CMP_LEN * dk)
    w1 = jnp.stack(w1s).reshape(2, CMP_LEN * dk, dk).astype(BF16)
    w2 = jnp.stack(w2s).astype(BF16)
    rspec = pl.BlockSpec((1, 1, nr, CMP_STRIDE * dk), lambda b, gi: (b, gi, 0, 0))
    return pl.pallas_call(
        _cmp_kernel,
        grid=(batch, g),
        in_specs=[rspec, rspec,
                  pl.BlockSpec((2, 1, CMP_LEN * dk), lambda b, gi: (0, 0, 0)),
                  pl.BlockSpec((2, CMP_LEN * dk, dk), lambda b, gi: (0, 0, 0)),
                  pl.BlockSpec((2, dk, dk), lambda b, gi: (0, 0, 0))],
        out_specs=pl.BlockSpec((1, 1, 2, nr, dk), lambda b, gi: (b, gi, 0, 0, 0)),
        out_shape=jax.ShapeDtypeStruct((batch, g, 2, nr, dk), BF16),
        compiler_params=_cparams(("parallel", "parallel"), V7X_VMEM_LIMIT),
        name="nsa_compress",
    )(r[0], r[1], pe, w1, w2)


def _nsa_kernel(q_ref, gate_ref, kvc_ref, ks_ref, kw_ref, vst_ref, vwt_ref, biasc_ref, bias_ref,
                ovt_ref, ext_ref, o_ref,
                selpen_ref, m_ref, l_ref, acc_ref, *, hg, tq, n_cmp, n_sblk):
    qb = pl.program_id(2)
    dk = NSA_DK
    nt_dims = (((1,), (1,)), ((), ()))
    q8 = jnp.concatenate([q_ref[:, h * dk:(h + 1) * dk] for h in range(hg)], axis=0)
    kj = lax.broadcasted_iota(jnp.int32, (LANES, tq), 0)
    qi = lax.broadcasted_iota(jnp.int32, (LANES, tq), 1)
    t_abs = qb * tq + qi

    def tile_h(a):
        return jnp.concatenate([a] * hg, axis=1)

    s = lax.dot_general(kvc_ref[0], q8, nt_dims, preferred_element_type=F32) + biasc_ref[...]
    valid_c = (kj * CMP_STRIDE + (CMP_LEN - 1) <= t_abs) & (kj < n_cmp)
    s = s + tile_h(jnp.where(valid_c, 0.0, NEG))
    mx = jnp.maximum(jnp.max(s, axis=0, keepdims=True), 0.5 * NEG)
    p = jnp.exp(s - mx)
    p = p * (1.0 / jnp.maximum(jnp.sum(p, axis=0, keepdims=True), 1e-30))
    o_c = jnp.dot(kvc_ref[1], p.astype(BF16), preferred_element_type=F32)
    psum = p[:, 0:tq]
    for h in range(1, hg):
        psum = psum + p[:, h * tq:(h + 1) * tq]
    p_hi = psum.astype(BF16)
    p_lo = (psum - p_hi.astype(F32)).astype(BF16)
    imp = (jnp.dot(ovt_ref[...], p_hi, preferred_element_type=F32)
           + jnp.dot(ovt_ref[...], p_lo, preferred_element_type=F32))

    imp = imp[:n_sblk]
    kb_i = lax.broadcasted_iota(jnp.int32, (n_sblk, tq), 0)
    t_b = qb * tq + lax.broadcasted_iota(jnp.int32, (n_sblk, tq), 1)
    cur = lax.shift_right_logical(t_b, int(math.log2(SEL_BLOCK)))
    valid_b = kb_i * SEL_BLOCK <= t_b
    forced = (kb_i == 0) | (kb_i == cur) | (kb_i == cur - 1)
    score = jnp.where(valid_b, jnp.where(forced, FORCED_SCORE, imp), INVALID_SCORE)
    rank = jnp.zeros((n_sblk, tq), F32)
    for j in range(n_sblk):
        cj = score[j:j + 1, :]
        ahead = (cj > score) | ((cj == score) & (kb_i > j))
        rank = rank + jnp.where(ahead, 1.0, 0.0)
    sel = jnp.where(rank < min(SEL_TOPK, n_sblk), 1.0, 0.0)
    if n_sblk < LANES:
        sel = jnp.concatenate([sel, jnp.zeros((LANES - n_sblk, tq), F32)], axis=0)
    selmask = jnp.dot(ext_ref[...], sel.astype(BF16), preferred_element_type=F32)
    selpen_ref[...] = ((selmask - 1.0) * (-NEG)).reshape(selpen_ref.shape)

    def attend(kt, vtt, addend, br):
        sc = lax.dot_general(kt, q8, nt_dims, preferred_element_type=F32) + addend
        m_old = m_ref[br]
        m_new = jnp.maximum(m_old, jnp.max(sc, axis=0, keepdims=True))
        alpha = jnp.exp(m_old - m_new)
        pt = jnp.exp(sc - m_new)
        l_ref[br] = alpha * l_ref[br] + jnp.sum(pt, axis=0, keepdims=True)
        acc_ref[br] = alpha * acc_ref[br] + jnp.dot(vtt, pt.astype(BF16), preferred_element_type=F32)
        m_ref[br] = m_new

    m_ref[...] = jnp.full(m_ref.shape, 0.5 * NEG, F32)
    l_ref[...] = jnp.zeros(l_ref.shape, F32)
    acc_ref[...] = jnp.zeros(acc_ref.shape, F32)
    n_far = (bias_ref.shape[0] - 1)

    def sel_body(kb, carry):
        off = pl.multiple_of(kb * LANES, LANES)
        dist = (qb - kb) * tq + qi - kj
        pen = selpen_ref[kb] + jnp.where(dist >= 0, 0.0, NEG)
        attend(ks_ref[pl.ds(off, LANES), :], vst_ref[kb],
               bias_ref[jnp.minimum(qb - kb, n_far)] + tile_h(pen), 0)
        return carry
    lax.fori_loop(0, qb + 1, sel_body, 0)

    def win_body(kb, carry):
        off = pl.multiple_of(kb * LANES, LANES)
        dist = (qb - kb) * tq + qi - kj
        pen = jnp.where((dist >= 0) & (dist < WINDOW), 0.0, NEG)
        attend(kw_ref[pl.ds(off, LANES), :], vwt_ref[kb],
               bias_ref[jnp.minimum(qb - kb, n_far)] + tile_h(pen), 1)
        return carry
    lax.fori_loop(jnp.maximum(qb - WINDOW // LANES, 0), qb + 1, win_body, 0)

    o_s = acc_ref[0] * (1.0 / jnp.maximum(l_ref[0], 1e-30))
    o_w = acc_ref[1] * (1.0 / jnp.maximum(l_ref[1], 1e-30))
    gt = gate_ref[...].T
    for h in range(hg):
        cols = slice(h * tq, (h + 1) * tq)
        out_t = (gt[3 * h:3 * h + 1] * o_c[:, cols] + gt[3 * h + 1:3 * h + 2] * o_s[:, cols]
                 + gt[3 * h + 2:3 * h + 3] * o_w[:, cols])
        o_ref[:, h * dk:(h + 1) * dk] = out_t.T.astype(o_ref.dtype)


def _nsa_attention(q, gates, kv, kvc, rel_bias, batch, seq, heads):
    g, dk = NSA_GROUPS, NSA_DK
    hg = heads // g
    tq = LANES
    nqb = seq // tq
    n_cmp = (seq - CMP_LEN) // CMP_STRIDE + 1
    n_sblk = seq // SEL_BLOCK
    nr = seq // CMP_STRIDE
    assert nr == LANES and n_sblk <= LANES and n_sblk % 8 == 0 and tq == LANES
    nt = seq // LANES
    rows = hg * tq
    table = rel_bias.astype(F32)
    tt = jnp.arange(seq)
    cmp_end = jnp.arange(LANES) * CMP_STRIDE + CMP_LEN - 1
    bias_c = _bias_lookup(table, _t5_bucket(tt[:, None] - cmp_end[None, :]))
    bias_c = bias_c.reshape(nqb, tq, LANES, g, hg).transpose(3, 0, 2, 4, 1).reshape(g, nqb, LANES, rows)
    ii = jnp.arange(tq)
    n_far = -(-(REL_MAX_DIST + tq - 1) // tq)
    dist = jnp.arange(n_far + 1)[:, None, None] * tq + ii[None, None, :] - ii[None, :, None]
    bias_t = _bias_lookup(table, _t5_bucket(dist))
    bias_t = bias_t.reshape(n_far + 1, tq, tq, g, hg).transpose(0, 3, 1, 4, 2).reshape(n_far + 1, g, tq, rows)
    cs = np.arange(LANES) * CMP_STRIDE
    ss = np.arange(LANES) * SEL_BLOCK
    ov = np.clip(np.minimum(cs[:, None] + CMP_LEN, ss[None, :] + SEL_BLOCK)
                 - np.maximum(cs[:, None], ss[None, :]), 0, None) / CMP_LEN
    ov[n_cmp:, :] = 0.0
    ov[:, n_sblk:] = 0.0
    overlap_t = jnp.asarray(ov.T, BF16)
    ex = ((np.arange(seq)[:, None] // SEL_BLOCK) == np.arange(LANES)[None, :]).astype(np.float32)
    expand_t = jnp.asarray(ex, BF16)
    kv6 = kv.reshape(batch, nt, LANES, 6, g, dk)
    vt = jnp.stack([kv6[:, :, :, 3], kv6[:, :, :, 5]]).transpose(0, 1, 4, 2, 5, 3)

    kspec = lambda blk: pl.BlockSpec((seq, dk), lambda b, gi, qb: (b, blk * g + gi))
    vspec = lambda br: pl.BlockSpec((None, None, None, nt, dk, LANES),
                                    lambda b, gi, qb: (br, b, gi, 0, 0, 0))
    return pl.pallas_call(
        functools.partial(_nsa_kernel, hg=hg, tq=tq, n_cmp=n_cmp, n_sblk=n_sblk),
        grid=(batch, g, nqb),
        in_specs=[
            pl.BlockSpec((tq, hg * dk), lambda b, gi, qb: (b * nqb + qb, gi)),
            pl.BlockSpec((tq, LANES), lambda b, gi, qb: (b * nqb + qb, gi)),
            pl.BlockSpec((None, None, 2, nr, dk), lambda b, gi, qb: (b, gi, 0, 0, 0)),
            kspec(2), kspec(4), vspec(0), vspec(1),
            pl.BlockSpec((None, None, LANES, rows), lambda b, gi, qb: (gi, qb, 0, 0)),
            pl.BlockSpec((n_far + 1, None, tq, rows), lambda b, gi, qb: (0, gi, 0, 0)),
            pl.BlockSpec((LANES, LANES), lambda b, gi, qb: (0, 0)),
            pl.BlockSpec((seq, LANES), lambda b, gi, qb: (0, 0)),
        ],
        out_specs=pl.BlockSpec((tq, hg * dk), lambda b, gi, qb: (b * nqb + qb, gi)),
        out_shape=jax.ShapeDtypeStruct((batch * seq, heads * dk), BF16),
        scratch_shapes=[pltpu.VMEM((nt, LANES, tq), F32),
                        pltpu.VMEM((2, 1, rows), F32), pltpu.VMEM((2, 1, rows), F32),
                        pltpu.VMEM((2, dk, rows), F32)],
        compiler_params=_cparams(("parallel", "parallel", "arbitrary"), V7X_VMEM_LIMIT),
        name="nsa_attention",
    )(q, gates, kvc, kv, kv, vt, vt, bias_c, bias_t, overlap_t, expand_t)


def _nsa_layer(x2, h_kv, h_q, w_kv, cmp_params, rel_bias, w_q, w_o, layer, batch, seq):
    m, d = x2.shape
    g, dk = NSA_GROUPS, NSA_DK
    heads = d // dk
    hg = heads // g
    tm = _tile(seq, 512)
    kv = _matmul(h_kv, w_kv[None], 0, col0=0, ncols=6 * g * dk, tm=tm, tn=_tile(6 * g * dk, 512),
                 out_dtype=BF16, epilogue=_ep_store, name="nsa_proj_kv")
    (k_pe, k_w1, k_w2, v_pe, v_w1, v_w2) = cmp_params
    kvc = _compress(kv, batch, seq, (k_pe, v_pe), (k_w1, v_w1), (k_w2, v_w2))
    q = _matmul(h_q, w_q, layer, col0=0, ncols=heads * dk, tm=tm, tn=_tile(heads * dk, 512),
                out_dtype=BF16, epilogue=functools.partial(_ep_scale, scale=dk ** -0.5), name="nsa_proj_q")
    wg = w_q[layer, :, heads * dk:].reshape(d, g, hg * 3)
    wg = jnp.pad(wg, ((0, 0), (0, 0), (0, LANES - hg * 3))).reshape(1, d, g * LANES)
    gates = _matmul(h_q, wg, 0, col0=0, ncols=g * LANES, tm=tm, tn=g * LANES, out_dtype=F32,
                    epilogue=_ep_sigmoid, name="nsa_proj_gate")
    o = _nsa_attention(q, gates, kv, kvc, rel_bias, batch, seq, heads)
    tn_o = _tile(d, 512)
    return _matmul(o, w_o, layer, col0=0, ncols=d, tm=tm, tn=tn_o, out_dtype=F32, epilogue=_ep_residual,
                   extra=((x2, (tm, tn_o), lambda j, i: (i, j)),), name="nsa_out_proj")


def kernel(x, ret_norm_g, ret_w_in, ret_w_o, kv_norm_g, nsa_w_kv, cmp_k_pe, cmp_k_w1, cmp_k_w2, cmp_v_pe, cmp_v_w1, cmp_v_w2, rel_bias, nsa_norm_g, nsa_w_q, nsa_w_o, moe_norm_g, moe_w_router_group, moe_w_router_expert, moe_w_gate, moe_w_up, moe_w_down, final_norm_g):
    batch, seq, d = x.shape
    depth = moe_norm_g.shape[0]
    n_a = ret_norm_g.shape[0]
    x2 = x.reshape(batch * seq, d)
    cmp_params = (cmp_k_pe, cmp_k_w1, cmp_k_w2, cmp_v_pe, cmp_v_w1, cmp_v_w2)
    h_kv = h_q = None
    out = None
    for layer in range(depth):
        if layer < n_a:
            x2 = _retention_layer(x2, ret_norm_g[layer], ret_w_in, ret_w_o, layer, batch, seq)
        else:
            j = layer - n_a
            if h_q is None:
                h_kv = _rmsnorm(x2, kv_norm_g)
                h_q = _rmsnorm(x2, nsa_norm_g[j])
            x2 = _nsa_layer(x2, h_kv, h_q, nsa_w_kv, cmp_params, rel_bias, nsa_w_q, nsa_w_o, j,
                            batch, seq)
        hm, route = _route(x2, moe_norm_g[layer], moe_w_router_group[layer], moe_w_router_expert[layer])
        pairs = _moe_ffn(hm, route, moe_w_gate, moe_w_up, moe_w_down, layer, tb=LANES)
        if layer == depth - 1:
            (out,) = _combine(x2, pairs, [final_norm_g], [F32], want_x=False)
        elif layer + 1 == n_a:
            j = layer + 1 - n_a
            x2, h_kv, h_q = _combine(x2, pairs, [kv_norm_g, nsa_norm_g[j]], [BF16, BF16], want_x=True)
        elif layer + 1 < n_a:
            (x2,) = _combine(x2, pairs, [], [], want_x=True)
        else:
            j = layer + 1 - n_a
            x2, h_q = _combine(x2, pairs, [nsa_norm_g[j]], [BF16], want_x=True)
    return out.reshape(batch, seq, d)
```

```python
import functools
import math

import numpy as np
import jax
import jax.numpy as jnp
from jax import lax
from jax.experimental import pallas as pl
from jax.experimental.pallas import tpu as pltpu

RMS_EPS = 1e-6
RET_DK = 256
RET_DV = 2 * RET_DK
ROPE_BASE = 10000.0
NSA_DK = 128
NSA_GROUPS = 4
CMP_LEN = 32
CMP_STRIDE = 16
SEL_BLOCK = 64
SEL_TOPK = 16
WINDOW = 512
FORCED_SCORE = 1e4
INVALID_SCORE = -1e9
REL_BUCKETS = 32
REL_MAX_DIST = 128
MOE_GROUPS = 8
MOE_EPG = 8
MOE_EXPERTS = MOE_GROUPS * MOE_EPG
MOE_TOPK = 2

LANES = 128
V7X_VMEM_LIMIT = 56 * 1024 * 1024
NEG = -1e30
LOG2E = math.log2(math.e)

BF16 = jnp.bfloat16
F32 = jnp.float32


def _tile(dim, pref):
    t = min(dim, pref)
    while dim % t:
        t //= 2
    return t


def _cparams(sem, vmem=None):
    return pltpu.CompilerParams(dimension_semantics=sem, vmem_limit_bytes=vmem)


def _norm_kernel(x_ref, g_ref, o_ref):
    x = x_ref[...]
    y = x * lax.rsqrt(jnp.mean(x * x, axis=-1, keepdims=True) + RMS_EPS)
    o_ref[...] = (y * g_ref[...]).astype(o_ref.dtype)


def _rmsnorm(x2, g, out_dtype=BF16):
    m, d = x2.shape
    tm = _tile(m, 256)
    return pl.pallas_call(
        _norm_kernel,
        grid=(m // tm,),
        in_specs=[pl.BlockSpec((tm, d), lambda i: (i, 0)),
                  pl.BlockSpec((1, d), lambda i: (0, 0))],
        out_specs=pl.BlockSpec((tm, d), lambda i: (i, 0)),
        out_shape=jax.ShapeDtypeStruct((m, d), out_dtype),
        compiler_params=_cparams(("parallel",), V7X_VMEM_LIMIT),
        name="rmsnorm",
    )(x2, g.reshape(1, d))


MM_CHUNK = 256


def _mm_kernel(a_ref, w_ref, *rest, n_extra, epilogue, cast_w):
    extra = rest[:n_extra]
    o_ref = rest[n_extra]
    if cast_w:
        wbf_ref = rest[n_extra + 1]

        @pl.when(pl.program_id(1) == 0)
        def _():
            wbf_ref[...] = w_ref[...].astype(BF16)
    else:
        wbf_ref = w_ref

    a = a_ref[...]
    tn = o_ref.shape[1]
    chunk = min(tn, MM_CHUNK)
    for c0 in range(0, tn, chunk):
        cols = slice(c0, c0 + chunk)
        acc = jnp.dot(a, wbf_ref[:, cols], preferred_element_type=F32)
        epilogue(acc, o_ref, cols, *extra)


def _matmul(a, w, layer, *, col0, ncols, tm, tn, out_dtype, epilogue, extra=(), name):
    m, k = a.shape
    assert col0 % tn == 0 and ncols % tn == 0 and m % tm == 0
    joff = col0 // tn
    cast_w = w.dtype != BF16
    in_specs = [pl.BlockSpec((tm, k), lambda j, i: (i, 0)),
                pl.BlockSpec((None, k, tn), lambda j, i: (layer, 0, j + joff))]
    args = [a, w]
    for arr, bshape, imap in extra:
        in_specs.append(pl.BlockSpec(bshape, imap))
        args.append(arr)
    return pl.pallas_call(
        functools.partial(_mm_kernel, n_extra=len(extra), epilogue=epilogue, cast_w=cast_w),
        grid=(ncols // tn, m // tm),
        in_specs=in_specs,
        out_specs=pl.BlockSpec((tm, tn), lambda j, i: (i, j)),
        out_shape=jax.ShapeDtypeStruct((m, ncols), out_dtype),
        scratch_shapes=[pltpu.VMEM((k, tn), BF16)] if cast_w else [],
        compiler_params=_cparams(("arbitrary", "arbitrary"), V7X_VMEM_LIMIT),
        name=name,
    )(*args)


def _cast_kernel(w_ref, o_ref):
    o_ref[...] = w_ref[...].astype(o_ref.dtype)


def _cast_bf16(w, layer):
    _, k, n = w.shape
    tk = _tile(k, 512)
    return pl.pallas_call(
        _cast_kernel,
        grid=(k // tk,),
        in_specs=[pl.BlockSpec((None, tk, n), lambda i: (layer, i, 0))],
        out_specs=pl.BlockSpec((None, tk, n), lambda i: (0, i, 0)),
        out_shape=jax.ShapeDtypeStruct((1, k, n), BF16),
        compiler_params=_cparams(("parallel",), V7X_VMEM_LIMIT),
        name="cast_bf16",
    )(w)


def _ep_store(acc, o_ref, cols):
    o_ref[:, cols] = acc.astype(o_ref.dtype)


def _ep_residual(acc, o_ref, cols, r_ref):
    o_ref[:, cols] = (r_ref[:, cols] + acc).astype(o_ref.dtype)


def _ep_scale(acc, o_ref, cols, *, scale):
    o_ref[:, cols] = (acc * scale).astype(o_ref.dtype)


def _ep_sigmoid(acc, o_ref, cols):
    o_ref[:, cols] = jax.nn.sigmoid(acc).astype(o_ref.dtype)


def _ep_rotary(acc, o_ref, cols, cos_ref, sin_ref, *, k_first_tile, k_scale):
    assert acc.shape[1] == RET_DK
    scale = jnp.where(pl.program_id(0) >= k_first_tile, k_scale, 1.0).astype(F32)
    cos = cos_ref[...] * scale
    sin = sin_ref[...] * scale
    half = RET_DK // 2
    x1 = acc[:, :half]
    x2 = acc[:, half:]
    o_ref[:, cols.start:cols.start + half] = (x1 * cos - x2 * sin).astype(o_ref.dtype)
    o_ref[:, cols.start + half:cols.stop] = (x1 * sin + x2 * cos).astype(o_ref.dtype)


def _ret_kernel(dchunk_ref, q_ref, k_ref, v_ref, g_ref, dintra_ref, dq_ref, dk_ref, y_ref, state_ref, *, hp):
    c = pl.program_id(2)

    @pl.when(c == 0)
    def _():
        state_ref[...] = jnp.zeros_like(state_ref)

    for j in range(hp):
        h = pl.program_id(1) * hp + j
        q = q_ref[:, j * RET_DK:(j + 1) * RET_DK]
        k = k_ref[:, j * RET_DK:(j + 1) * RET_DK]
        v = v_ref[:, j * RET_DV:(j + 1) * RET_DV]
        scores = lax.dot_general(q, k, (((1,), (1,)), ((), ())), preferred_element_type=F32)
        scores = scores * dintra_ref[j]
        intra = jnp.dot(scores.astype(BF16), v, preferred_element_type=F32)
        cross = jnp.dot(q, state_ref[j].astype(BF16), preferred_element_type=F32)
        dq = dq_ref[j]
        out = jnp.concatenate(
            [intra[:, n:n + LANES] + cross[:, n:n + LANES] * dq for n in range(0, RET_DV, LANES)], axis=1)
        kt = (k.astype(F32).T * dk_ref[j]).astype(BF16)
        state_ref[j] = dchunk_ref[h] * state_ref[j] + jnp.dot(kt, v, preferred_element_type=F32)
        out = out * lax.rsqrt(jnp.mean(out * out, axis=-1, keepdims=True) + RMS_EPS)
        gate = g_ref[:, j * RET_DV:(j + 1) * RET_DV].astype(F32)
        y_ref[:, j * RET_DV:(j + 1) * RET_DV] = (jax.nn.silu(gate) * out).astype(y_ref.dtype)


def _retention_core(qk, vg, batch, seq, heads):
    m = batch * seq
    c = _tile(seq, 256)
    nc = seq // c
    log_gamma = jnp.log1p(-jnp.exp2(-5.0 - jnp.arange(heads, dtype=F32)))
    idx = jnp.arange(c, dtype=F32)
    diff = idx[:, None] - idx[None, :]
    dintra = jnp.where(diff >= 0, jnp.exp(log_gamma[:, None, None] * jnp.maximum(diff, 0.0)), 0.0)
    dq = jnp.broadcast_to(jnp.exp(log_gamma[:, None] * (idx + 1.0))[:, :, None], (heads, c, LANES))
    dk = jnp.exp(log_gamma[:, None] * (c - 1.0 - idx))[:, None, :]
    dchunk = jnp.exp(log_gamma * c)
    hp = 2 if heads % 2 == 0 else 1
    ng = heads // hp
    grid_spec = pltpu.PrefetchScalarGridSpec(
        num_scalar_prefetch=1,
        grid=(batch, ng, nc),
        in_specs=[
            pl.BlockSpec((c, hp * RET_DK), lambda b, h, i, s: (b * nc + i, h)),
            pl.BlockSpec((c, hp * RET_DK), lambda b, h, i, s: (b * nc + i, ng + h)),
            pl.BlockSpec((c, hp * RET_DV), lambda b, h, i, s: (b * nc + i, h)),
            pl.BlockSpec((c, hp * RET_DV), lambda b, h, i, s: (b * nc + i, ng + h)),
            pl.BlockSpec((hp, c, c), lambda b, h, i, s: (h, 0, 0)),
            pl.BlockSpec((hp, c, LANES), lambda b, h, i, s: (h, 0, 0)),
            pl.BlockSpec((hp, 1, c), lambda b, h, i, s: (h, 0, 0)),
        ],
        out_specs=pl.BlockSpec((c, hp * RET_DV), lambda b, h, i, s: (b * nc + i, h)),
        scratch_shapes=[pltpu.VMEM((hp, RET_DK, RET_DV), F32)],
    )
    return pl.pallas_call(
        functools.partial(_ret_kernel, hp=hp),
        grid_spec=grid_spec,
        out_shape=jax.ShapeDtypeStruct((m, heads * RET_DV), BF16),
        compiler_params=_cparams(("parallel", "parallel", "arbitrary"), V7X_VMEM_LIMIT),
        name="retention_core",
    )(dchunk, qk, qk, vg, vg, dintra, dq, dk)


def _retention_layer(x2, norm_g, w_in, w_o, layer, batch, seq):
    m, d = x2.shape
    heads = d // RET_DK
    hq = heads * RET_DK
    h = _rmsnorm(x2, norm_g)
    pos = jnp.arange(seq, dtype=F32)
    half = RET_DK // 2
    inv = 1.0 / (ROPE_BASE ** (jnp.arange(half, dtype=F32) / half))
    ang = pos[:, None] * inv[None, :]
    cos, sin = jnp.cos(ang), jnp.sin(ang)
    tm = _tile(seq, 1024)
    tn = _tile(hq, 512)
    nrow = seq // tm
    rot = functools.partial(_ep_rotary, k_first_tile=hq // tn, k_scale=RET_DK ** -0.5)
    qk = _matmul(h, w_in, layer, col0=0, ncols=2 * hq, tm=tm, tn=tn, out_dtype=BF16, epilogue=rot,
                 extra=((cos, (tm, half), lambda j, i: (i % nrow, 0)),
                        (sin, (tm, half), lambda j, i: (i % nrow, 0))),
                 name="ret_proj_qk")
    vg = _matmul(h, w_in, layer, col0=2 * hq, ncols=2 * heads * RET_DV, tm=tm, tn=tn, out_dtype=BF16,
                 epilogue=_ep_store, name="ret_proj_vg")
    y = _retention_core(qk, vg, batch, seq, heads)
    tm_o = _tile(seq, 512)
    tn_o = _tile(d, 512)
    return _matmul(y, _cast_bf16(w_o, layer), 0, col0=0, ncols=d, tm=tm_o, tn=tn_o, out_dtype=F32,
                   epilogue=_ep_residual, extra=((x2, (tm_o, tn_o), lambda j, i: (i, j)),),
                   name="ret_out_proj")


def _pack_bf16_pairs(lo, hi):
    lo_bits = pltpu.bitcast(lo.astype(BF16).astype(F32), jnp.uint32)
    hi_bits = pltpu.bitcast(hi.astype(BF16).astype(F32), jnp.uint32)
    return hi_bits | lax.shift_right_logical(lo_bits, jnp.uint32(16))


def _unpack_bf16_pairs(words):
    lo = pltpu.bitcast(lax.shift_left(words, jnp.uint32(16)), F32)
    hi = pltpu.bitcast(words & jnp.uint32(0xFFFF0000), F32)
    return lo, hi


def _route_kernel(x_ref, g_ref, whi_ref, wlo_ref, h_ref, r_ref):
    x = x_ref[...]
    hn = x * lax.rsqrt(jnp.mean(x * x, axis=-1, keepdims=True) + RMS_EPS) * g_ref[...]
    half = hn.shape[1] // 2
    h_ref[...] = _pack_bf16_pairs(hn[:, :half], hn[:, half:])
    hn_hi = hn.astype(BF16)
    hn_lo = (hn - hn_hi.astype(F32)).astype(BF16)
    logits = (jnp.dot(hn_hi, whi_ref[...], preferred_element_type=F32)
              + jnp.dot(hn_hi, wlo_ref[...], preferred_element_type=F32)
              + jnp.dot(hn_lo, whi_ref[...], preferred_element_type=F32))
    lane = lax.broadcasted_iota(jnp.int32, logits.shape, 1)
    is_g = lane < MOE_GROUPS
    lg = jnp.where(is_g, logits, -jnp.inf)
    mg = jnp.max(lg, axis=-1, keepdims=True)
    grp = jnp.min(jnp.where(lg == mg, lane, LANES), axis=-1, keepdims=True)
    p_grp = 1.0 / jnp.sum(jnp.where(is_g, jnp.exp(lg - mg), 0.0), axis=-1, keepdims=True)
    e_lane = lane - MOE_GROUPS
    is_e = (e_lane >= 0) & (e_lane < MOE_EXPERTS) & ((e_lane >> 3) == grp)
    le = jnp.where(is_e, logits, -jnp.inf)
    m1 = jnp.max(le, axis=-1, keepdims=True)
    i1 = jnp.min(jnp.where(le == m1, lane, LANES), axis=-1, keepdims=True)
    le2 = jnp.where(lane == i1, -jnp.inf, le)
    m2 = jnp.max(le2, axis=-1, keepdims=True)
    i2 = jnp.min(jnp.where(le2 == m2, lane, LANES), axis=-1, keepdims=True)
    t = jnp.exp(m2 - m1)
    w1 = p_grp / (1.0 + t)
    w2 = p_grp * t / (1.0 + t)
    e1 = (i1 - MOE_GROUPS).astype(F32)
    e2 = (i2 - MOE_GROUPS).astype(F32)
    r_ref[...] = jnp.where(lane == 0, e1, jnp.where(lane == 1, e2,
                           jnp.where(lane == 2, w1, jnp.where(lane == 3, w2, 0.0))))


def _route(x2, norm_g, w_rg, w_re):
    m, d = x2.shape
    tm = _tile(m, 256)
    wr = jnp.concatenate(
        [w_rg, w_re, jnp.zeros((d, LANES - MOE_GROUPS - MOE_EXPERTS), F32)], axis=1)
    wr_hi = wr.astype(BF16)
    wr_lo = (wr - wr_hi.astype(F32)).astype(BF16)
    return pl.pallas_call(
        _route_kernel,
        grid=(m // tm,),
        in_specs=[pl.BlockSpec((tm, d), lambda i: (i, 0)),
                  pl.BlockSpec((1, d), lambda i: (0, 0)),
                  pl.BlockSpec((d, LANES), lambda i: (0, 0)),
                  pl.BlockSpec((d, LANES), lambda i: (0, 0))],
        out_specs=[pl.BlockSpec((tm, d // 2), lambda i: (i, 0)),
                   pl.BlockSpec((tm, LANES), lambda i: (i, 0))],
        out_shape=[jax.ShapeDtypeStruct((m, d // 2), jnp.uint32),
                   jax.ShapeDtypeStruct((m, LANES), F32)],
        compiler_params=_cparams(("parallel",), V7X_VMEM_LIMIT),
        name="moe_route",
    )(x2, norm_g.reshape(1, d), wr_hi, wr_lo)


def _ffn_kernel(blk_e_ref, nvalid_ref, nused_ref,
                src_cur_ref, src_nxt_ref, dst_ref, roww_ref, wg_ref, wu_ref, wd_ref, h_hbm,
                pairs_hbm,
                wgb_ref, wub_ref, wdb_ref, xbuf_ref, ybuf_ref, gsem, ssem, *, tb):
    i = pl.program_id(0)
    n_used = nused_ref[0]
    slot = i % 2

    def gather_copy(src_ref, r, s):
        return pltpu.make_async_copy(h_hbm.at[pl.ds(src_ref[0, 0, r], 1)],
                                     xbuf_ref.at[s, pl.ds(r, 1)], gsem.at[s])

    def scatter_copy(d_row, r, s):
        return pltpu.make_async_copy(ybuf_ref.at[s, pl.ds(r, 1)],
                                     pairs_hbm.at[pl.ds(d_row, 1)], ssem.at[s])

    def start_gather(src_ref, s):
        for r in range(tb):
            gather_copy(src_ref, r, s).start()

    def wait_gather(s):
        pltpu.make_async_copy(h_hbm.at[pl.ds(0, tb)], xbuf_ref.at[s], gsem.at[s]).wait()

    def start_scatter(s, n):
        @pl.when(n == tb)
        def _():
            for r in range(tb):
                scatter_copy(dst_ref[0, 0, r], r, s).start()

        @pl.when(n != tb)
        def _():
            def body(r, carry):
                scatter_copy(dst_ref[0, 0, r], r, s).start()
                return carry
            lax.fori_loop(0, n, body, 0)

    def wait_scatter(s, n):
        @pl.when(n == tb)
        def _():
            pltpu.make_async_copy(ybuf_ref.at[s], pairs_hbm.at[pl.ds(0, tb)], ssem.at[s]).wait()

        @pl.when(n != tb)
        def _():
            def body(r, carry):
                scatter_copy(0, r, s).wait()
                return carry
            lax.fori_loop(0, n, body, 0)

    @pl.when(i == 0)
    def _():
        start_gather(src_cur_ref, 0)

    @pl.when(i < n_used)
    def _():
        @pl.when(i + 1 < n_used)
        def _():
            start_gather(src_nxt_ref, 1 - slot)

        first = jnp.logical_or(i == 0, blk_e_ref[i] != blk_e_ref[jnp.maximum(i - 1, 0)])

        @pl.when(first)
        def _():
            wgb_ref[...] = wg_ref[...].astype(BF16)
            wub_ref[...] = wu_ref[...].astype(BF16)
            wdb_ref[...] = wd_ref[...].astype(BF16)

        wait_gather(slot)
        x_lo, x_hi = _unpack_bf16_pairs(xbuf_ref[slot])
        xb = jnp.concatenate([x_lo, x_hi], axis=1).astype(BF16)
        hid = (jax.nn.silu(jnp.dot(xb, wgb_ref[...], preferred_element_type=F32))
               * jnp.dot(xb, wub_ref[...], preferred_element_type=F32))
        y = jnp.dot(hid.astype(BF16), wdb_ref[...], preferred_element_type=F32)

        @pl.when(i >= 2)
        def _():
            wait_scatter(slot, nvalid_ref[jnp.maximum(i - 2, 0)])

        yw = y * roww_ref[:, 0:1]
        half = yw.shape[1] // 2
        ybuf_ref[slot] = _pack_bf16_pairs(yw[:, :half], yw[:, half:])
        nv = nvalid_ref[i]
        start_scatter(slot, nv)

        @pl.when(i == n_used - 1)
        def _():
            @pl.when(i >= 1)
            def _():
                wait_scatter(1 - slot, nvalid_ref[jnp.maximum(i - 1, 0)])
            wait_scatter(slot, nv)


def _moe_ffn(h, route, w_gate, w_up, w_down, layer, tb):
    n = h.shape[0]
    _, e, d, f = w_gate.shape
    k = MOE_TOPK
    mcopies = n * k
    flat_e = route[:, 0:2].astype(jnp.int32).reshape(mcopies)
    flat_w = route[:, 2:4].reshape(mcopies)
    order = jnp.argsort(flat_e).astype(jnp.int32)
    counts = jnp.sum((flat_e[:, None] == jnp.arange(e)[None, :]).astype(jnp.int32), axis=0)
    padded = (counts + tb - 1) // tb * tb
    pad_end = jnp.cumsum(padded)
    pad_start = pad_end - padded
    start = jnp.cumsum(counts) - counts
    nb = (mcopies + e * (tb - 1) + tb - 1) // tb
    p = nb * tb
    blk_start = jnp.arange(nb) * tb
    blk_e = jnp.minimum(jnp.sum((blk_start[:, None] >= pad_end[None, :]).astype(jnp.int32), axis=1),
                        e - 1).astype(jnp.int32)
    blk_off = blk_start - pad_start[blk_e]
    nvalid = jnp.clip(counts[blk_e] - blk_off, 0, tb).astype(jnp.int32)
    n_used = (pad_end[-1] // tb).astype(jnp.int32).reshape(1)
    rows = jnp.arange(tb)[None, :]
    valid = rows < nvalid[:, None]
    sorted_idx = jnp.clip((start[blk_e] + blk_off)[:, None] + rows, 0, mcopies - 1)
    copy_id = order[sorted_idx]
    tok = copy_id // k
    src3 = jnp.where(valid, tok, 0).reshape(nb, 1, tb)
    dst3 = jnp.where(valid, (copy_id % k) * n + tok, 0).reshape(nb, 1, tb)
    row_w = jnp.where(valid, flat_w[copy_id], 0.0).reshape(p)
    roww = jnp.broadcast_to(row_w[:, None], (p, LANES))

    smem = functools.partial(pl.BlockSpec, memory_space=pltpu.SMEM)
    grid_spec = pltpu.PrefetchScalarGridSpec(
        num_scalar_prefetch=3,
        grid=(nb,),
        in_specs=[
            smem((1, 1, tb), lambda i, be, nv, nu: (i, 0, 0)),
            smem((1, 1, tb), lambda i, be, nv, nu: (jnp.minimum(i + 1, nb - 1), 0, 0)),
            smem((1, 1, tb), lambda i, be, nv, nu: (i, 0, 0)),
            pl.BlockSpec((tb, LANES), lambda i, be, nv, nu: (i, 0)),
            pl.BlockSpec((None, None, d, f), lambda i, be, nv, nu: (layer, be[i], 0, 0)),
            pl.BlockSpec((None, None, d, f), lambda i, be, nv, nu: (layer, be[i], 0, 0)),
            pl.BlockSpec((None, None, f, d), lambda i, be, nv, nu: (layer, be[i], 0, 0)),
            pl.BlockSpec(memory_space=pl.ANY),
        ],
        out_specs=pl.BlockSpec(memory_space=pl.ANY),
        scratch_shapes=[
            pltpu.VMEM((d, f), BF16), pltpu.VMEM((d, f), BF16), pltpu.VMEM((f, d), BF16),
            pltpu.VMEM((2, tb, d // 2), jnp.uint32), pltpu.VMEM((2, tb, d // 2), jnp.uint32),
            pltpu.SemaphoreType.DMA((2,)), pltpu.SemaphoreType.DMA((2,)),
        ],
    )
    return pl.pallas_call(
        functools.partial(_ffn_kernel, tb=tb),
        grid_spec=grid_spec,
        out_shape=jax.ShapeDtypeStruct((k * n, d // 2), jnp.uint32),
        compiler_params=_cparams(("arbitrary",), V7X_VMEM_LIMIT),
        name="moe_ffn",
    )(blk_e, nvalid, n_used, src3, src3, dst3, roww, w_gate, w_up, w_down, h)


def _combine_kernel(x_ref, p0_ref, p1_ref, *rest, n_norm, want_x):
    g_refs = rest[:n_norm]
    outs = rest[n_norm:]
    lo0, hi0 = _unpack_bf16_pairs(p0_ref[...])
    lo1, hi1 = _unpack_bf16_pairs(p1_ref[...])
    x = x_ref[...] + jnp.concatenate([lo0 + lo1, hi0 + hi1], axis=1)
    oi = 0
    if want_x:
        outs[0][...] = x
        oi = 1
    r = lax.rsqrt(jnp.mean(x * x, axis=-1, keepdims=True) + RMS_EPS)
    y = x * r
    for g_ref, o_ref in zip(g_refs, outs[oi:]):
        o_ref[...] = (y * g_ref[...]).astype(o_ref.dtype)


def _combine(x2, pairs, gains, out_dtypes, want_x):
    m, d = x2.shape
    tm = _tile(m, 256)
    nrow = m // tm
    in_specs = [pl.BlockSpec((tm, d), lambda i: (i, 0)),
                pl.BlockSpec((tm, d // 2), lambda i: (i, 0)),
                pl.BlockSpec((tm, d // 2), lambda i: (i + nrow, 0))]
    in_specs += [pl.BlockSpec((1, d), lambda i: (0, 0)) for _ in gains]
    out_shape = ([jax.ShapeDtypeStruct((m, d), F32)] if want_x else []) + [
        jax.ShapeDtypeStruct((m, d), dt) for dt in out_dtypes]
    out_specs = [pl.BlockSpec((tm, d), lambda i: (i, 0)) for _ in out_shape]
    return pl.pallas_call(
        functools.partial(_combine_kernel, n_norm=len(gains), want_x=want_x),
        grid=(nrow,),
        in_specs=in_specs,
        out_specs=out_specs,
        out_shape=out_shape,
        compiler_params=_cparams(("parallel",), V7X_VMEM_LIMIT),
        name="moe_combine",
    )(x2, pairs, pairs, *[g.reshape(1, d) for g in gains])


def _t5_bucket(dist):
    n = jnp.maximum(dist, 0)
    exact = REL_BUCKETS // 2
    nf = jnp.maximum(n, 1).astype(F32)
    large = exact + (jnp.log(nf / exact) / math.log(REL_MAX_DIST / exact)
                     * (REL_BUCKETS - exact)).astype(jnp.int32)
    return jnp.where(n < exact, n, jnp.minimum(large, REL_BUCKETS - 1))


def _bias_lookup(table, bucket):
    onehot = jax.nn.one_hot(bucket, REL_BUCKETS, dtype=F32)
    return jnp.einsum('...b,bh->...h', onehot, table, precision=lax.Precision.HIGHEST)


def _cmp_kernel(rk_ref, rv_ref, pe_ref, w1_ref, w2_ref, o_ref):
    half = w1_ref.shape[1] // 2
    for s, r_ref in enumerate((rk_ref, rv_ref)):
        r = r_ref[0, 0].astype(F32)
        ra = (r + pe_ref[s, :, :half]).astype(BF16)
        rb = (r + pe_ref[s, :, half:]).astype(BF16)
        ha = jnp.dot(ra, w1_ref[s, :half, :], preferred_element_type=F32)
        hb = jnp.dot(rb, w1_ref[s, half:, :], preferred_element_type=F32)
        hb_next = jnp.concatenate([hb[1:], hb[:1]], axis=0)
        hid = jax.nn.gelu(ha + hb_next)
        res = jnp.dot(hid.astype(BF16), w2_ref[s], preferred_element_type=F32)
        o_ref[0, 0, s] = (res if s == 0 else res.T).astype(o_ref.dtype)


def _compress(kv, batch, seq, pes, w1s, w2s):
    g, dk = NSA_GROUPS, NSA_DK
    nr = seq // CMP_STRIDE
    assert nr == dk
    ct = kv[:, :2 * g * dk].reshape(batch, nr, CMP_STRIDE, 2, g, dk)
    r = ct.transpose(3, 0, 4, 1, 2, 5).reshape(2, batch, g, nr, CMP_STRIDE * dk)
    pe = jnp.stack(pes).reshape(2, 1, CMP_LEN * dk)
    w1 = jnp.stack(w1s).reshape(2, CMP_LEN * dk, dk).astype(BF16)
    w2 = jnp.stack(w2s).astype(BF16)
    rspec = pl.BlockSpec((1, 1, nr, CMP_STRIDE * dk), lambda b, gi: (b, gi, 0, 0))
    return pl.pallas_call(
        _cmp_kernel,
        grid=(batch, g),
        in_specs=[rspec, rspec,
                  pl.BlockSpec((2, 1, CMP_LEN * dk), lambda b, gi: (0, 0, 0)),
                  pl.BlockSpec((2, CMP_LEN * dk, dk), lambda b, gi: (0, 0, 0)),
                  pl.BlockSpec((2, dk, dk), lambda b, gi: (0, 0, 0))],
        out_specs=pl.BlockSpec((1, 1, 2, nr, dk), lambda b, gi: (b, gi, 0, 0, 0)),
        out_shape=jax.ShapeDtypeStruct((batch, g, 2, nr, dk), BF16),
        compiler_params=_cparams(("parallel", "parallel"), V7X_VMEM_LIMIT),
        name="nsa_compress",
    )(r[0], r[1], pe, w1, w2)


def _nsa_kernel(q_ref, gate_ref, kvc_ref, ks_ref, kw_ref, vst_ref, vwt_ref, biasc_ref, bias_ref,
                ovt_ref, ext_ref, o_ref,
                selpen_ref, m_ref, l_ref, acc_ref, *, hg, tq, n_cmp, n_sblk):
    qb = pl.program_id(2)
    dk = NSA_DK
    nt_dims = (((1,), (1,)), ((), ()))
    q8 = jnp.concatenate([q_ref[:, h * dk:(h + 1) * dk] for h in range(hg)], axis=0)
    kj = lax.broadcasted_iota(jnp.int32, (LANES, tq), 0)
    qi = lax.broadcasted_iota(jnp.int32, (LANES, tq), 1)
    t_abs = qb * tq + qi

    def tile_h(a):
        return jnp.concatenate([a] * hg, axis=1)

    s = lax.dot_general(kvc_ref[0], q8, nt_dims, preferred_element_type=F32) + biasc_ref[...]
    valid_c = (kj * CMP_STRIDE + (CMP_LEN - 1) <= t_abs) & (kj < n_cmp)
    s = s + tile_h(jnp.where(valid_c, 0.0, NEG))
    mx = jnp.maximum(jnp.max(s, axis=0, keepdims=True), 0.5 * NEG)
    p = jnp.exp2(s - mx)
    p = p * (1.0 / jnp.maximum(jnp.sum(p, axis=0, keepdims=True), 1e-30))
    o_c = jnp.dot(kvc_ref[1], p.astype(BF16), preferred_element_type=F32)
    psum = p[:, 0:tq]
    for h in range(1, hg):
        psum = psum + p[:, h * tq:(h + 1) * tq]
    p_hi = psum.astype(BF16)
    p_lo = (psum - p_hi.astype(F32)).astype(BF16)
    imp = (jnp.dot(ovt_ref[...], p_hi, preferred_element_type=F32)
           + jnp.dot(ovt_ref[...], p_lo, preferred_element_type=F32))

    imp = imp[:n_sblk]
    kb_i = lax.broadcasted_iota(jnp.int32, (n_sblk, tq), 0)
    t_b = qb * tq + lax.broadcasted_iota(jnp.int32, (n_sblk, tq), 1)
    cur = lax.shift_right_logical(t_b, int(math.log2(SEL_BLOCK)))
    valid_b = kb_i * SEL_BLOCK <= t_b
    forced = (kb_i == 0) | (kb_i == cur) | (kb_i == cur - 1)
    score = jnp.where(valid_b, jnp.where(forced, FORCED_SCORE, imp), INVALID_SCORE)
    rank = jnp.zeros((n_sblk, tq), F32)
    for j in range(n_sblk):
        cj = score[j:j + 1, :]
        ahead = (cj > score) | ((cj == score) & (kb_i > j))
        rank = rank + jnp.where(ahead, 1.0, 0.0)
    sel = jnp.where(rank < min(SEL_TOPK, n_sblk), 1.0, 0.0)
    if n_sblk < LANES:
        sel = jnp.concatenate([sel, jnp.zeros((LANES - n_sblk, tq), F32)], axis=0)
    selmask = jnp.dot(ext_ref[...], sel.astype(BF16), preferred_element_type=F32)
    selpen_ref[...] = ((selmask - 1.0) * (-NEG)).reshape(selpen_ref.shape)

    def attend(kt, vtt, addend, br):
        sc = lax.dot_general(kt, q8, nt_dims, preferred_element_type=F32) + addend
        m_old = m_ref[br]
        m_new = jnp.maximum(m_old, jnp.max(sc, axis=0, keepdims=True))
        alpha = jnp.exp2(m_old - m_new)
        pt = jnp.exp2(sc - m_new)
        l_ref[br] = alpha * l_ref[br] + jnp.sum(pt, axis=0, keepdims=True)
        acc_ref[br] = alpha * acc_ref[br] + jnp.dot(vtt, pt.astype(BF16), preferred_element_type=F32)
        m_ref[br] = m_new

    m_ref[...] = jnp.full(m_ref.shape, 0.5 * NEG, F32)
    l_ref[...] = jnp.zeros(l_ref.shape, F32)
    acc_ref[...] = jnp.zeros(acc_ref.shape, F32)
    n_far = (bias_ref.shape[0] - 1)
    win_tiles = WINDOW // LANES
    assert win_tiles >= n_far
    lo = jnp.maximum(qb - win_tiles, 0)

    def near_body(kb, carry):
        off = pl.multiple_of(kb * LANES, LANES)
        dist = (qb - kb) * tq + qi - kj
        bias = bias_ref[jnp.minimum(qb - kb, n_far)]
        pen_s = selpen_ref[kb] + jnp.where(dist >= 0, 0.0, NEG)
        pen_w = jnp.where((dist >= 0) & (dist < WINDOW), 0.0, NEG)
        attend(ks_ref[pl.ds(off, LANES), :], vst_ref[kb], bias + tile_h(pen_s), 0)
        attend(kw_ref[pl.ds(off, LANES), :], vwt_ref[kb], bias + tile_h(pen_w), 1)
        return carry
    lax.fori_loop(lo, qb + 1, near_body, 0)

    def far_pair(pi, carry):
        kb = 2 * pi
        off = pl.multiple_of(kb * LANES, LANES)
        pen = jnp.concatenate([selpen_ref[kb], selpen_ref[kb + 1]], axis=0)
        vtt = jnp.concatenate([vst_ref[kb], vst_ref[kb + 1]], axis=1)
        attend(ks_ref[pl.ds(off, 2 * LANES), :], vtt, tile_h(pen), 2)
        return carry
    lax.fori_loop(0, lo // 2, far_pair, 0)

    @pl.when(lo % 2 == 1)
    def _():
        kb = lo - 1
        off = pl.multiple_of(kb * LANES, LANES)
        attend(ks_ref[pl.ds(off, LANES), :], vst_ref[kb], tile_h(selpen_ref[kb]), 2)

    c_far = bias_ref[n_far, 0:1, :]
    m_s = jnp.maximum(m_ref[0], m_ref[2] + c_far)
    a0 = jnp.exp2(m_ref[0] - m_s)
    a2 = jnp.exp2(m_ref[2] + c_far - m_s)
    l_s = l_ref[0] * a0 + l_ref[2] * a2
    o_s = (acc_ref[0] * a0 + acc_ref[2] * a2) * (1.0 / jnp.maximum(l_s, 1e-30))
    o_w = acc_ref[1] * (1.0 / jnp.maximum(l_ref[1], 1e-30))
    gt = gate_ref[...].T
    for h in range(hg):
        cols = slice(h * tq, (h + 1) * tq)
        out_t = (gt[3 * h:3 * h + 1] * o_c[:, cols] + gt[3 * h + 1:3 * h + 2] * o_s[:, cols]
                 + gt[3 * h + 2:3 * h + 3] * o_w[:, cols])
        o_ref[:, h * dk:(h + 1) * dk] = out_t.T.astype(o_ref.dtype)


def _nsa_attention(q, gates, kv, kvc, rel_bias, batch, seq, heads):
    g, dk = NSA_GROUPS, NSA_DK
    hg = heads // g
    tq = LANES
    nqb = seq // tq
    n_cmp = (seq - CMP_LEN) // CMP_STRIDE + 1
    n_sblk = seq // SEL_BLOCK
    nr = seq // CMP_STRIDE
    assert nr == LANES and n_sblk <= LANES and n_sblk % 8 == 0 and tq == LANES
    nt = seq // LANES
    rows = hg * tq
    table = rel_bias.astype(F32) * LOG2E
    tt = jnp.arange(seq)
    cmp_end = jnp.arange(LANES) * CMP_STRIDE + CMP_LEN - 1
    bias_c = _bias_lookup(table, _t5_bucket(tt[:, None] - cmp_end[None, :]))
    bias_c = bias_c.reshape(nqb, tq, LANES, g, hg).transpose(3, 0, 2, 4, 1).reshape(g, nqb, LANES, rows)
    ii = jnp.arange(tq)
    n_far = -(-(REL_MAX_DIST + tq - 1) // tq)
    dist = jnp.arange(n_far + 1)[:, None, None] * tq + ii[None, None, :] - ii[None, :, None]
    bias_t = _bias_lookup(table, _t5_bucket(dist))
    bias_t = bias_t.reshape(n_far + 1, tq, tq, g, hg).transpose(0, 3, 1, 4, 2).reshape(n_far + 1, g, tq, rows)
    cs = np.arange(LANES) * CMP_STRIDE
    ss = np.arange(LANES) * SEL_BLOCK
    ov = np.clip(np.minimum(cs[:, None] + CMP_LEN, ss[None, :] + SEL_BLOCK)
                 - np.maximum(cs[:, None], ss[None, :]), 0, None) / CMP_LEN
    ov[n_cmp:, :] = 0.0
    ov[:, n_sblk:] = 0.0
    overlap_t = jnp.asarray(ov.T, BF16)
    ex = ((np.arange(seq)[:, None] // SEL_BLOCK) == np.arange(LANES)[None, :]).astype(np.float32)
    expand_t = jnp.asarray(ex, BF16)
    kv6 = kv.reshape(batch, nt, LANES, 6, g, dk)
    vt = jnp.stack([kv6[:, :, :, 3], kv6[:, :, :, 5]]).transpose(0, 1, 4, 2, 5, 3)

    kspec = lambda blk: pl.BlockSpec((seq, dk), lambda b, gi, qb: (b, blk * g + gi))
    vspec = lambda br: pl.BlockSpec((None, None, None, nt, dk, LANES),
                                    lambda b, gi, qb: (br, b, gi, 0, 0, 0))
    return pl.pallas_call(
        functools.partial(_nsa_kernel, hg=hg, tq=tq, n_cmp=n_cmp, n_sblk=n_sblk),
        grid=(batch, g, nqb),
        in_specs=[
            pl.BlockSpec((tq, hg * dk), lambda b, gi, qb: (b * nqb + qb, gi)),
            pl.BlockSpec((tq, LANES), lambda b, gi, qb: (b * nqb + qb, gi)),
            pl.BlockSpec((None, None, 2, nr, dk), lambda b, gi, qb: (b, gi, 0, 0, 0)),
            kspec(2), kspec(4), vspec(0), vspec(1),
            pl.BlockSpec((None, None, LANES, rows), lambda b, gi, qb: (gi, qb, 0, 0)),
            pl.BlockSpec((n_far + 1, None, tq, rows), lambda b, gi, qb: (0, gi, 0, 0)),
            pl.BlockSpec((LANES, LANES), lambda b, gi, qb: (0, 0)),
            pl.BlockSpec((seq, LANES), lambda b, gi, qb: (0, 0)),
        ],
        out_specs=pl.BlockSpec((tq, hg * dk), lambda b, gi, qb: (b * nqb + qb, gi)),
        out_shape=jax.ShapeDtypeStruct((batch * seq, heads * dk), BF16),
        scratch_shapes=[pltpu.VMEM((nt, LANES, tq), F32),
                        pltpu.VMEM((3, 1, rows), F32), pltpu.VMEM((3, 1, rows), F32),
                        pltpu.VMEM((3, dk, rows), F32)],
        compiler_params=_cparams(("parallel", "parallel", "arbitrary"), V7X_VMEM_LIMIT),
        name="nsa_attention",
    )(q, gates, kvc, kv, kv, vt, vt, bias_c, bias_t, overlap_t, expand_t)


def _nsa_layer(x2, h_kv, h_q, w_kv, cmp_params, rel_bias, w_q, w_o, layer, batch, seq):
    m, d = x2.shape
    g, dk = NSA_GROUPS, NSA_DK
    heads = d // dk
    hg = heads // g
    tm = _tile(seq, 1024)
    kv = _matmul(h_kv, w_kv[None], 0, col0=0, ncols=6 * g * dk, tm=tm, tn=_tile(6 * g * dk, 512),
                 out_dtype=BF16, epilogue=_ep_store, name="nsa_proj_kv")
    (k_pe, k_w1, k_w2, v_pe, v_w1, v_w2) = cmp_params
    kvc = _compress(kv, batch, seq, (k_pe, v_pe), (k_w1, v_w1), (k_w2, v_w2))
    q = _matmul(h_q, w_q, layer, col0=0, ncols=heads * dk, tm=tm, tn=_tile(heads * dk, 512),
                out_dtype=BF16, epilogue=functools.partial(_ep_scale, scale=dk ** -0.5 * LOG2E),
                name="nsa_proj_q")
    wg = w_q[layer, :, heads * dk:].reshape(d, g, hg * 3)
    wg = jnp.pad(wg, ((0, 0), (0, 0), (0, LANES - hg * 3))).reshape(1, d, g * LANES)
    gates = _matmul(h_q, wg, 0, col0=0, ncols=g * LANES, tm=tm, tn=g * LANES, out_dtype=F32,
                    epilogue=_ep_sigmoid, name="nsa_proj_gate")
    o = _nsa_attention(q, gates, kv, kvc, rel_bias, batch, seq, heads)
    tn_o = _tile(d, 512)
    return _matmul(o, w_o, layer, col0=0, ncols=d, tm=tm, tn=tn_o, out_dtype=F32, epilogue=_ep_residual,
                   extra=((x2, (tm, tn_o), lambda j, i: (i, j)),), name="nsa_out_proj")


def kernel(x, ret_norm_g, ret_w_in, ret_w_o, kv_norm_g, nsa_w_kv, cmp_k_pe, cmp_k_w1, cmp_k_w2, cmp_v_pe, cmp_v_w1, cmp_v_w2, rel_bias, nsa_norm_g, nsa_w_q, nsa_w_o, moe_norm_g, moe_w_router_group, moe_w_router_expert, moe_w_gate, moe_w_up, moe_w_down, final_norm_g):
    batch, seq, d = x.shape
    depth = moe_norm_g.shape[0]
    n_a = ret_norm_g.shape[0]
    x2 = x.reshape(batch * seq, d)
    cmp_params = (cmp_k_pe, cmp_k_w1, cmp_k_w2, cmp_v_pe, cmp_v_w1, cmp_v_w2)
    h_kv = h_q = None
    out = None
    for layer in range(depth):
        if layer < n_a:
            x2 = _retention_layer(x2, ret_norm_g[layer], ret_w_in, ret_w_o, layer, batch, seq)
        else:
            j = layer - n_a
            if h_q is None:
                h_kv = _rmsnorm(x2, kv_norm_g)
                h_q = _rmsnorm(x2, nsa_norm_g[j])
            x2 = _nsa_layer(x2, h_kv, h_q, nsa_w_kv, cmp_params, rel_bias, nsa_w_q, nsa_w_o, j,
                            batch, seq)
        hm, route = _route(x2, moe_norm_g[layer], moe_w_router_group[layer], moe_w_router_expert[layer])
        pairs = _moe_ffn(hm, route, moe_w_gate, moe_w_up, moe_w_down, layer, tb=LANES)
        if layer == depth - 1:
            (out,) = _combine(x2, pairs, [final_norm_g], [F32], want_x=False)
        elif layer + 1 == n_a:
            j = layer + 1 - n_a
            x2, h_kv, h_q = _combine(x2, pairs, [kv_norm_g, nsa_norm_g[j]], [BF16, BF16], want_x=True)
        elif layer + 1 < n_a:
            (x2,) = _combine(x2, pairs, [], [], want_x=True)
        else:
            j = layer + 1 - n_a
            x2, h_q = _combine(x2, pairs, [nsa_norm_g[j]], [BF16], want_x=True)
    return out.reshape(batch, seq, d)
```

```python
import functools
import math

import numpy as np
import jax
import jax.numpy as jnp
from jax import lax
from jax.experimental import pallas as pl
from jax.experimental.pallas import tpu as pltpu

RMS_EPS = 1e-6
RET_DK = 256
RET_DV = 2 * RET_DK
ROPE_BASE = 10000.0
NSA_DK = 128
NSA_GROUPS = 4
CMP_LEN = 32
CMP_STRIDE = 16
SEL_BLOCK = 64
SEL_TOPK = 16
WINDOW = 512
FORCED_SCORE = 1e4
INVALID_SCORE = -1e9
REL_BUCKETS = 32
REL_MAX_DIST = 128
MOE_GROUPS = 8
MOE_EPG = 8
MOE_EXPERTS = MOE_GROUPS * MOE_EPG
MOE_TOPK = 2

LANES = 128
V7X_VMEM_LIMIT = 56 * 1024 * 1024
NEG = -1e30
LOG2E = math.log2(math.e)

BF16 = jnp.bfloat16
F32 = jnp.float32


def _tile(dim, pref):
    t = min(dim, pref)
    while dim % t:
        t //= 2
    return t


def _cparams(sem, vmem=None):
    return pltpu.CompilerParams(dimension_semantics=sem, vmem_limit_bytes=vmem)


def _norm_kernel(x_ref, g_ref, o_ref):
    x = x_ref[...]
    y = x * lax.rsqrt(jnp.mean(x * x, axis=-1, keepdims=True) + RMS_EPS)
    o_ref[...] = (y * g_ref[...]).astype(o_ref.dtype)


def _rmsnorm(x2, g, out_dtype=BF16):
    m, d = x2.shape
    tm = _tile(m, 256)
    return pl.pallas_call(
        _norm_kernel,
        grid=(m // tm,),
        in_specs=[pl.BlockSpec((tm, d), lambda i: (i, 0)),
                  pl.BlockSpec((1, d), lambda i: (0, 0))],
        out_specs=pl.BlockSpec((tm, d), lambda i: (i, 0)),
        out_shape=jax.ShapeDtypeStruct((m, d), out_dtype),
        compiler_params=_cparams(("parallel",), V7X_VMEM_LIMIT),
        name="rmsnorm",
    )(x2, g.reshape(1, d))


MM_CHUNK = 256


def _mm_kernel(a_ref, w_ref, *rest, n_extra, epilogue, cast_w):
    extra = rest[:n_extra]
    o_ref = rest[n_extra]
    if cast_w:
        wbf_ref = rest[n_extra + 1]

        @pl.when(pl.program_id(1) == 0)
        def _():
            wbf_ref[...] = w_ref[...].astype(BF16)
    else:
        wbf_ref = w_ref

    a = a_ref[...]
    tn = o_ref.shape[1]
    chunk = min(tn, MM_CHUNK)
    for c0 in range(0, tn, chunk):
        cols = slice(c0, c0 + chunk)
        acc = jnp.dot(a, wbf_ref[:, cols], preferred_element_type=F32)
        epilogue(acc, o_ref, cols, *extra)


def _matmul(a, w, layer, *, col0, ncols, tm, tn, out_dtype, epilogue, extra=(), name):
    m, k = a.shape
    assert col0 % tn == 0 and ncols % tn == 0 and m % tm == 0
    joff = col0 // tn
    cast_w = w.dtype != BF16
    in_specs = [pl.BlockSpec((tm, k), lambda j, i: (i, 0)),
                pl.BlockSpec((None, k, tn), lambda j, i: (layer, 0, j + joff))]
    args = [a, w]
    for arr, bshape, imap in extra:
        in_specs.append(pl.BlockSpec(bshape, imap))
        args.append(arr)
    return pl.pallas_call(
        functools.partial(_mm_kernel, n_extra=len(extra), epilogue=epilogue, cast_w=cast_w),
        grid=(ncols // tn, m // tm),
        in_specs=in_specs,
        out_specs=pl.BlockSpec((tm, tn), lambda j, i: (i, j)),
        out_shape=jax.ShapeDtypeStruct((m, ncols), out_dtype),
        scratch_shapes=[pltpu.VMEM((k, tn), BF16)] if cast_w else [],
        compiler_params=_cparams(("arbitrary", "arbitrary"), V7X_VMEM_LIMIT),
        name=name,
    )(*args)


def _cast_kernel(w_ref, o_ref):
    o_ref[...] = w_ref[...].astype(o_ref.dtype)


def _cast_bf16(w, layer):
    _, k, n = w.shape
    tk = _tile(k, 512)
    return pl.pallas_call(
        _cast_kernel,
        grid=(k // tk,),
        in_specs=[pl.BlockSpec((None, tk, n), lambda i: (layer, i, 0))],
        out_specs=pl.BlockSpec((None, tk, n), lambda i: (0, i, 0)),
        out_shape=jax.ShapeDtypeStruct((1, k, n), BF16),
        compiler_params=_cparams(("parallel",), V7X_VMEM_LIMIT),
        name="cast_bf16",
    )(w)


def _ep_store(acc, o_ref, cols):
    o_ref[:, cols] = acc.astype(o_ref.dtype)


def _ep_residual(acc, o_ref, cols, r_ref):
    o_ref[:, cols] = (r_ref[:, cols] + acc).astype(o_ref.dtype)


def _ep_scale(acc, o_ref, cols, *, scale):
    o_ref[:, cols] = (acc * scale).astype(o_ref.dtype)


def _ep_sigmoid(acc, o_ref, cols):
    o_ref[:, cols] = jax.nn.sigmoid(acc).astype(o_ref.dtype)


def _ep_rotary(acc, o_ref, cols, cos_ref, sin_ref, *, k_first_tile, k_scale):
    assert acc.shape[1] == RET_DK
    scale = jnp.where(pl.program_id(0) >= k_first_tile, k_scale, 1.0).astype(F32)
    cos = cos_ref[...] * scale
    sin = sin_ref[...] * scale
    half = RET_DK // 2
    x1 = acc[:, :half]
    x2 = acc[:, half:]
    o_ref[:, cols.start:cols.start + half] = (x1 * cos - x2 * sin).astype(o_ref.dtype)
    o_ref[:, cols.start + half:cols.stop] = (x1 * sin + x2 * cos).astype(o_ref.dtype)


def _ret_kernel(dchunk_ref, q_ref, k_ref, v_ref, g_ref, dintra_ref, dq_ref, dk_ref, y_ref, state_ref, *, hp):
    c = pl.program_id(2)

    @pl.when(c == 0)
    def _():
        state_ref[...] = jnp.zeros_like(state_ref)

    for j in range(hp):
        h = pl.program_id(1) * hp + j
        q = q_ref[:, j * RET_DK:(j + 1) * RET_DK]
        k = k_ref[:, j * RET_DK:(j + 1) * RET_DK]
        v = v_ref[:, j * RET_DV:(j + 1) * RET_DV]
        scores = lax.dot_general(q, k, (((1,), (1,)), ((), ())), preferred_element_type=F32)
        scores = scores * dintra_ref[j]
        intra = jnp.dot(scores.astype(BF16), v, preferred_element_type=F32)
        cross = jnp.dot(q, state_ref[j].astype(BF16), preferred_element_type=F32)
        dq = dq_ref[j]
        out = jnp.concatenate(
            [intra[:, n:n + LANES] + cross[:, n:n + LANES] * dq for n in range(0, RET_DV, LANES)], axis=1)
        kt = (k.astype(F32).T * dk_ref[j]).astype(BF16)
        state_ref[j] = dchunk_ref[h] * state_ref[j] + jnp.dot(kt, v, preferred_element_type=F32)
        out = out * lax.rsqrt(jnp.mean(out * out, axis=-1, keepdims=True) + RMS_EPS)
        gate = g_ref[:, j * RET_DV:(j + 1) * RET_DV].astype(F32)
        y_ref[:, j * RET_DV:(j + 1) * RET_DV] = (jax.nn.silu(gate) * out).astype(y_ref.dtype)


def _retention_core(qk, vg, batch, seq, heads):
    m = batch * seq
    c = _tile(seq, 256)
    nc = seq // c
    log_gamma = jnp.log1p(-jnp.exp2(-5.0 - jnp.arange(heads, dtype=F32)))
    idx = jnp.arange(c, dtype=F32)
    diff = idx[:, None] - idx[None, :]
    dintra = jnp.where(diff >= 0, jnp.exp(log_gamma[:, None, None] * jnp.maximum(diff, 0.0)), 0.0)
    dq = jnp.broadcast_to(jnp.exp(log_gamma[:, None] * (idx + 1.0))[:, :, None], (heads, c, LANES))
    dk = jnp.exp(log_gamma[:, None] * (c - 1.0 - idx))[:, None, :]
    dchunk = jnp.exp(log_gamma * c)
    hp = 2 if heads % 2 == 0 else 1
    ng = heads // hp
    grid_spec = pltpu.PrefetchScalarGridSpec(
        num_scalar_prefetch=1,
        grid=(batch, ng, nc),
        in_specs=[
            pl.BlockSpec((c, hp * RET_DK), lambda b, h, i, s: (b * nc + i, h)),
            pl.BlockSpec((c, hp * RET_DK), lambda b, h, i, s: (b * nc + i, ng + h)),
            pl.BlockSpec((c, hp * RET_DV), lambda b, h, i, s: (b * nc + i, h)),
            pl.BlockSpec((c, hp * RET_DV), lambda b, h, i, s: (b * nc + i, ng + h)),
            pl.BlockSpec((hp, c, c), lambda b, h, i, s: (h, 0, 0)),
            pl.BlockSpec((hp, c, LANES), lambda b, h, i, s: (h, 0, 0)),
            pl.BlockSpec((hp, 1, c), lambda b, h, i, s: (h, 0, 0)),
        ],
        out_specs=pl.BlockSpec((c, hp * RET_DV), lambda b, h, i, s: (b * nc + i, h)),
        scratch_shapes=[pltpu.VMEM((hp, RET_DK, RET_DV), F32)],
    )
    return pl.pallas_call(
        functools.partial(_ret_kernel, hp=hp),
        grid_spec=grid_spec,
        out_shape=jax.ShapeDtypeStruct((m, heads * RET_DV), BF16),
        compiler_params=_cparams(("parallel", "parallel", "arbitrary"), V7X_VMEM_LIMIT),
        name="retention_core",
    )(dchunk, qk, qk, vg, vg, dintra, dq, dk)


def _retention_layer(x2, norm_g, w_in, w_o, layer, batch, seq):
    m, d = x2.shape
    heads = d // RET_DK
    hq = heads * RET_DK
    h = _rmsnorm(x2, norm_g)
    pos = jnp.arange(seq, dtype=F32)
    half = RET_DK // 2
    inv = 1.0 / (ROPE_BASE ** (jnp.arange(half, dtype=F32) / half))
    ang = pos[:, None] * inv[None, :]
    cos, sin = jnp.cos(ang), jnp.sin(ang)
    tm = _tile(seq, 1024)
    tn = _tile(hq, 512)
    nrow = seq // tm
    rot = functools.partial(_ep_rotary, k_first_tile=hq // tn, k_scale=RET_DK ** -0.5)
    qk = _matmul(h, w_in, layer, col0=0, ncols=2 * hq, tm=tm, tn=tn, out_dtype=BF16, epilogue=rot,
                 extra=((cos, (tm, half), lambda j, i: (i % nrow, 0)),
                        (sin, (tm, half), lambda j, i: (i % nrow, 0))),
                 name="ret_proj_qk")
    vg = _matmul(h, w_in, layer, col0=2 * hq, ncols=2 * heads * RET_DV, tm=tm, tn=tn, out_dtype=BF16,
                 epilogue=_ep_store, name="ret_proj_vg")
    y = _retention_core(qk, vg, batch, seq, heads)
    tm_o = _tile(seq, 512)
    tn_o = _tile(d, 512)
    return _matmul(y, _cast_bf16(w_o, layer), 0, col0=0, ncols=d, tm=tm_o, tn=tn_o, out_dtype=F32,
                   epilogue=_ep_residual, extra=((x2, (tm_o, tn_o), lambda j, i: (i, j)),),
                   name="ret_out_proj")


def _pack_bf16_pairs(lo, hi):
    lo_bits = pltpu.bitcast(lo.astype(BF16).astype(F32), jnp.uint32)
    hi_bits = pltpu.bitcast(hi.astype(BF16).astype(F32), jnp.uint32)
    return hi_bits | lax.shift_right_logical(lo_bits, jnp.uint32(16))


def _unpack_bf16_pairs(words):
    lo = pltpu.bitcast(lax.shift_left(words, jnp.uint32(16)), F32)
    hi = pltpu.bitcast(words & jnp.uint32(0xFFFF0000), F32)
    return lo, hi


def _route_kernel(x_ref, g_ref, whi_ref, wlo_ref, h_ref, r_ref):
    x = x_ref[...]
    hn = x * lax.rsqrt(jnp.mean(x * x, axis=-1, keepdims=True) + RMS_EPS) * g_ref[...]
    half = hn.shape[1] // 2
    h_ref[...] = _pack_bf16_pairs(hn[:, :half], hn[:, half:])
    hn_hi = hn.astype(BF16)
    hn_lo = (hn - hn_hi.astype(F32)).astype(BF16)
    logits = (jnp.dot(hn_hi, whi_ref[...], preferred_element_type=F32)
              + jnp.dot(hn_hi, wlo_ref[...], preferred_element_type=F32)
              + jnp.dot(hn_lo, whi_ref[...], preferred_element_type=F32))
    lane = lax.broadcasted_iota(jnp.int32, logits.shape, 1)
    is_g = lane < MOE_GROUPS
    lg = jnp.where(is_g, logits, -jnp.inf)
    mg = jnp.max(lg, axis=-1, keepdims=True)
    grp = jnp.min(jnp.where(lg == mg, lane, LANES), axis=-1, keepdims=True)
    p_grp = 1.0 / jnp.sum(jnp.where(is_g, jnp.exp(lg - mg), 0.0), axis=-1, keepdims=True)
    e_lane = lane - MOE_GROUPS
    is_e = (e_lane >= 0) & (e_lane < MOE_EXPERTS) & ((e_lane >> 3) == grp)
    le = jnp.where(is_e, logits, -jnp.inf)
    m1 = jnp.max(le, axis=-1, keepdims=True)
    i1 = jnp.min(jnp.where(le == m1, lane, LANES), axis=-1, keepdims=True)
    le2 = jnp.where(lane == i1, -jnp.inf, le)
    m2 = jnp.max(le2, axis=-1, keepdims=True)
    i2 = jnp.min(jnp.where(le2 == m2, lane, LANES), axis=-1, keepdims=True)
    t = jnp.exp(m2 - m1)
    w1 = p_grp / (1.0 + t)
    w2 = p_grp * t / (1.0 + t)
    e1 = (i1 - MOE_GROUPS).astype(F32)
    e2 = (i2 - MOE_GROUPS).astype(F32)
    r_ref[...] = jnp.where(lane == 0, e1, jnp.where(lane == 1, e2,
                           jnp.where(lane == 2, w1, jnp.where(lane == 3, w2, 0.0))))


def _route(x2, norm_g, w_rg, w_re):
    m, d = x2.shape
    tm = _tile(m, 256)
    wr = jnp.concatenate(
        [w_rg, w_re, jnp.zeros((d, LANES - MOE_GROUPS - MOE_EXPERTS), F32)], axis=1)
    wr_hi = wr.astype(BF16)
    wr_lo = (wr - wr_hi.astype(F32)).astype(BF16)
    return pl.pallas_call(
        _route_kernel,
        grid=(m // tm,),
        in_specs=[pl.BlockSpec((tm, d), lambda i: (i, 0)),
                  pl.BlockSpec((1, d), lambda i: (0, 0)),
                  pl.BlockSpec((d, LANES), lambda i: (0, 0)),
                  pl.BlockSpec((d, LANES), lambda i: (0, 0))],
        out_specs=[pl.BlockSpec((tm, d // 2), lambda i: (i, 0)),
                   pl.BlockSpec((tm, LANES), lambda i: (i, 0))],
        out_shape=[jax.ShapeDtypeStruct((m, d // 2), jnp.uint32),
                   jax.ShapeDtypeStruct((m, LANES), F32)],
        compiler_params=_cparams(("parallel",), V7X_VMEM_LIMIT),
        name="moe_route",
    )(x2, norm_g.reshape(1, d), wr_hi, wr_lo)


def _ffn_kernel(blk_e_ref, nvalid_ref, nused_ref,
                src_cur_ref, src_nxt_ref, dst_ref, roww_ref, wg_ref, wu_ref, wd_ref, h_hbm,
                pairs_hbm,
                wgb_ref, wub_ref, wdb_ref, xbuf_ref, ybuf_ref, gsem, ssem, *, tb):
    i = pl.program_id(0)
    n_used = nused_ref[0]
    slot = i % 2

    def gather_copy(src_ref, r, s):
        return pltpu.make_async_copy(h_hbm.at[pl.ds(src_ref[0, 0, r], 1)],
                                     xbuf_ref.at[s, pl.ds(r, 1)], gsem.at[s])

    def scatter_copy(d_row, r, s):
        return pltpu.make_async_copy(ybuf_ref.at[s, pl.ds(r, 1)],
                                     pairs_hbm.at[pl.ds(d_row, 1)], ssem.at[s])

    def start_gather(src_ref, s):
        for r in range(tb):
            gather_copy(src_ref, r, s).start(priority=1)

    def wait_gather(s):
        pltpu.make_async_copy(h_hbm.at[pl.ds(0, tb)], xbuf_ref.at[s], gsem.at[s]).wait()

    def start_scatter(s, n):
        @pl.when(n == tb)
        def _():
            for r in range(tb):
                scatter_copy(dst_ref[0, 0, r], r, s).start(priority=r % 2)

        @pl.when(n != tb)
        def _():
            def body(r, carry):
                scatter_copy(dst_ref[0, 0, r], r, s).start()
                return carry
            lax.fori_loop(0, n, body, 0)

    def wait_scatter(s, n):
        @pl.when(n == tb)
        def _():
            pltpu.make_async_copy(ybuf_ref.at[s], pairs_hbm.at[pl.ds(0, tb)], ssem.at[s]).wait()

        @pl.when(n != tb)
        def _():
            def body(r, carry):
                scatter_copy(0, r, s).wait()
                return carry
            lax.fori_loop(0, n, body, 0)

    @pl.when(i == 0)
    def _():
        start_gather(src_cur_ref, 0)

    @pl.when(i < n_used)
    def _():
        @pl.when(i + 1 < n_used)
        def _():
            start_gather(src_nxt_ref, 1 - slot)

        first = jnp.logical_or(i == 0, blk_e_ref[i] != blk_e_ref[jnp.maximum(i - 1, 0)])

        @pl.when(first)
        def _():
            wgb_ref[...] = wg_ref[...].astype(BF16)
            wub_ref[...] = wu_ref[...].astype(BF16)
            wdb_ref[...] = wd_ref[...].astype(BF16)

        wait_gather(slot)
        x_lo, x_hi = _unpack_bf16_pairs(xbuf_ref[slot])
        xb = jnp.concatenate([x_lo, x_hi], axis=1).astype(BF16)
        hid = (jax.nn.silu(jnp.dot(xb, wgb_ref[...], preferred_element_type=F32))
               * jnp.dot(xb, wub_ref[...], preferred_element_type=F32))
        y = jnp.dot(hid.astype(BF16), wdb_ref[...], preferred_element_type=F32)

        @pl.when(i >= 2)
        def _():
            wait_scatter(slot, nvalid_ref[jnp.maximum(i - 2, 0)])

        yw = y * roww_ref[:, 0:1]
        half = yw.shape[1] // 2
        ybuf_ref[slot] = _pack_bf16_pairs(yw[:, :half], yw[:, half:])
        nv = nvalid_ref[i]
        start_scatter(slot, nv)

        @pl.when(i == n_used - 1)
        def _():
            @pl.when(i >= 1)
            def _():
                wait_scatter(1 - slot, nvalid_ref[jnp.maximum(i - 1, 0)])
            wait_scatter(slot, nv)


def _moe_ffn(h, route, w_gate, w_up, w_down, layer, tb):
    n = h.shape[0]
    _, e, d, f = w_gate.shape
    k = MOE_TOPK
    mcopies = n * k
    flat_e = route[:, 0:2].astype(jnp.int32).reshape(mcopies)
    flat_w = route[:, 2:4].reshape(mcopies)
    order = jnp.argsort(flat_e).astype(jnp.int32)
    counts = jnp.sum((flat_e[:, None] == jnp.arange(e)[None, :]).astype(jnp.int32), axis=0)
    padded = (counts + tb - 1) // tb * tb
    pad_end = jnp.cumsum(padded)
    pad_start = pad_end - padded
    start = jnp.cumsum(counts) - counts
    nb = (mcopies + e * (tb - 1) + tb - 1) // tb
    p = nb * tb
    blk_start = jnp.arange(nb) * tb
    blk_e = jnp.minimum(jnp.sum((blk_start[:, None] >= pad_end[None, :]).astype(jnp.int32), axis=1),
                        e - 1).astype(jnp.int32)
    blk_off = blk_start - pad_start[blk_e]
    nvalid = jnp.clip(counts[blk_e] - blk_off, 0, tb).astype(jnp.int32)
    n_used = (pad_end[-1] // tb).astype(jnp.int32).reshape(1)
    rows = jnp.arange(tb)[None, :]
    valid = rows < nvalid[:, None]
    sorted_idx = jnp.clip((start[blk_e] + blk_off)[:, None] + rows, 0, mcopies - 1)
    copy_id = order[sorted_idx]
    tok = copy_id // k
    src3 = jnp.where(valid, tok, 0).reshape(nb, 1, tb)
    dst3 = jnp.where(valid, (copy_id % k) * n + tok, 0).reshape(nb, 1, tb)
    row_w = jnp.where(valid, flat_w[copy_id], 0.0).reshape(p)
    roww = jnp.broadcast_to(row_w[:, None], (p, LANES))

    smem = functools.partial(pl.BlockSpec, memory_space=pltpu.SMEM)
    grid_spec = pltpu.PrefetchScalarGridSpec(
        num_scalar_prefetch=3,
        grid=(nb,),
        in_specs=[
            smem((1, 1, tb), lambda i, be, nv, nu: (i, 0, 0)),
            smem((1, 1, tb), lambda i, be, nv, nu: (jnp.minimum(i + 1, nb - 1), 0, 0)),
            smem((1, 1, tb), lambda i, be, nv, nu: (i, 0, 0)),
            pl.BlockSpec((tb, LANES), lambda i, be, nv, nu: (i, 0)),
            pl.BlockSpec((None, None, d, f), lambda i, be, nv, nu: (layer, be[i], 0, 0)),
            pl.BlockSpec((None, None, d, f), lambda i, be, nv, nu: (layer, be[i], 0, 0)),
            pl.BlockSpec((None, None, f, d), lambda i, be, nv, nu: (layer, be[i], 0, 0)),
            pl.BlockSpec(memory_space=pl.ANY),
        ],
        out_specs=pl.BlockSpec(memory_space=pl.ANY),
        scratch_shapes=[
            pltpu.VMEM((d, f), BF16), pltpu.VMEM((d, f), BF16), pltpu.VMEM((f, d), BF16),
            pltpu.VMEM((2, tb, d // 2), jnp.uint32), pltpu.VMEM((2, tb, d // 2), jnp.uint32),
            pltpu.SemaphoreType.DMA((2,)), pltpu.SemaphoreType.DMA((2,)),
        ],
    )
    return pl.pallas_call(
        functools.partial(_ffn_kernel, tb=tb),
        grid_spec=grid_spec,
        out_shape=jax.ShapeDtypeStruct((k * n, d // 2), jnp.uint32),
        compiler_params=_cparams(("arbitrary",), V7X_VMEM_LIMIT),
        name="moe_ffn",
    )(blk_e, nvalid, n_used, src3, src3, dst3, roww, w_gate, w_up, w_down, h)


def _combine_kernel(x_ref, p0_ref, p1_ref, *rest, n_norm, want_x):
    g_refs = rest[:n_norm]
    outs = rest[n_norm:]
    lo0, hi0 = _unpack_bf16_pairs(p0_ref[...])
    lo1, hi1 = _unpack_bf16_pairs(p1_ref[...])
    x = x_ref[...] + jnp.concatenate([lo0 + lo1, hi0 + hi1], axis=1)
    oi = 0
    if want_x:
        outs[0][...] = x
        oi = 1
    r = lax.rsqrt(jnp.mean(x * x, axis=-1, keepdims=True) + RMS_EPS)
    y = x * r
    for g_ref, o_ref in zip(g_refs, outs[oi:]):
        o_ref[...] = (y * g_ref[...]).astype(o_ref.dtype)


def _combine(x2, pairs, gains, out_dtypes, want_x):
    m, d = x2.shape
    tm = _tile(m, 256)
    nrow = m // tm
    in_specs = [pl.BlockSpec((tm, d), lambda i: (i, 0)),
                pl.BlockSpec((tm, d // 2), lambda i: (i, 0)),
                pl.BlockSpec((tm, d // 2), lambda i: (i + nrow, 0))]
    in_specs += [pl.BlockSpec((1, d), lambda i: (0, 0)) for _ in gains]
    out_shape = ([jax.ShapeDtypeStruct((m, d), F32)] if want_x else []) + [
        jax.ShapeDtypeStruct((m, d), dt) for dt in out_dtypes]
    out_specs = [pl.BlockSpec((tm, d), lambda i: (i, 0)) for _ in out_shape]
    return pl.pallas_call(
        functools.partial(_combine_kernel, n_norm=len(gains), want_x=want_x),
        grid=(nrow,),
        in_specs=in_specs,
        out_specs=out_specs,
        out_shape=out_shape,
        compiler_params=_cparams(("parallel",), V7X_VMEM_LIMIT),
        name="moe_combine",
    )(x2, pairs, pairs, *[g.reshape(1, d) for g in gains])


def _t5_bucket(dist):
    n = jnp.maximum(dist, 0)
    exact = REL_BUCKETS // 2
    nf = jnp.maximum(n, 1).astype(F32)
    large = exact + (jnp.log(nf / exact) / math.log(REL_MAX_DIST / exact)
                     * (REL_BUCKETS - exact)).astype(jnp.int32)
    return jnp.where(n < exact, n, jnp.minimum(large, REL_BUCKETS - 1))


def _bias_lookup(table, bucket):
    onehot = jax.nn.one_hot(bucket, REL_BUCKETS, dtype=F32)
    return jnp.einsum('...b,bh->...h', onehot, table, precision=lax.Precision.HIGHEST)


def _cmp_kernel(rk_ref, rv_ref, pe_ref, w1_ref, w2_ref, o_ref):
    half = w1_ref.shape[1] // 2
    for s, r_ref in enumerate((rk_ref, rv_ref)):
        r = r_ref[0, 0].astype(F32)
        ra = (r + pe_ref[s, :, :half]).astype(BF16)
        rb = (r + pe_ref[s, :, half:]).astype(BF16)
        ha = jnp.dot(ra, w1_ref[s, :half, :], preferred_element_type=F32)
        hb = jnp.dot(rb, w1_ref[s, half:, :], preferred_element_type=F32)
        hb_next = jnp.concatenate([hb[1:], hb[:1]], axis=0)
        hid = jax.nn.gelu(ha + hb_next)
        res = jnp.dot(hid.astype(BF16), w2_ref[s], preferred_element_type=F32)
        o_ref[0, 0, s] = (res if s == 0 else res.T).astype(o_ref.dtype)


def _compress(kv, batch, seq, pes, w1s, w2s):
    g, dk = NSA_GROUPS, NSA_DK
    nr = seq // CMP_STRIDE
    assert nr == dk
    ct = kv[:, :2 * g * dk].reshape(batch, nr, CMP_STRIDE, 2, g, dk)
    r = ct.transpose(3, 0, 4, 1, 2, 5).reshape(2, batch, g, nr, CMP_STRIDE * dk)
    pe = jnp.stack(pes).reshape(2, 1, CMP_LEN * dk)
    w1 = jnp.stack(w1s).reshape(2, CMP_LEN * dk, dk).astype(BF16)
    w2 = jnp.stack(w2s).astype(BF16)
    rspec = pl.BlockSpec((1, 1, nr, CMP_STRIDE * dk), lambda b, gi: (b, gi, 0, 0))
    return pl.pallas_call(
        _cmp_kernel,
        grid=(batch, g),
        in_specs=[rspec, rspec,
                  pl.BlockSpec((2, 1, CMP_LEN * dk), lambda b, gi: (0, 0, 0)),
                  pl.BlockSpec((2, CMP_LEN * dk, dk), lambda b, gi: (0, 0, 0)),
                  pl.BlockSpec((2, dk, dk), lambda b, gi: (0, 0, 0))],
        out_specs=pl.BlockSpec((1, 1, 2, nr, dk), lambda b, gi: (b, gi, 0, 0, 0)),
        out_shape=jax.ShapeDtypeStruct((batch, g, 2, nr, dk), BF16),
        compiler_params=_cparams(("parallel", "parallel"), V7X_VMEM_LIMIT),
        name="nsa_compress",
    )(r[0], r[1], pe, w1, w2)


BIAS_DIAG = 0
BIAS_NEXT = 1
BIAS_CONST = 2
BIAS_WIN_EDGE = 3
BIAS_MASKED = 4
N_BIAS_TILES = 5


def _nsa_kernel(q_ref, gate_ref, kvc_ref, ks_ref, kw_ref, vst_ref, vwt_ref, biasc_ref, bias_ref,
                ovt_ref, ext_ref, o_ref,
                selpen_ref, m_ref, l_ref, acc_ref, sfar_ref, sodd_ref, snear_ref, *, hg, tq, n_cmp, n_sblk):
    qb = pl.program_id(2)
    dk = NSA_DK
    nt_dims = (((1,), (1,)), ((), ()))
    q8 = jnp.concatenate([q_ref[:, h * dk:(h + 1) * dk] for h in range(hg)], axis=0)
    kj = lax.broadcasted_iota(jnp.int32, (LANES, tq), 0)
    qi = lax.broadcasted_iota(jnp.int32, (LANES, tq), 1)
    t_abs = qb * tq + qi

    def tile_h(a):
        return jnp.concatenate([a] * hg, axis=1)

    s = lax.dot_general(kvc_ref[0], q8, nt_dims, preferred_element_type=F32) + biasc_ref[...]
    valid_c = (kj * CMP_STRIDE + (CMP_LEN - 1) <= t_abs) & (kj < n_cmp)
    s = s + tile_h(jnp.where(valid_c, 0.0, NEG))
    mx = jnp.maximum(jnp.max(s, axis=0, keepdims=True), 0.5 * NEG)
    p = jnp.exp2(s - mx)
    p = p * (1.0 / jnp.maximum(jnp.sum(p, axis=0, keepdims=True), 1e-30))
    o_c = jnp.dot(kvc_ref[1], p.astype(BF16), preferred_element_type=F32)
    psum = p[:, 0:tq]
    for h in range(1, hg):
        psum = psum + p[:, h * tq:(h + 1) * tq]
    p_hi = psum.astype(BF16)
    p_lo = (psum - p_hi.astype(F32)).astype(BF16)
    imp = (jnp.dot(ovt_ref[...], p_hi, preferred_element_type=F32)
           + jnp.dot(ovt_ref[...], p_lo, preferred_element_type=F32))

    imp = imp[:n_sblk]
    kb_i = lax.broadcasted_iota(jnp.int32, (n_sblk, tq), 0)
    t_b = qb * tq + lax.broadcasted_iota(jnp.int32, (n_sblk, tq), 1)
    cur = lax.shift_right_logical(t_b, int(math.log2(SEL_BLOCK)))
    valid_b = kb_i * SEL_BLOCK <= t_b
    forced = (kb_i == 0) | (kb_i == cur) | (kb_i == cur - 1)
    score = jnp.where(valid_b, jnp.where(forced, FORCED_SCORE, imp), INVALID_SCORE)
    rank = jnp.zeros((n_sblk, tq), F32)
    for j in range(n_sblk):
        cj = score[j:j + 1, :]
        ahead = (cj > score) | ((cj == score) & (kb_i > j))
        rank = rank + jnp.where(ahead, 1.0, 0.0)
    sel = jnp.where(rank < min(SEL_TOPK, n_sblk), 1.0, 0.0)
    if n_sblk < LANES:
        sel = jnp.concatenate([sel, jnp.zeros((LANES - n_sblk, tq), F32)], axis=0)
    selmask = jnp.dot(ext_ref[...], sel.astype(BF16), preferred_element_type=F32)
    selpen_ref[...] = ((selmask - 1.0) * (-NEG)).reshape(selpen_ref.shape)

    win_tiles = WINDOW // LANES
    n_near = win_tiles + 1
    lo = jnp.maximum(qb - win_tiles, 0)
    c_far = bias_ref[BIAS_CONST, 0:1, :]

    def scores(k_ref, row0, n, addend):
        return lax.dot_general(k_ref[pl.ds(row0, n), :], q8, nt_dims, preferred_element_type=F32) + addend

    def colmax(sc):
        return jnp.max(sc, axis=0, keepdims=True)

    m_ref[...] = jnp.full(m_ref.shape, 0.5 * NEG, F32)

    def far_scores(pi, carry):
        kb = 2 * pi
        pen = jnp.concatenate([selpen_ref[kb], selpen_ref[kb + 1]], axis=0)
        sc = scores(ks_ref, pl.multiple_of(kb * LANES, LANES), 2 * LANES, tile_h(pen))
        sfar_ref[pi] = sc
        m_ref[...] = jnp.maximum(m_ref[...], colmax(sc))
        return carry
    lax.fori_loop(0, lo // 2, far_scores, 0)

    @pl.when(lo % 2 == 1)
    def _():
        kb = lo - 1
        sc = scores(ks_ref, pl.multiple_of(kb * LANES, LANES), LANES, tile_h(selpen_ref[kb]))
        sodd_ref[...] = sc
        m_ref[...] = jnp.maximum(m_ref[...], colmax(sc))

    near = []
    m_s = m_ref[...] + c_far
    m_w = jnp.full(m_s.shape, 0.5 * NEG, F32)
    for j in range(n_near):
        d = win_tiles - j
        kb = qb - d
        kb_c = jnp.maximum(kb, 0)
        row0 = pl.multiple_of(kb_c * LANES, LANES)
        plain = BIAS_DIAG if d == 0 else (BIAS_NEXT if d == 1 else BIAS_CONST)
        edge = BIAS_WIN_EDGE if d == win_tiles else plain
        idx_s = jnp.where(kb >= 0, plain, BIAS_MASKED)
        idx_w = jnp.where(kb >= 0, edge, BIAS_MASKED)
        s_s = scores(ks_ref, row0, LANES, bias_ref[idx_s] + tile_h(selpen_ref[kb_c]))
        s_w = scores(kw_ref, row0, LANES, bias_ref[idx_w])
        snear_ref[0, j] = s_s
        snear_ref[1, j] = s_w
        m_s = jnp.maximum(m_s, colmax(s_s))
        m_w = jnp.maximum(m_w, colmax(s_w))
        near.append(kb_c)

    l_s = jnp.zeros(m_s.shape, F32)
    l_w = jnp.zeros(m_s.shape, F32)
    acc_s = jnp.zeros((dk, hg * tq), F32)
    acc_w = jnp.zeros((dk, hg * tq), F32)
    for j, kb_c in enumerate(near):
        p_s = jnp.exp2(snear_ref[0, j] - m_s)
        p_w = jnp.exp2(snear_ref[1, j] - m_w)
        l_s = l_s + jnp.sum(p_s, axis=0, keepdims=True)
        l_w = l_w + jnp.sum(p_w, axis=0, keepdims=True)
        acc_s = acc_s + jnp.dot(vst_ref[kb_c], p_s.astype(BF16), preferred_element_type=F32)
        acc_w = acc_w + jnp.dot(vwt_ref[kb_c], p_w.astype(BF16), preferred_element_type=F32)
    l_ref[...] = l_s
    acc_ref[...] = acc_s

    m_far = m_s - c_far

    def far_pv(pi, carry):
        kb = 2 * pi
        p = jnp.exp2(sfar_ref[pi] - m_far)
        vtt = jnp.concatenate([vst_ref[kb], vst_ref[kb + 1]], axis=1)
        l_ref[...] = l_ref[...] + jnp.sum(p, axis=0, keepdims=True)
        acc_ref[...] = acc_ref[...] + jnp.dot(vtt, p.astype(BF16), preferred_element_type=F32)
        return carry
    lax.fori_loop(0, lo // 2, far_pv, 0)

    @pl.when(lo % 2 == 1)
    def _():
        p = jnp.exp2(sodd_ref[...] - m_far)
        l_ref[...] = l_ref[...] + jnp.sum(p, axis=0, keepdims=True)
        acc_ref[...] = acc_ref[...] + jnp.dot(vst_ref[lo - 1], p.astype(BF16), preferred_element_type=F32)

    o_s = acc_ref[...] * (1.0 / jnp.maximum(l_ref[...], 1e-30))
    o_w = acc_w * (1.0 / jnp.maximum(l_w, 1e-30))
    gt = gate_ref[...].T
    for h in range(hg):
        cols = slice(h * tq, (h + 1) * tq)
        out_t = (gt[3 * h:3 * h + 1] * o_c[:, cols] + gt[3 * h + 1:3 * h + 2] * o_s[:, cols]
                 + gt[3 * h + 2:3 * h + 3] * o_w[:, cols])
        o_ref[:, h * dk:(h + 1) * dk] = out_t.T.astype(o_ref.dtype)


def _nsa_attention(q, gates, kv, kvc, rel_bias, batch, seq, heads):
    g, dk = NSA_GROUPS, NSA_DK
    hg = heads // g
    tq = LANES
    nqb = seq // tq
    n_cmp = (seq - CMP_LEN) // CMP_STRIDE + 1
    n_sblk = seq // SEL_BLOCK
    nr = seq // CMP_STRIDE
    assert nr == LANES and n_sblk <= LANES and n_sblk % 8 == 0 and tq == LANES
    nt = seq // LANES
    rows = hg * tq
    table = rel_bias.astype(F32) * LOG2E
    tt = jnp.arange(seq)
    cmp_end = jnp.arange(LANES) * CMP_STRIDE + CMP_LEN - 1
    bias_c = _bias_lookup(table, _t5_bucket(tt[:, None] - cmp_end[None, :]))
    bias_c = bias_c.reshape(nqb, tq, LANES, g, hg).transpose(3, 0, 2, 4, 1).reshape(g, nqb, LANES, rows)
    ii = jnp.arange(tq)
    n_far = -(-(REL_MAX_DIST + tq - 1) // tq)
    dist = jnp.arange(n_far + 1)[:, None, None] * tq + ii[None, None, :] - ii[None, :, None]
    bias_t = _bias_lookup(table, _t5_bucket(dist))
    bias_t = bias_t.reshape(n_far + 1, tq, tq, g, hg).transpose(0, 3, 1, 4, 2).reshape(n_far + 1, g, tq, rows)
    assert n_far == BIAS_CONST and WINDOW % tq == 0 and WINDOW // tq >= n_far
    key_j = jnp.arange(tq)[:, None]
    qry_i = jnp.tile(jnp.arange(tq), hg)[None, :]
    future = key_j > qry_i
    bias_t = jnp.stack([
        jnp.where(future, NEG, bias_t[0]),
        bias_t[1],
        bias_t[2],
        jnp.where(future, bias_t[2], NEG),
        jnp.full_like(bias_t[2], NEG),
    ])
    cs = np.arange(LANES) * CMP_STRIDE
    ss = np.arange(LANES) * SEL_BLOCK
    ov = np.clip(np.minimum(cs[:, None] + CMP_LEN, ss[None, :] + SEL_BLOCK)
                 - np.maximum(cs[:, None], ss[None, :]), 0, None) / CMP_LEN
    ov[n_cmp:, :] = 0.0
    ov[:, n_sblk:] = 0.0
    overlap_t = jnp.asarray(ov.T, BF16)
    ex = ((np.arange(seq)[:, None] // SEL_BLOCK) == np.arange(LANES)[None, :]).astype(np.float32)
    expand_t = jnp.asarray(ex, BF16)
    kv6 = kv.reshape(batch, nt, LANES, 6, g, dk)
    vt = jnp.stack([kv6[:, :, :, 3], kv6[:, :, :, 5]]).transpose(0, 1, 4, 2, 5, 3)

    kspec = lambda blk: pl.BlockSpec((seq, dk), lambda b, gi, qb: (b, blk * g + gi))
    vspec = lambda br: pl.BlockSpec((None, None, None, nt, dk, LANES),
                                    lambda b, gi, qb: (br, b, gi, 0, 0, 0))
    return pl.pallas_call(
        functools.partial(_nsa_kernel, hg=hg, tq=tq, n_cmp=n_cmp, n_sblk=n_sblk),
        grid=(batch, g, nqb),
        in_specs=[
            pl.BlockSpec((tq, hg * dk), lambda b, gi, qb: (b * nqb + qb, gi)),
            pl.BlockSpec((tq, LANES), lambda b, gi, qb: (b * nqb + qb, gi)),
            pl.BlockSpec((None, None, 2, nr, dk), lambda b, gi, qb: (b, gi, 0, 0, 0)),
            kspec(2), kspec(4), vspec(0), vspec(1),
            pl.BlockSpec((None, None, LANES, rows), lambda b, gi, qb: (gi, qb, 0, 0)),
            pl.BlockSpec((N_BIAS_TILES, None, tq, rows), lambda b, gi, qb: (0, gi, 0, 0)),
            pl.BlockSpec((LANES, LANES), lambda b, gi, qb: (0, 0)),
            pl.BlockSpec((seq, LANES), lambda b, gi, qb: (0, 0)),
        ],
        out_specs=pl.BlockSpec((tq, hg * dk), lambda b, gi, qb: (b * nqb + qb, gi)),
        out_shape=jax.ShapeDtypeStruct((batch * seq, heads * dk), BF16),
        scratch_shapes=[pltpu.VMEM((nt, LANES, tq), F32),
                        pltpu.VMEM((1, rows), F32), pltpu.VMEM((1, rows), F32),
                        pltpu.VMEM((dk, rows), F32),
                        pltpu.VMEM((max(nt // 2, 1), 2 * LANES, rows), F32),
                        pltpu.VMEM((LANES, rows), F32),
                        pltpu.VMEM((2, WINDOW // LANES + 1, LANES, rows), F32)],
        compiler_params=_cparams(("parallel", "parallel", "arbitrary"), V7X_VMEM_LIMIT),
        name="nsa_attention",
    )(q, gates, kvc, kv, kv, vt, vt, bias_c, bias_t, overlap_t, expand_t)


def _nsa_layer(x2, h_kv, h_q, w_kv, cmp_params, rel_bias, w_q, w_o, layer, batch, seq):
    m, d = x2.shape
    g, dk = NSA_GROUPS, NSA_DK
    heads = d // dk
    hg = heads // g
    tm = _tile(seq, 1024)
    kv = _matmul(h_kv, w_kv[None], 0, col0=0, ncols=6 * g * dk, tm=tm, tn=_tile(6 * g * dk, 512),
                 out_dtype=BF16, epilogue=_ep_store, name="nsa_proj_kv")
    (k_pe, k_w1, k_w2, v_pe, v_w1, v_w2) = cmp_params
    kvc = _compress(kv, batch, seq, (k_pe, v_pe), (k_w1, v_w1), (k_w2, v_w2))
    q = _matmul(h_q, w_q, layer, col0=0, ncols=heads * dk, tm=tm, tn=_tile(heads * dk, 512),
                out_dtype=BF16, epilogue=functools.partial(_ep_scale, scale=dk ** -0.5 * LOG2E),
                name="nsa_proj_q")
    wg = w_q[layer, :, heads * dk:].reshape(d, g, hg * 3)
    wg = jnp.pad(wg, ((0, 0), (0, 0), (0, LANES - hg * 3))).reshape(1, d, g * LANES)
    gates = _matmul(h_q, wg, 0, col0=0, ncols=g * LANES, tm=tm, tn=g * LANES, out_dtype=F32,
                    epilogue=_ep_sigmoid, name="nsa_proj_gate")
    o = _nsa_attention(q, gates, kv, kvc, rel_bias, batch, seq, heads)
    tn_o = _tile(d, 512)
    return _matmul(o, w_o, layer, col0=0, ncols=d, tm=tm, tn=tn_o, out_dtype=F32, epilogue=_ep_residual,
                   extra=((x2, (tm, tn_o), lambda j, i: (i, j)),), name="nsa_out_proj")


def kernel(x, ret_norm_g, ret_w_in, ret_w_o, kv_norm_g, nsa_w_kv, cmp_k_pe, cmp_k_w1, cmp_k_w2, cmp_v_pe, cmp_v_w1, cmp_v_w2, rel_bias, nsa_norm_g, nsa_w_q, nsa_w_o, moe_norm_g, moe_w_router_group, moe_w_router_expert, moe_w_gate, moe_w_up, moe_w_down, final_norm_g):
    batch, seq, d = x.shape
    depth = moe_norm_g.shape[0]
    n_a = ret_norm_g.shape[0]
    x2 = x.reshape(batch * seq, d)
    cmp_params = (cmp_k_pe, cmp_k_w1, cmp_k_w2, cmp_v_pe, cmp_v_w1, cmp_v_w2)
    h_kv = h_q = None
    out = None
    for layer in range(depth):
        if layer < n_a:
            x2 = _retention_layer(x2, ret_norm_g[layer], ret_w_in, ret_w_o, layer, batch, seq)
        else:
            j = layer - n_a
            if h_q is None:
                h_kv = _rmsnorm(x2, kv_norm_g)
                h_q = _rmsnorm(x2, nsa_norm_g[j])
            x2 = _nsa_layer(x2, h_kv, h_q, nsa_w_kv, cmp_params, rel_bias, nsa_w_q, nsa_w_o, j,
                            batch, seq)
        hm, route = _route(x2, moe_norm_g[layer], moe_w_router_group[layer], moe_w_router_expert[layer])
        pairs = _moe_ffn(hm, route, moe_w_gate, moe_w_up, moe_w_down, layer, tb=LANES)
        if layer == depth - 1:
            (out,) = _combine(x2, pairs, [final_norm_g], [F32], want_x=False)
        elif layer + 1 == n_a:
            j = layer + 1 - n_a
            x2, h_kv, h_q = _combine(x2, pairs, [kv_norm_g, nsa_norm_g[j]], [BF16, BF16], want_x=True)
        elif layer + 1 < n_a:
            (x2,) = _combine(x2, pairs, [], [], want_x=True)
        else:
            j = layer + 1 - n_a
            x2, h_q = _combine(x2, pairs, [nsa_norm_g[j]], [BF16], want_x=True)
    return out.reshape(batch, seq, d)
```

```python
import functools
import math

import numpy as np
import jax
import jax.numpy as jnp
from jax import lax
from jax.experimental import pallas as pl
from jax.experimental.pallas import tpu as pltpu

RMS_EPS = 1e-6
RET_DK = 256
RET_DV = 2 * RET_DK
ROPE_BASE = 10000.0
NSA_DK = 128
NSA_GROUPS = 4
CMP_LEN = 32
CMP_STRIDE = 16
SEL_BLOCK = 64
SEL_TOPK = 16
WINDOW = 512
FORCED_SCORE = 1e4
INVALID_SCORE = -1e9
REL_BUCKETS = 32
REL_MAX_DIST = 128
MOE_GROUPS = 8
MOE_EPG = 8
MOE_EXPERTS = MOE_GROUPS * MOE_EPG
MOE_TOPK = 2
MOE_ROWS = 256

LANES = 128
V7X_VMEM_LIMIT = 56 * 1024 * 1024
NEG = -1e30
LOG2E = math.log2(math.e)

BF16 = jnp.bfloat16
F32 = jnp.float32


def _tile(dim, pref):
    t = min(dim, pref)
    while dim % t:
        t //= 2
    return t


def _cparams(sem, vmem=None):
    return pltpu.CompilerParams(dimension_semantics=sem, vmem_limit_bytes=vmem)


def _norm_kernel(x_ref, g_ref, o_ref):
    x = x_ref[...]
    y = x * lax.rsqrt(jnp.mean(x * x, axis=-1, keepdims=True) + RMS_EPS)
    o_ref[...] = (y * g_ref[...]).astype(o_ref.dtype)


def _rmsnorm(x2, g, out_dtype=BF16):
    m, d = x2.shape
    tm = _tile(m, 256)
    return pl.pallas_call(
        _norm_kernel,
        grid=(m // tm,),
        in_specs=[pl.BlockSpec((tm, d), lambda i: (i, 0)),
                  pl.BlockSpec((1, d), lambda i: (0, 0))],
        out_specs=pl.BlockSpec((tm, d), lambda i: (i, 0)),
        out_shape=jax.ShapeDtypeStruct((m, d), out_dtype),
        compiler_params=_cparams(("parallel",), V7X_VMEM_LIMIT),
        name="rmsnorm",
    )(x2, g.reshape(1, d))


MM_CHUNK = 256


def _mm_kernel(a_ref, w_ref, *rest, n_extra, epilogue, cast_w):
    extra = rest[:n_extra]
    o_ref = rest[n_extra]
    if cast_w:
        wbf_ref = rest[n_extra + 1]

        @pl.when(pl.program_id(1) == 0)
        def _():
            wbf_ref[...] = w_ref[...].astype(BF16)
    else:
        wbf_ref = w_ref

    a = a_ref[...]
    tn = o_ref.shape[1]
    chunk = min(tn, MM_CHUNK)
    for c0 in range(0, tn, chunk):
        cols = slice(c0, c0 + chunk)
        acc = jnp.dot(a, wbf_ref[:, cols], preferred_element_type=F32)
        epilogue(acc, o_ref, cols, *extra)


def _matmul(a, w, layer, *, col0, ncols, tm, tn, out_dtype, epilogue, extra=(), name):
    m, k = a.shape
    assert col0 % tn == 0 and ncols % tn == 0 and m % tm == 0
    joff = col0 // tn
    cast_w = w.dtype != BF16
    in_specs = [pl.BlockSpec((tm, k), lambda j, i: (i, 0)),
                pl.BlockSpec((None, k, tn), lambda j, i: (layer, 0, j + joff))]
    args = [a, w]
    for arr, bshape, imap in extra:
        in_specs.append(pl.BlockSpec(bshape, imap))
        args.append(arr)
    return pl.pallas_call(
        functools.partial(_mm_kernel, n_extra=len(extra), epilogue=epilogue, cast_w=cast_w),
        grid=(ncols // tn, m // tm),
        in_specs=in_specs,
        out_specs=pl.BlockSpec((tm, tn), lambda j, i: (i, j)),
        out_shape=jax.ShapeDtypeStruct((m, ncols), out_dtype),
        scratch_shapes=[pltpu.VMEM((k, tn), BF16)] if cast_w else [],
        compiler_params=_cparams(("arbitrary", "arbitrary"), V7X_VMEM_LIMIT),
        name=name,
    )(*args)


def _cast_kernel(w_ref, o_ref):
    o_ref[...] = w_ref[...].astype(o_ref.dtype)


def _cast_bf16(w, layer):
    _, k, n = w.shape
    tk = _tile(k, 512)
    return pl.pallas_call(
        _cast_kernel,
        grid=(k // tk,),
        in_specs=[pl.BlockSpec((None, tk, n), lambda i: (layer, i, 0))],
        out_specs=pl.BlockSpec((None, tk, n), lambda i: (0, i, 0)),
        out_shape=jax.ShapeDtypeStruct((1, k, n), BF16),
        compiler_params=_cparams(("parallel",), V7X_VMEM_LIMIT),
        name="cast_bf16",
    )(w)


def _ep_store(acc, o_ref, cols):
    o_ref[:, cols] = acc.astype(o_ref.dtype)


def _ep_residual(acc, o_ref, cols, r_ref):
    o_ref[:, cols] = (r_ref[:, cols] + acc).astype(o_ref.dtype)


def _ep_scale(acc, o_ref, cols, *, scale):
    o_ref[:, cols] = (acc * scale).astype(o_ref.dtype)


def _ep_sigmoid(acc, o_ref, cols):
    o_ref[:, cols] = jax.nn.sigmoid(acc).astype(o_ref.dtype)


def _ep_rotary(acc, o_ref, cols, cos_ref, sin_ref, *, k_first_tile, k_scale):
    assert acc.shape[1] == RET_DK
    scale = jnp.where(pl.program_id(0) >= k_first_tile, k_scale, 1.0).astype(F32)
    cos = cos_ref[...] * scale
    sin = sin_ref[...] * scale
    half = RET_DK // 2
    x1 = acc[:, :half]
    x2 = acc[:, half:]
    o_ref[:, cols.start:cols.start + half] = (x1 * cos - x2 * sin).astype(o_ref.dtype)
    o_ref[:, cols.start + half:cols.stop] = (x1 * sin + x2 * cos).astype(o_ref.dtype)


def _ret_kernel(dchunk_ref, q_ref, k_ref, v_ref, g_ref, dintra_ref, dq_ref, dk_ref, y_ref, state_ref, *, hp):
    c = pl.program_id(2)

    @pl.when(c == 0)
    def _():
        state_ref[...] = jnp.zeros_like(state_ref)

    for j in range(hp):
        h = pl.program_id(1) * hp + j
        q = q_ref[:, j * RET_DK:(j + 1) * RET_DK]
        k = k_ref[:, j * RET_DK:(j + 1) * RET_DK]
        v = v_ref[:, j * RET_DV:(j + 1) * RET_DV]
        scores = lax.dot_general(q, k, (((1,), (1,)), ((), ())), preferred_element_type=F32)
        scores = scores * dintra_ref[j]
        intra = jnp.dot(scores.astype(BF16), v, preferred_element_type=F32)
        cross = jnp.dot(q, state_ref[j].astype(BF16), preferred_element_type=F32)
        dq = dq_ref[j]
        out = jnp.concatenate(
            [intra[:, n:n + LANES] + cross[:, n:n + LANES] * dq for n in range(0, RET_DV, LANES)], axis=1)
        dkc = dk_ref[j]
        kd = (k.astype(F32) * jnp.concatenate([dkc] * (RET_DK // LANES), axis=1)).astype(BF16)
        state_ref[j] = dchunk_ref[h] * state_ref[j] + lax.dot_general(
            kd, v, (((0,), (0,)), ((), ())), preferred_element_type=F32)
        out = out * lax.rsqrt(jnp.mean(out * out, axis=-1, keepdims=True) + RMS_EPS)
        gate = g_ref[:, j * RET_DV:(j + 1) * RET_DV].astype(F32)
        y_ref[:, j * RET_DV:(j + 1) * RET_DV] = (jax.nn.silu(gate) * out).astype(y_ref.dtype)


def _retention_core(qk, vg, batch, seq, heads):
    m = batch * seq
    c = _tile(seq, 256)
    nc = seq // c
    log_gamma = jnp.log1p(-jnp.exp2(-5.0 - jnp.arange(heads, dtype=F32)))
    idx = jnp.arange(c, dtype=F32)
    diff = idx[:, None] - idx[None, :]
    dintra = jnp.where(diff >= 0, jnp.exp(log_gamma[:, None, None] * jnp.maximum(diff, 0.0)), 0.0)
    dq = jnp.broadcast_to(jnp.exp(log_gamma[:, None] * (idx + 1.0))[:, :, None], (heads, c, LANES))
    dk = jnp.broadcast_to(jnp.exp(log_gamma[:, None] * (c - 1.0 - idx))[:, :, None], (heads, c, LANES))
    dchunk = jnp.exp(log_gamma * c)
    hp = 2 if heads % 2 == 0 else 1
    ng = heads // hp
    grid_spec = pltpu.PrefetchScalarGridSpec(
        num_scalar_prefetch=1,
        grid=(batch, ng, nc),
        in_specs=[
            pl.BlockSpec((c, hp * RET_DK), lambda b, h, i, s: (b * nc + i, h)),
            pl.BlockSpec((c, hp * RET_DK), lambda b, h, i, s: (b * nc + i, ng + h)),
            pl.BlockSpec((c, hp * RET_DV), lambda b, h, i, s: (b * nc + i, h)),
            pl.BlockSpec((c, hp * RET_DV), lambda b, h, i, s: (b * nc + i, ng + h)),
            pl.BlockSpec((hp, c, c), lambda b, h, i, s: (h, 0, 0)),
            pl.BlockSpec((hp, c, LANES), lambda b, h, i, s: (h, 0, 0)),
            pl.BlockSpec((hp, c, LANES), lambda b, h, i, s: (h, 0, 0)),
        ],
        out_specs=pl.BlockSpec((c, hp * RET_DV), lambda b, h, i, s: (b * nc + i, h)),
        scratch_shapes=[pltpu.VMEM((hp, RET_DK, RET_DV), F32)],
    )
    return pl.pallas_call(
        functools.partial(_ret_kernel, hp=hp),
        grid_spec=grid_spec,
        out_shape=jax.ShapeDtypeStruct((m, heads * RET_DV), BF16),
        compiler_params=_cparams(("parallel", "parallel", "arbitrary"), V7X_VMEM_LIMIT),
        name="retention_core",
    )(dchunk, qk, qk, vg, vg, dintra, dq, dk)


def _retention_layer(x2, norm_g, w_in, w_o, layer, batch, seq):
    m, d = x2.shape
    heads = d // RET_DK
    hq = heads * RET_DK
    h = _rmsnorm(x2, norm_g)
    pos = jnp.arange(seq, dtype=F32)
    half = RET_DK // 2
    inv = 1.0 / (ROPE_BASE ** (jnp.arange(half, dtype=F32) / half))
    ang = pos[:, None] * inv[None, :]
    cos, sin = jnp.cos(ang), jnp.sin(ang)
    tm = _tile(seq, 1024)
    tn = _tile(hq, 512)
    nrow = seq // tm
    rot = functools.partial(_ep_rotary, k_first_tile=hq // tn, k_scale=RET_DK ** -0.5)
    qk = _matmul(h, w_in, layer, col0=0, ncols=2 * hq, tm=tm, tn=tn, out_dtype=BF16, epilogue=rot,
                 extra=((cos, (tm, half), lambda j, i: (i % nrow, 0)),
                        (sin, (tm, half), lambda j, i: (i % nrow, 0))),
                 name="ret_proj_qk")
    vg = _matmul(h, w_in, layer, col0=2 * hq, ncols=2 * heads * RET_DV, tm=tm, tn=tn, out_dtype=BF16,
                 epilogue=_ep_store, name="ret_proj_vg")
    y = _retention_core(qk, vg, batch, seq, heads)
    tm_o = _tile(seq, 512)
    tn_o = _tile(d, 512)
    return _matmul(y, _cast_bf16(w_o, layer), 0, col0=0, ncols=d, tm=tm_o, tn=tn_o, out_dtype=F32,
                   epilogue=_ep_residual, extra=((x2, (tm_o, tn_o), lambda j, i: (i, j)),),
                   name="ret_out_proj")


def _pack_bf16_pairs(lo, hi):
    lo_bits = pltpu.bitcast(lo.astype(BF16).astype(F32), jnp.uint32)
    hi_bits = pltpu.bitcast(hi.astype(BF16).astype(F32), jnp.uint32)
    return hi_bits | lax.shift_right_logical(lo_bits, jnp.uint32(16))


def _unpack_bf16_pairs(words):
    lo = pltpu.bitcast(lax.shift_left(words, jnp.uint32(16)), F32)
    hi = pltpu.bitcast(words & jnp.uint32(0xFFFF0000), F32)
    return lo, hi


def _route_kernel(x_ref, g_ref, whi_ref, wlo_ref, h_ref, r_ref):
    x = x_ref[...]
    hn = x * lax.rsqrt(jnp.mean(x * x, axis=-1, keepdims=True) + RMS_EPS) * g_ref[...]
    half = hn.shape[1] // 2
    h_ref[...] = _pack_bf16_pairs(hn[:, :half], hn[:, half:])
    hn_hi = hn.astype(BF16)
    hn_lo = (hn - hn_hi.astype(F32)).astype(BF16)
    logits = (jnp.dot(hn_hi, whi_ref[...], preferred_element_type=F32)
              + jnp.dot(hn_hi, wlo_ref[...], preferred_element_type=F32)
              + jnp.dot(hn_lo, whi_ref[...], preferred_element_type=F32))
    lane = lax.broadcasted_iota(jnp.int32, logits.shape, 1)
    is_g = lane < MOE_GROUPS
    lg = jnp.where(is_g, logits, -jnp.inf)
    mg = jnp.max(lg, axis=-1, keepdims=True)
    grp = jnp.min(jnp.where(lg == mg, lane, LANES), axis=-1, keepdims=True)
    p_grp = 1.0 / jnp.sum(jnp.where(is_g, jnp.exp(lg - mg), 0.0), axis=-1, keepdims=True)
    e_lane = lane - MOE_GROUPS
    is_e = (e_lane >= 0) & (e_lane < MOE_EXPERTS) & ((e_lane >> 3) == grp)
    le = jnp.where(is_e, logits, -jnp.inf)
    m1 = jnp.max(le, axis=-1, keepdims=True)
    i1 = jnp.min(jnp.where(le == m1, lane, LANES), axis=-1, keepdims=True)
    le2 = jnp.where(lane == i1, -jnp.inf, le)
    m2 = jnp.max(le2, axis=-1, keepdims=True)
    i2 = jnp.min(jnp.where(le2 == m2, lane, LANES), axis=-1, keepdims=True)
    t = jnp.exp(m2 - m1)
    w1 = p_grp / (1.0 + t)
    w2 = p_grp * t / (1.0 + t)
    e1 = (i1 - MOE_GROUPS).astype(F32)
    e2 = (i2 - MOE_GROUPS).astype(F32)
    r_ref[...] = jnp.where(lane == 0, e1, jnp.where(lane == 1, e2,
                           jnp.where(lane == 2, w1, jnp.where(lane == 3, w2, 0.0))))


def _route(x2, norm_g, w_rg, w_re):
    m, d = x2.shape
    tm = _tile(m, 256)
    wr = jnp.concatenate(
        [w_rg, w_re, jnp.zeros((d, LANES - MOE_GROUPS - MOE_EXPERTS), F32)], axis=1)
    wr_hi = wr.astype(BF16)
    wr_lo = (wr - wr_hi.astype(F32)).astype(BF16)
    return pl.pallas_call(
        _route_kernel,
        grid=(m // tm,),
        in_specs=[pl.BlockSpec((tm, d), lambda i: (i, 0)),
                  pl.BlockSpec((1, d), lambda i: (0, 0)),
                  pl.BlockSpec((d, LANES), lambda i: (0, 0)),
                  pl.BlockSpec((d, LANES), lambda i: (0, 0))],
        out_specs=[pl.BlockSpec((tm, d // 2), lambda i: (i, 0)),
                   pl.BlockSpec((tm, LANES), lambda i: (i, 0))],
        out_shape=[jax.ShapeDtypeStruct((m, d // 2), jnp.uint32),
                   jax.ShapeDtypeStruct((m, LANES), F32)],
        compiler_params=_cparams(("parallel",), V7X_VMEM_LIMIT),
        name="moe_route",
    )(x2, norm_g.reshape(1, d), wr_hi, wr_lo)


def _ffn_kernel(blk_e_ref, nvalid_ref, nused_ref,
                src_cur_ref, src_nxt_ref, dst_ref, roww_ref, wg_ref, wu_ref, wd_ref, h_hbm,
                pairs_hbm,
                wgb_ref, wub_ref, wdb_ref, xbuf_ref, ybuf_ref, gsem, ssem, *, tb):
    i = pl.program_id(0)
    n_used = nused_ref[0]
    slot = i % 2

    def gather_copy(src_ref, r, s):
        return pltpu.make_async_copy(h_hbm.at[pl.ds(src_ref[0, 0, r], 1)],
                                     xbuf_ref.at[s, pl.ds(r, 1)], gsem.at[s])

    def scatter_copy(d_row, r, s):
        return pltpu.make_async_copy(ybuf_ref.at[s, pl.ds(r, 1)],
                                     pairs_hbm.at[pl.ds(d_row, 1)], ssem.at[s])

    def start_gather(src_ref, s):
        for r in range(tb):
            gather_copy(src_ref, r, s).start(priority=1)

    def wait_gather(s):
        pltpu.make_async_copy(h_hbm.at[pl.ds(0, tb)], xbuf_ref.at[s], gsem.at[s]).wait()

    def start_scatter(s, n):
        @pl.when(n == tb)
        def _():
            for r in range(tb):
                scatter_copy(dst_ref[0, 0, r], r, s).start(priority=r % 2)

        @pl.when(n != tb)
        def _():
            def body(r, carry):
                scatter_copy(dst_ref[0, 0, r], r, s).start()
                return carry
            lax.fori_loop(0, n, body, 0)

    def wait_scatter(s, n):
        @pl.when(n == tb)
        def _():
            pltpu.make_async_copy(ybuf_ref.at[s], pairs_hbm.at[pl.ds(0, tb)], ssem.at[s]).wait()

        @pl.when(n != tb)
        def _():
            def body(r, carry):
                scatter_copy(0, r, s).wait()
                return carry
            lax.fori_loop(0, n, body, 0)

    @pl.when(i == 0)
    def _():
        start_gather(src_cur_ref, 0)

    @pl.when(i < n_used)
    def _():
        @pl.when(i + 1 < n_used)
        def _():
            start_gather(src_nxt_ref, 1 - slot)

        first = jnp.logical_or(i == 0, blk_e_ref[i] != blk_e_ref[jnp.maximum(i - 1, 0)])

        @pl.when(first)
        def _():
            wgb_ref[...] = wg_ref[...].astype(BF16)
            wub_ref[...] = wu_ref[...].astype(BF16)
            wdb_ref[...] = wd_ref[...].astype(BF16)

        wait_gather(slot)
        x_lo, x_hi = _unpack_bf16_pairs(xbuf_ref[slot])
        xb = jnp.concatenate([x_lo, x_hi], axis=1).astype(BF16)
        hid = (jax.nn.silu(jnp.dot(xb, wgb_ref[...], preferred_element_type=F32))
               * jnp.dot(xb, wub_ref[...], preferred_element_type=F32))
        y = jnp.dot(hid.astype(BF16), wdb_ref[...], preferred_element_type=F32)

        @pl.when(i >= 2)
        def _():
            wait_scatter(slot, nvalid_ref[jnp.maximum(i - 2, 0)])

        yw = y * roww_ref[:, 0:1]
        half = yw.shape[1] // 2
        ybuf_ref[slot] = _pack_bf16_pairs(yw[:, :half], yw[:, half:])
        nv = nvalid_ref[i]
        start_scatter(slot, nv)

        @pl.when(i == n_used - 1)
        def _():
            @pl.when(i >= 1)
            def _():
                wait_scatter(1 - slot, nvalid_ref[jnp.maximum(i - 1, 0)])
            wait_scatter(slot, nv)


def _moe_ffn(h, route, w_gate, w_up, w_down, layer, tb):
    n = h.shape[0]
    _, e, d, f = w_gate.shape
    k = MOE_TOPK
    mcopies = n * k
    flat_e = route[:, 0:2].astype(jnp.int32).reshape(mcopies)
    flat_w = route[:, 2:4].reshape(mcopies)
    order = jnp.argsort(flat_e).astype(jnp.int32)
    counts = jnp.sum((flat_e[:, None] == jnp.arange(e)[None, :]).astype(jnp.int32), axis=0)
    padded = (counts + tb - 1) // tb * tb
    pad_end = jnp.cumsum(padded)
    pad_start = pad_end - padded
    start = jnp.cumsum(counts) - counts
    nb = (mcopies + e * (tb - 1) + tb - 1) // tb
    p = nb * tb
    blk_start = jnp.arange(nb) * tb
    blk_e = jnp.minimum(jnp.sum((blk_start[:, None] >= pad_end[None, :]).astype(jnp.int32), axis=1),
                        e - 1).astype(jnp.int32)
    blk_off = blk_start - pad_start[blk_e]
    nvalid = jnp.clip(counts[blk_e] - blk_off, 0, tb).astype(jnp.int32)
    n_used = (pad_end[-1] // tb).astype(jnp.int32).reshape(1)
    rows = jnp.arange(tb)[None, :]
    valid = rows < nvalid[:, None]
    sorted_idx = jnp.clip((start[blk_e] + blk_off)[:, None] + rows, 0, mcopies - 1)
    copy_id = order[sorted_idx]
    tok = copy_id // k
    src3 = jnp.where(valid, tok, 0).reshape(nb, 1, tb)
    dst3 = jnp.where(valid, (copy_id % k) * n + tok, 0).reshape(nb, 1, tb)
    row_w = jnp.where(valid, flat_w[copy_id], 0.0).reshape(p)
    roww = jnp.broadcast_to(row_w[:, None], (p, LANES))

    smem = functools.partial(pl.BlockSpec, memory_space=pltpu.SMEM)
    grid_spec = pltpu.PrefetchScalarGridSpec(
        num_scalar_prefetch=3,
        grid=(nb,),
        in_specs=[
            smem((1, 1, tb), lambda i, be, nv, nu: (i, 0, 0)),
            smem((1, 1, tb), lambda i, be, nv, nu: (jnp.minimum(i + 1, nb - 1), 0, 0)),
            smem((1, 1, tb), lambda i, be, nv, nu: (i, 0, 0)),
            pl.BlockSpec((tb, LANES), lambda i, be, nv, nu: (i, 0)),
            pl.BlockSpec((None, None, d, f), lambda i, be, nv, nu: (layer, be[i], 0, 0)),
            pl.BlockSpec((None, None, d, f), lambda i, be, nv, nu: (layer, be[i], 0, 0)),
            pl.BlockSpec((None, None, f, d), lambda i, be, nv, nu: (layer, be[i], 0, 0)),
            pl.BlockSpec(memory_space=pl.ANY),
        ],
        out_specs=pl.BlockSpec(memory_space=pl.ANY),
        scratch_shapes=[
            pltpu.VMEM((d, f), BF16), pltpu.VMEM((d, f), BF16), pltpu.VMEM((f, d), BF16),
            pltpu.VMEM((2, tb, d // 2), jnp.uint32), pltpu.VMEM((2, tb, d // 2), jnp.uint32),
            pltpu.SemaphoreType.DMA((2,)), pltpu.SemaphoreType.DMA((2,)),
        ],
    )
    return pl.pallas_call(
        functools.partial(_ffn_kernel, tb=tb),
        grid_spec=grid_spec,
        out_shape=jax.ShapeDtypeStruct((k * n, d // 2), jnp.uint32),
        compiler_params=_cparams(("arbitrary",), V7X_VMEM_LIMIT),
        name="moe_ffn",
    )(blk_e, nvalid, n_used, src3, src3, dst3, roww, w_gate, w_up, w_down, h)


def _combine_kernel(x_ref, p0_ref, p1_ref, *rest, n_norm, want_x):
    g_refs = rest[:n_norm]
    outs = rest[n_norm:]
    lo0, hi0 = _unpack_bf16_pairs(p0_ref[...])
    lo1, hi1 = _unpack_bf16_pairs(p1_ref[...])
    x = x_ref[...] + jnp.concatenate([lo0 + lo1, hi0 + hi1], axis=1)
    oi = 0
    if want_x:
        outs[0][...] = x
        oi = 1
    r = lax.rsqrt(jnp.mean(x * x, axis=-1, keepdims=True) + RMS_EPS)
    y = x * r
    for g_ref, o_ref in zip(g_refs, outs[oi:]):
        o_ref[...] = (y * g_ref[...]).astype(o_ref.dtype)


def _combine(x2, pairs, gains, out_dtypes, want_x):
    m, d = x2.shape
    tm = _tile(m, 256)
    nrow = m // tm
    in_specs = [pl.BlockSpec((tm, d), lambda i: (i, 0)),
                pl.BlockSpec((tm, d // 2), lambda i: (i, 0)),
                pl.BlockSpec((tm, d // 2), lambda i: (i + nrow, 0))]
    in_specs += [pl.BlockSpec((1, d), lambda i: (0, 0)) for _ in gains]
    out_shape = ([jax.ShapeDtypeStruct((m, d), F32)] if want_x else []) + [
        jax.ShapeDtypeStruct((m, d), dt) for dt in out_dtypes]
    out_specs = [pl.BlockSpec((tm, d), lambda i: (i, 0)) for _ in out_shape]
    return pl.pallas_call(
        functools.partial(_combine_kernel, n_norm=len(gains), want_x=want_x),
        grid=(nrow,),
        in_specs=in_specs,
        out_specs=out_specs,
        out_shape=out_shape,
        compiler_params=_cparams(("parallel",), V7X_VMEM_LIMIT),
        name="moe_combine",
    )(x2, pairs, pairs, *[g.reshape(1, d) for g in gains])


def _t5_bucket(dist):
    n = jnp.maximum(dist, 0)
    exact = REL_BUCKETS // 2
    nf = jnp.maximum(n, 1).astype(F32)
    large = exact + (jnp.log(nf / exact) / math.log(REL_MAX_DIST / exact)
                     * (REL_BUCKETS - exact)).astype(jnp.int32)
    return jnp.where(n < exact, n, jnp.minimum(large, REL_BUCKETS - 1))


def _bias_lookup(table, bucket):
    onehot = jax.nn.one_hot(bucket, REL_BUCKETS, dtype=F32)
    return jnp.einsum('...b,bh->...h', onehot, table, precision=lax.Precision.HIGHEST)


def _cmp_kernel(rk_ref, rv_ref, pe_ref, w1_ref, w2_ref, o_ref):
    half = w1_ref.shape[1] // 2
    for s, r_ref in enumerate((rk_ref, rv_ref)):
        r = r_ref[0, 0].astype(F32)
        ra = (r + pe_ref[s, :, :half]).astype(BF16)
        rb = (r + pe_ref[s, :, half:]).astype(BF16)
        ha = jnp.dot(ra, w1_ref[s, :half, :], preferred_element_type=F32)
        hb = jnp.dot(rb, w1_ref[s, half:, :], preferred_element_type=F32)
        hb_next = jnp.concatenate([hb[1:], hb[:1]], axis=0)
        hid = jax.nn.gelu(ha + hb_next)
        o_ref[0, 0, s] = jnp.dot(hid.astype(BF16), w2_ref[s], preferred_element_type=F32).astype(o_ref.dtype)


def _compress(kv, batch, seq, pes, w1s, w2s):
    g, dk = NSA_GROUPS, NSA_DK
    nr = seq // CMP_STRIDE
    assert nr == dk
    ct = kv[:, :2 * g * dk].reshape(batch, nr, CMP_STRIDE, 2, g, dk)
    r = ct.transpose(3, 0, 4, 1, 2, 5).reshape(2, batch, g, nr, CMP_STRIDE * dk)
    pe = jnp.stack(pes).reshape(2, 1, CMP_LEN * dk)
    w1 = jnp.stack(w1s).reshape(2, CMP_LEN * dk, dk).astype(BF16)
    w2 = jnp.stack(w2s).astype(BF16)
    rspec = pl.BlockSpec((1, 1, nr, CMP_STRIDE * dk), lambda b, gi: (b, gi, 0, 0))
    return pl.pallas_call(
        _cmp_kernel,
        grid=(batch, g),
        in_specs=[rspec, rspec,
                  pl.BlockSpec((2, 1, CMP_LEN * dk), lambda b, gi: (0, 0, 0)),
                  pl.BlockSpec((2, CMP_LEN * dk, dk), lambda b, gi: (0, 0, 0)),
                  pl.BlockSpec((2, dk, dk), lambda b, gi: (0, 0, 0))],
        out_specs=pl.BlockSpec((1, 1, 2, nr, dk), lambda b, gi: (b, gi, 0, 0, 0)),
        out_shape=jax.ShapeDtypeStruct((batch, g, 2, nr, dk), BF16),
        compiler_params=_cparams(("parallel", "parallel"), V7X_VMEM_LIMIT),
        name="nsa_compress",
    )(r[0], r[1], pe, w1, w2)


BIAS_DIAG = 0
BIAS_NEXT = 1
BIAS_CONST = 2
BIAS_WIN_EDGE = 3
BIAS_MASKED = 4
N_BIAS_TILES = 5


def _nsa_kernel(q_ref, gate_ref, kvc_ref, ks_ref, vs_ref, kw_ref, vw_ref, biasc_ref, bias_ref,
                ovt_ref, ext_ref, o_ref,
                selpen_ref, m_ref, l_ref, acc_ref, sfar_ref, sodd_ref, snear_ref, *, hg, tq, n_cmp, n_sblk):
    qb = pl.program_id(2)
    dk = NSA_DK
    nt_dims = (((1,), (1,)), ((), ()))
    q8 = jnp.concatenate([q_ref[:, h * dk:(h + 1) * dk] for h in range(hg)], axis=0)
    kj = lax.broadcasted_iota(jnp.int32, (LANES, tq), 0)
    qi = lax.broadcasted_iota(jnp.int32, (LANES, tq), 1)
    t_abs = qb * tq + qi

    def tile_h(a):
        return jnp.concatenate([a] * hg, axis=1)

    s = lax.dot_general(kvc_ref[0], q8, nt_dims, preferred_element_type=F32) + biasc_ref[...]
    valid_c = (kj * CMP_STRIDE + (CMP_LEN - 1) <= t_abs) & (kj < n_cmp)
    s = s + tile_h(jnp.where(valid_c, 0.0, NEG))
    mx = jnp.maximum(jnp.max(s, axis=0, keepdims=True), 0.5 * NEG)
    p = jnp.exp2(s - mx)
    p = p * (1.0 / jnp.maximum(jnp.sum(p, axis=0, keepdims=True), 1e-30))
    o_c = lax.dot_general(kvc_ref[1], p.astype(BF16), (((0,), (0,)), ((), ())),
                          preferred_element_type=F32)
    psum = p[:, 0:tq]
    for h in range(1, hg):
        psum = psum + p[:, h * tq:(h + 1) * tq]
    p_hi = psum.astype(BF16)
    p_lo = (psum - p_hi.astype(F32)).astype(BF16)
    imp = (jnp.dot(ovt_ref[...], p_hi, preferred_element_type=F32)
           + jnp.dot(ovt_ref[...], p_lo, preferred_element_type=F32))

    imp = imp[:n_sblk]
    kb_i = lax.broadcasted_iota(jnp.int32, (n_sblk, tq), 0)
    t_b = qb * tq + lax.broadcasted_iota(jnp.int32, (n_sblk, tq), 1)
    cur = lax.shift_right_logical(t_b, int(math.log2(SEL_BLOCK)))
    valid_b = kb_i * SEL_BLOCK <= t_b
    forced = (kb_i == 0) | (kb_i == cur) | (kb_i == cur - 1)
    score = jnp.where(valid_b, jnp.where(forced, FORCED_SCORE, imp), INVALID_SCORE)
    rank = jnp.zeros((n_sblk, tq), F32)
    for j in range(n_sblk):
        cj = score[j:j + 1, :]
        ahead = (cj > score) | ((cj == score) & (kb_i > j))
        rank = rank + jnp.where(ahead, 1.0, 0.0)
    sel = jnp.where(rank < min(SEL_TOPK, n_sblk), 1.0, 0.0)
    if n_sblk < LANES:
        sel = jnp.concatenate([sel, jnp.zeros((LANES - n_sblk, tq), F32)], axis=0)
    selmask = jnp.dot(ext_ref[...], sel.astype(BF16), preferred_element_type=F32)
    selpen_ref[...] = ((selmask - 1.0) * (-NEG)).reshape(selpen_ref.shape)

    win_tiles = WINDOW // LANES
    n_near = win_tiles + 1
    lo = jnp.maximum(qb - win_tiles, 0)
    c_far = bias_ref[BIAS_CONST, 0:1, :]

    def scores(k_ref, row0, n, addend):
        return lax.dot_general(k_ref[pl.ds(row0, n), :], q8, nt_dims, preferred_element_type=F32) + addend

    def colmax(sc):
        return jnp.max(sc, axis=0, keepdims=True)

    def pv(v_ref, row0, n, p):
        return lax.dot_general(v_ref[pl.ds(row0, n), :], p.astype(BF16), (((0,), (0,)), ((), ())),
                               preferred_element_type=F32)

    m_ref[...] = jnp.full(m_ref.shape, 0.5 * NEG, F32)

    def far_scores(pi, carry):
        kb = 2 * pi
        pen = jnp.concatenate([selpen_ref[kb], selpen_ref[kb + 1]], axis=0)
        sc = scores(ks_ref, pl.multiple_of(kb * LANES, LANES), 2 * LANES, tile_h(pen))
        sfar_ref[pi] = sc
        m_ref[...] = jnp.maximum(m_ref[...], colmax(sc))
        return carry
    lax.fori_loop(0, lo // 2, far_scores, 0)

    @pl.when(lo % 2 == 1)
    def _():
        kb = lo - 1
        sc = scores(ks_ref, pl.multiple_of(kb * LANES, LANES), LANES, tile_h(selpen_ref[kb]))
        sodd_ref[...] = sc
        m_ref[...] = jnp.maximum(m_ref[...], colmax(sc))

    near = []
    m_s = m_ref[...] + c_far
    m_w = jnp.full(m_s.shape, 0.5 * NEG, F32)
    for j in range(n_near):
        d = win_tiles - j
        kb = qb - d
        kb_c = jnp.maximum(kb, 0)
        row0 = pl.multiple_of(kb_c * LANES, LANES)
        plain = BIAS_DIAG if d == 0 else (BIAS_NEXT if d == 1 else BIAS_CONST)
        edge = BIAS_WIN_EDGE if d == win_tiles else plain
        idx_s = jnp.where(kb >= 0, plain, BIAS_MASKED)
        idx_w = jnp.where(kb >= 0, edge, BIAS_MASKED)
        s_s = scores(ks_ref, row0, LANES, bias_ref[idx_s] + tile_h(selpen_ref[kb_c]))
        s_w = scores(kw_ref, row0, LANES, bias_ref[idx_w])
        snear_ref[0, j] = s_s
        snear_ref[1, j] = s_w
        m_s = jnp.maximum(m_s, colmax(s_s))
        m_w = jnp.maximum(m_w, colmax(s_w))
        near.append(kb_c)

    l_s = jnp.zeros(m_s.shape, F32)
    l_w = jnp.zeros(m_s.shape, F32)
    acc_s = jnp.zeros((dk, hg * tq), F32)
    acc_w = jnp.zeros((dk, hg * tq), F32)
    for j, kb_c in enumerate(near):
        p_s = jnp.exp2(snear_ref[0, j] - m_s)
        p_w = jnp.exp2(snear_ref[1, j] - m_w)
        l_s = l_s + jnp.sum(p_s, axis=0, keepdims=True)
        l_w = l_w + jnp.sum(p_w, axis=0, keepdims=True)
        row0 = pl.multiple_of(kb_c * LANES, LANES)
        acc_s = acc_s + pv(vs_ref, row0, LANES, p_s)
        acc_w = acc_w + pv(vw_ref, row0, LANES, p_w)
    l_ref[...] = l_s
    acc_ref[...] = acc_s

    m_far = m_s - c_far

    def far_pv(pi, carry):
        kb = 2 * pi
        p = jnp.exp2(sfar_ref[pi] - m_far)
        l_ref[...] = l_ref[...] + jnp.sum(p, axis=0, keepdims=True)
        acc_ref[...] = acc_ref[...] + pv(vs_ref, pl.multiple_of(kb * LANES, LANES), 2 * LANES, p)
        return carry
    lax.fori_loop(0, lo // 2, far_pv, 0)

    @pl.when(lo % 2 == 1)
    def _():
        p = jnp.exp2(sodd_ref[...] - m_far)
        l_ref[...] = l_ref[...] + jnp.sum(p, axis=0, keepdims=True)
        acc_ref[...] = acc_ref[...] + pv(vs_ref, pl.multiple_of((lo - 1) * LANES, LANES), LANES, p)

    o_s = acc_ref[...] * (1.0 / jnp.maximum(l_ref[...], 1e-30))
    o_w = acc_w * (1.0 / jnp.maximum(l_w, 1e-30))
    gt = gate_ref[...].T
    for h in range(hg):
        cols = slice(h * tq, (h + 1) * tq)
        out_t = (gt[3 * h:3 * h + 1] * o_c[:, cols] + gt[3 * h + 1:3 * h + 2] * o_s[:, cols]
                 + gt[3 * h + 2:3 * h + 3] * o_w[:, cols])
        o_ref[:, h * dk:(h + 1) * dk] = out_t.T.astype(o_ref.dtype)


def _nsa_attention(q, gates, kv, kvc, rel_bias, batch, seq, heads):
    g, dk = NSA_GROUPS, NSA_DK
    hg = heads // g
    tq = LANES
    nqb = seq // tq
    n_cmp = (seq - CMP_LEN) // CMP_STRIDE + 1
    n_sblk = seq // SEL_BLOCK
    nr = seq // CMP_STRIDE
    assert nr == LANES and n_sblk <= LANES and n_sblk % 8 == 0 and tq == LANES
    nt = seq // LANES
    rows = hg * tq
    table = rel_bias.astype(F32) * LOG2E
    tt = jnp.arange(seq)
    cmp_end = jnp.arange(LANES) * CMP_STRIDE + CMP_LEN - 1
    bias_c = _bias_lookup(table, _t5_bucket(tt[:, None] - cmp_end[None, :]))
    bias_c = bias_c.reshape(nqb, tq, LANES, g, hg).transpose(3, 0, 2, 4, 1).reshape(g, nqb, LANES, rows)
    ii = jnp.arange(tq)
    n_far = -(-(REL_MAX_DIST + tq - 1) // tq)
    dist = jnp.arange(n_far + 1)[:, None, None] * tq + ii[None, None, :] - ii[None, :, None]
    bias_t = _bias_lookup(table, _t5_bucket(dist))
    bias_t = bias_t.reshape(n_far + 1, tq, tq, g, hg).transpose(0, 3, 1, 4, 2).reshape(n_far + 1, g, tq, rows)
    assert n_far == BIAS_CONST and WINDOW % tq == 0 and WINDOW // tq >= n_far
    key_j = jnp.arange(tq)[:, None]
    qry_i = jnp.tile(jnp.arange(tq), hg)[None, :]
    future = key_j > qry_i
    bias_t = jnp.stack([
        jnp.where(future, NEG, bias_t[0]),
        bias_t[1],
        bias_t[2],
        jnp.where(future, bias_t[2], NEG),
        jnp.full_like(bias_t[2], NEG),
    ])
    cs = np.arange(LANES) * CMP_STRIDE
    ss = np.arange(LANES) * SEL_BLOCK
    ov = np.clip(np.minimum(cs[:, None] + CMP_LEN, ss[None, :] + SEL_BLOCK)
                 - np.maximum(cs[:, None], ss[None, :]), 0, None) / CMP_LEN
    ov[n_cmp:, :] = 0.0
    ov[:, n_sblk:] = 0.0
    overlap_t = jnp.asarray(ov.T, BF16)
    ex = ((np.arange(seq)[:, None] // SEL_BLOCK) == np.arange(LANES)[None, :]).astype(np.float32)
    expand_t = jnp.asarray(ex, BF16)

    kspec = lambda blk: pl.BlockSpec((seq, dk), lambda b, gi, qb: (b, blk * g + gi))
    return pl.pallas_call(
        functools.partial(_nsa_kernel, hg=hg, tq=tq, n_cmp=n_cmp, n_sblk=n_sblk),
        grid=(batch, g, nqb),
        in_specs=[
            pl.BlockSpec((tq, hg * dk), lambda b, gi, qb: (b * nqb + qb, gi)),
            pl.BlockSpec((tq, LANES), lambda b, gi, qb: (b * nqb + qb, gi)),
            pl.BlockSpec((None, None, 2, nr, dk), lambda b, gi, qb: (b, gi, 0, 0, 0)),
            kspec(2), kspec(3), kspec(4), kspec(5),
            pl.BlockSpec((None, None, LANES, rows), lambda b, gi, qb: (gi, qb, 0, 0)),
            pl.BlockSpec((N_BIAS_TILES, None, tq, rows), lambda b, gi, qb: (0, gi, 0, 0)),
            pl.BlockSpec((LANES, LANES), lambda b, gi, qb: (0, 0)),
            pl.BlockSpec((seq, LANES), lambda b, gi, qb: (0, 0)),
        ],
        out_specs=pl.BlockSpec((tq, hg * dk), lambda b, gi, qb: (b * nqb + qb, gi)),
        out_shape=jax.ShapeDtypeStruct((batch * seq, heads * dk), BF16),
        scratch_shapes=[pltpu.VMEM((nt, LANES, tq), F32),
                        pltpu.VMEM((1, rows), F32), pltpu.VMEM((1, rows), F32),
                        pltpu.VMEM((dk, rows), F32),
                        pltpu.VMEM((max(nt // 2, 1), 2 * LANES, rows), F32),
                        pltpu.VMEM((LANES, rows), F32),
                        pltpu.VMEM((2, WINDOW // LANES + 1, LANES, rows), F32)],
        compiler_params=_cparams(("parallel", "parallel", "arbitrary"), V7X_VMEM_LIMIT),
        name="nsa_attention",
    )(q, gates, kvc, kv, kv, kv, kv, bias_c, bias_t, overlap_t, expand_t)


def _nsa_layer(x2, h_kv, h_q, w_kv, cmp_params, rel_bias, w_q, w_o, layer, batch, seq):
    m, d = x2.shape
    g, dk = NSA_GROUPS, NSA_DK
    heads = d // dk
    hg = heads // g
    tm = _tile(seq, 1024)
    kv = _matmul(h_kv, w_kv[None], 0, col0=0, ncols=6 * g * dk, tm=tm, tn=_tile(6 * g * dk, 512),
                 out_dtype=BF16, epilogue=_ep_store, name="nsa_proj_kv")
    (k_pe, k_w1, k_w2, v_pe, v_w1, v_w2) = cmp_params
    kvc = _compress(kv, batch, seq, (k_pe, v_pe), (k_w1, v_w1), (k_w2, v_w2))
    q = _matmul(h_q, w_q, layer, col0=0, ncols=heads * dk, tm=tm, tn=_tile(heads * dk, 512),
                out_dtype=BF16, epilogue=functools.partial(_ep_scale, scale=dk ** -0.5 * LOG2E),
                name="nsa_proj_q")
    wg = w_q[layer, :, heads * dk:].reshape(d, g, hg * 3)
    wg = jnp.pad(wg, ((0, 0), (0, 0), (0, LANES - hg * 3))).reshape(1, d, g * LANES)
    gates = _matmul(h_q, wg, 0, col0=0, ncols=g * LANES, tm=tm, tn=g * LANES, out_dtype=F32,
                    epilogue=_ep_sigmoid, name="nsa_proj_gate")
    o = _nsa_attention(q, gates, kv, kvc, rel_bias, batch, seq, heads)
    tn_o = _tile(d, 512)
    return _matmul(o, w_o, layer, col0=0, ncols=d, tm=tm, tn=tn_o, out_dtype=F32, epilogue=_ep_residual,
                   extra=((x2, (tm, tn_o), lambda j, i: (i, j)),), name="nsa_out_proj")


def kernel(x, ret_norm_g, ret_w_in, ret_w_o, kv_norm_g, nsa_w_kv, cmp_k_pe, cmp_k_w1, cmp_k_w2, cmp_v_pe, cmp_v_w1, cmp_v_w2, rel_bias, nsa_norm_g, nsa_w_q, nsa_w_o, moe_norm_g, moe_w_router_group, moe_w_router_expert, moe_w_gate, moe_w_up, moe_w_down, final_norm_g):
    batch, seq, d = x.shape
    depth = moe_norm_g.shape[0]
    n_a = ret_norm_g.shape[0]
    x2 = x.reshape(batch * seq, d)
    cmp_params = (cmp_k_pe, cmp_k_w1, cmp_k_w2, cmp_v_pe, cmp_v_w1, cmp_v_w2)
    h_kv = h_q = None
    out = None
    for layer in range(depth):
        if layer < n_a:
            x2 = _retention_layer(x2, ret_norm_g[layer], ret_w_in, ret_w_o, layer, batch, seq)
        else:
            j = layer - n_a
            if h_q is None:
                h_kv = _rmsnorm(x2, kv_norm_g)
                h_q = _rmsnorm(x2, nsa_norm_g[j])
            x2 = _nsa_layer(x2, h_kv, h_q, nsa_w_kv, cmp_params, rel_bias, nsa_w_q, nsa_w_o, j,
                            batch, seq)
        hm, route = _route(x2, moe_norm_g[layer], moe_w_router_group[layer], moe_w_router_expert[layer])
        pairs = _moe_ffn(hm, route, moe_w_gate, moe_w_up, moe_w_down, layer, tb=MOE_ROWS)
        if layer == depth - 1:
            (out,) = _combine(x2, pairs, [final_norm_g], [F32], want_x=False)
        elif layer + 1 == n_a:
            j = layer + 1 - n_a
            x2, h_kv, h_q = _combine(x2, pairs, [kv_norm_g, nsa_norm_g[j]], [BF16, BF16], want_x=True)
        elif layer + 1 < n_a:
            (x2,) = _combine(x2, pairs, [], [], want_x=True)
        else:
            j = layer + 1 - n_a
            x2, h_q = _combine(x2, pairs, [nsa_norm_g[j]], [BF16], want_x=True)
    return out.reshape(batch, seq, d)
```

```python
import functools
import math

import numpy as np
import jax
import jax.numpy as jnp
from jax import lax
from jax.experimental import pallas as pl
from jax.experimental.pallas import tpu as pltpu

RMS_EPS = 1e-6
RET_DK = 256
RET_DV = 2 * RET_DK
ROPE_BASE = 10000.0
NSA_DK = 128
NSA_GROUPS = 4
CMP_LEN = 32
CMP_STRIDE = 16
SEL_BLOCK = 64
SEL_TOPK = 16
WINDOW = 512
FORCED_SCORE = 1e4
INVALID_SCORE = -1e9
REL_BUCKETS = 32
REL_MAX_DIST = 128
MOE_GROUPS = 8
MOE_EPG = 8
MOE_EXPERTS = MOE_GROUPS * MOE_EPG
MOE_TOPK = 2
MOE_ROWS = 256

LANES = 128
V7X_VMEM_LIMIT = 56 * 1024 * 1024
NEG = -1e30
LOG2E = math.log2(math.e)

BF16 = jnp.bfloat16
F32 = jnp.float32


def _tile(dim, pref):
    t = min(dim, pref)
    while dim % t:
        t //= 2
    return t


def _cparams(sem, vmem=None):
    return pltpu.CompilerParams(dimension_semantics=sem, vmem_limit_bytes=vmem)


def _norm_kernel(x_ref, g_ref, o_ref):
    x = x_ref[...]
    y = x * lax.rsqrt(jnp.mean(x * x, axis=-1, keepdims=True) + RMS_EPS)
    o_ref[...] = (y * g_ref[...]).astype(o_ref.dtype)


def _rmsnorm(x2, g, out_dtype=BF16):
    m, d = x2.shape
    tm = _tile(m, 256)
    return pl.pallas_call(
        _norm_kernel,
        grid=(m // tm,),
        in_specs=[pl.BlockSpec((tm, d), lambda i: (i, 0)),
                  pl.BlockSpec((1, d), lambda i: (0, 0))],
        out_specs=pl.BlockSpec((tm, d), lambda i: (i, 0)),
        out_shape=jax.ShapeDtypeStruct((m, d), out_dtype),
        compiler_params=_cparams(("parallel",), V7X_VMEM_LIMIT),
        name="rmsnorm",
    )(x2, g.reshape(1, d))


MM_CHUNK = 256


def _mm_kernel(a_ref, w_ref, *rest, n_extra, epilogue, cast_w):
    extra = rest[:n_extra]
    o_ref = rest[n_extra]
    if cast_w:
        wbf_ref = rest[n_extra + 1]

        @pl.when(pl.program_id(1) == 0)
        def _():
            wbf_ref[...] = w_ref[...].astype(BF16)
    else:
        wbf_ref = w_ref

    a = a_ref[...]
    tn = o_ref.shape[1]
    chunk = min(tn, MM_CHUNK)
    for c0 in range(0, tn, chunk):
        cols = slice(c0, c0 + chunk)
        acc = jnp.dot(a, wbf_ref[:, cols], preferred_element_type=F32)
        epilogue(acc, o_ref, cols, *extra)


def _matmul(a, w, layer, *, col0, ncols, tm, tn, out_dtype, epilogue, extra=(), name):
    m, k = a.shape
    assert col0 % tn == 0 and ncols % tn == 0 and m % tm == 0
    joff = col0 // tn
    cast_w = w.dtype != BF16
    in_specs = [pl.BlockSpec((tm, k), lambda j, i: (i, 0)),
                pl.BlockSpec((None, k, tn), lambda j, i: (layer, 0, j + joff))]
    args = [a, w]
    for arr, bshape, imap in extra:
        in_specs.append(pl.BlockSpec(bshape, imap))
        args.append(arr)
    return pl.pallas_call(
        functools.partial(_mm_kernel, n_extra=len(extra), epilogue=epilogue, cast_w=cast_w),
        grid=(ncols // tn, m // tm),
        in_specs=in_specs,
        out_specs=pl.BlockSpec((tm, tn), lambda j, i: (i, j)),
        out_shape=jax.ShapeDtypeStruct((m, ncols), out_dtype),
        scratch_shapes=[pltpu.VMEM((k, tn), BF16)] if cast_w else [],
        compiler_params=_cparams(("arbitrary", "arbitrary"), V7X_VMEM_LIMIT),
        name=name,
    )(*args)


def _cast_kernel(w_ref, o_ref):
    o_ref[...] = w_ref[...].astype(o_ref.dtype)


def _cast_bf16(w, layer):
    _, k, n = w.shape
    tk = _tile(k, 512)
    return pl.pallas_call(
        _cast_kernel,
        grid=(k // tk,),
        in_specs=[pl.BlockSpec((None, tk, n), lambda i: (layer, i, 0))],
        out_specs=pl.BlockSpec((None, tk, n), lambda i: (0, i, 0)),
        out_shape=jax.ShapeDtypeStruct((1, k, n), BF16),
        compiler_params=_cparams(("parallel",), V7X_VMEM_LIMIT),
        name="cast_bf16",
    )(w)


def _ep_store(acc, o_ref, cols):
    o_ref[:, cols] = acc.astype(o_ref.dtype)


def _ep_residual(acc, o_ref, cols, r_ref):
    o_ref[:, cols] = (r_ref[:, cols] + acc).astype(o_ref.dtype)


def _ep_scale(acc, o_ref, cols, *, scale):
    o_ref[:, cols] = (acc * scale).astype(o_ref.dtype)


def _ep_sigmoid(acc, o_ref, cols):
    o_ref[:, cols] = jax.nn.sigmoid(acc).astype(o_ref.dtype)


def _ep_rotary(acc, o_ref, cols, cos_ref, sin_ref, *, k_first_tile, k_scale):
    assert acc.shape[1] == RET_DK
    scale = jnp.where(pl.program_id(0) >= k_first_tile, k_scale, 1.0).astype(F32)
    cos = cos_ref[...] * scale
    sin = sin_ref[...] * scale
    half = RET_DK // 2
    x1 = acc[:, :half]
    x2 = acc[:, half:]
    o_ref[:, cols.start:cols.start + half] = (x1 * cos - x2 * sin).astype(o_ref.dtype)
    o_ref[:, cols.start + half:cols.stop] = (x1 * sin + x2 * cos).astype(o_ref.dtype)


def _ret_kernel(dchunk_ref, q_ref, k_ref, v_ref, g_ref, dintra_ref, dq_ref, dk_ref, y_ref, state_ref, *, hp):
    c = pl.program_id(2)

    @pl.when(c == 0)
    def _():
        state_ref[...] = jnp.zeros_like(state_ref)

    for j in range(hp):
        h = pl.program_id(1) * hp + j
        q = q_ref[:, j * RET_DK:(j + 1) * RET_DK]
        k = k_ref[:, j * RET_DK:(j + 1) * RET_DK]
        v = v_ref[:, j * RET_DV:(j + 1) * RET_DV]
        scores = lax.dot_general(q, k, (((1,), (1,)), ((), ())), preferred_element_type=F32)
        scores = scores * dintra_ref[j]
        intra = jnp.dot(scores.astype(BF16), v, preferred_element_type=F32)
        cross = jnp.dot(q, state_ref[j].astype(BF16), preferred_element_type=F32)
        dq = dq_ref[j]
        out = jnp.concatenate(
            [intra[:, n:n + LANES] + cross[:, n:n + LANES] * dq for n in range(0, RET_DV, LANES)], axis=1)
        dkc = dk_ref[j]
        kd = (k.astype(F32) * jnp.concatenate([dkc] * (RET_DK // LANES), axis=1)).astype(BF16)
        state_ref[j] = dchunk_ref[h] * state_ref[j] + lax.dot_general(
            kd, v, (((0,), (0,)), ((), ())), preferred_element_type=F32)
        out = out * lax.rsqrt(jnp.mean(out * out, axis=-1, keepdims=True) + RMS_EPS)
        gate = g_ref[:, j * RET_DV:(j + 1) * RET_DV].astype(F32)
        y_ref[:, j * RET_DV:(j + 1) * RET_DV] = (jax.nn.silu(gate) * out).astype(y_ref.dtype)


def _retention_core(qk, vg, batch, seq, heads):
    m = batch * seq
    c = _tile(seq, 256)
    nc = seq // c
    log_gamma = jnp.log1p(-jnp.exp2(-5.0 - jnp.arange(heads, dtype=F32)))
    idx = jnp.arange(c, dtype=F32)
    diff = idx[:, None] - idx[None, :]
    dintra = jnp.where(diff >= 0, jnp.exp(log_gamma[:, None, None] * jnp.maximum(diff, 0.0)), 0.0)
    dq = jnp.broadcast_to(jnp.exp(log_gamma[:, None] * (idx + 1.0))[:, :, None], (heads, c, LANES))
    dk = jnp.broadcast_to(jnp.exp(log_gamma[:, None] * (c - 1.0 - idx))[:, :, None], (heads, c, LANES))
    dchunk = jnp.exp(log_gamma * c)
    hp = 2 if heads % 2 == 0 else 1
    ng = heads // hp
    grid_spec = pltpu.PrefetchScalarGridSpec(
        num_scalar_prefetch=1,
        grid=(batch, ng, nc),
        in_specs=[
            pl.BlockSpec((c, hp * RET_DK), lambda b, h, i, s: (b * nc + i, h)),
            pl.BlockSpec((c, hp * RET_DK), lambda b, h, i, s: (b * nc + i, ng + h)),
            pl.BlockSpec((c, hp * RET_DV), lambda b, h, i, s: (b * nc + i, h)),
            pl.BlockSpec((c, hp * RET_DV), lambda b, h, i, s: (b * nc + i, ng + h)),
            pl.BlockSpec((hp, c, c), lambda b, h, i, s: (h, 0, 0)),
            pl.BlockSpec((hp, c, LANES), lambda b, h, i, s: (h, 0, 0)),
            pl.BlockSpec((hp, c, LANES), lambda b, h, i, s: (h, 0, 0)),
        ],
        out_specs=pl.BlockSpec((c, hp * RET_DV), lambda b, h, i, s: (b * nc + i, h)),
        scratch_shapes=[pltpu.VMEM((hp, RET_DK, RET_DV), F32)],
    )
    return pl.pallas_call(
        functools.partial(_ret_kernel, hp=hp),
        grid_spec=grid_spec,
        out_shape=jax.ShapeDtypeStruct((m, heads * RET_DV), BF16),
        compiler_params=_cparams(("parallel", "parallel", "arbitrary"), V7X_VMEM_LIMIT),
        name="retention_core",
    )(dchunk, qk, qk, vg, vg, dintra, dq, dk)


def _retention_layer(x2, norm_g, w_in, w_o, layer, batch, seq):
    m, d = x2.shape
    heads = d // RET_DK
    hq = heads * RET_DK
    h = _rmsnorm(x2, norm_g)
    pos = jnp.arange(seq, dtype=F32)
    half = RET_DK // 2
    inv = 1.0 / (ROPE_BASE ** (jnp.arange(half, dtype=F32) / half))
    ang = pos[:, None] * inv[None, :]
    cos, sin = jnp.cos(ang), jnp.sin(ang)
    tm = _tile(seq, 1024)
    tn = _tile(hq, 512)
    nrow = seq // tm
    rot = functools.partial(_ep_rotary, k_first_tile=hq // tn, k_scale=RET_DK ** -0.5)
    qk = _matmul(h, w_in, layer, col0=0, ncols=2 * hq, tm=tm, tn=tn, out_dtype=BF16, epilogue=rot,
                 extra=((cos, (tm, half), lambda j, i: (i % nrow, 0)),
                        (sin, (tm, half), lambda j, i: (i % nrow, 0))),
                 name="ret_proj_qk")
    vg = _matmul(h, w_in, layer, col0=2 * hq, ncols=2 * heads * RET_DV, tm=tm, tn=tn, out_dtype=BF16,
                 epilogue=_ep_store, name="ret_proj_vg")
    y = _retention_core(qk, vg, batch, seq, heads)
    tm_o = _tile(seq, 512)
    tn_o = _tile(d, 512)
    return _matmul(y, _cast_bf16(w_o, layer), 0, col0=0, ncols=d, tm=tm_o, tn=tn_o, out_dtype=F32,
                   epilogue=_ep_residual, extra=((x2, (tm_o, tn_o), lambda j, i: (i, j)),),
                   name="ret_out_proj")


def _pack_bf16_pairs(lo, hi):
    lo_bits = pltpu.bitcast(lo.astype(BF16).astype(F32), jnp.uint32)
    hi_bits = pltpu.bitcast(hi.astype(BF16).astype(F32), jnp.uint32)
    return hi_bits | lax.shift_right_logical(lo_bits, jnp.uint32(16))


def _unpack_bf16_pairs(words):
    lo = pltpu.bitcast(lax.shift_left(words, jnp.uint32(16)), F32)
    hi = pltpu.bitcast(words & jnp.uint32(0xFFFF0000), F32)
    return lo, hi


def _route_kernel(x_ref, g_ref, whi_ref, wlo_ref, h_ref, r_ref):
    x = x_ref[...]
    hn = x * lax.rsqrt(jnp.mean(x * x, axis=-1, keepdims=True) + RMS_EPS) * g_ref[...]
    half = hn.shape[1] // 2
    h_ref[...] = _pack_bf16_pairs(hn[:, :half], hn[:, half:])
    hn_hi = hn.astype(BF16)
    hn_lo = (hn - hn_hi.astype(F32)).astype(BF16)
    logits = (jnp.dot(hn_hi, whi_ref[...], preferred_element_type=F32)
              + jnp.dot(hn_hi, wlo_ref[...], preferred_element_type=F32)
              + jnp.dot(hn_lo, whi_ref[...], preferred_element_type=F32))
    lane = lax.broadcasted_iota(jnp.int32, logits.shape, 1)
    is_g = lane < MOE_GROUPS
    lg = jnp.where(is_g, logits, -jnp.inf)
    mg = jnp.max(lg, axis=-1, keepdims=True)
    grp = jnp.min(jnp.where(lg == mg, lane, LANES), axis=-1, keepdims=True)
    p_grp = 1.0 / jnp.sum(jnp.where(is_g, jnp.exp(lg - mg), 0.0), axis=-1, keepdims=True)
    e_lane = lane - MOE_GROUPS
    is_e = (e_lane >= 0) & (e_lane < MOE_EXPERTS) & ((e_lane >> 3) == grp)
    le = jnp.where(is_e, logits, -jnp.inf)
    m1 = jnp.max(le, axis=-1, keepdims=True)
    i1 = jnp.min(jnp.where(le == m1, lane, LANES), axis=-1, keepdims=True)
    le2 = jnp.where(lane == i1, -jnp.inf, le)
    m2 = jnp.max(le2, axis=-1, keepdims=True)
    i2 = jnp.min(jnp.where(le2 == m2, lane, LANES), axis=-1, keepdims=True)
    t = jnp.exp(m2 - m1)
    w1 = p_grp / (1.0 + t)
    w2 = p_grp * t / (1.0 + t)
    e1 = (i1 - MOE_GROUPS).astype(F32)
    e2 = (i2 - MOE_GROUPS).astype(F32)
    r_ref[...] = jnp.where(lane == 0, e1, jnp.where(lane == 1, e2,
                           jnp.where(lane == 2, w1, jnp.where(lane == 3, w2, 0.0))))


def _route(x2, norm_g, w_rg, w_re):
    m, d = x2.shape
    tm = _tile(m, 256)
    wr = jnp.concatenate(
        [w_rg, w_re, jnp.zeros((d, LANES - MOE_GROUPS - MOE_EXPERTS), F32)], axis=1)
    wr_hi = wr.astype(BF16)
    wr_lo = (wr - wr_hi.astype(F32)).astype(BF16)
    return pl.pallas_call(
        _route_kernel,
        grid=(m // tm,),
        in_specs=[pl.BlockSpec((tm, d), lambda i: (i, 0)),
                  pl.BlockSpec((1, d), lambda i: (0, 0)),
                  pl.BlockSpec((d, LANES), lambda i: (0, 0)),
                  pl.BlockSpec((d, LANES), lambda i: (0, 0))],
        out_specs=[pl.BlockSpec((tm, d // 2), lambda i: (i, 0)),
                   pl.BlockSpec((tm, LANES), lambda i: (i, 0))],
        out_shape=[jax.ShapeDtypeStruct((m, d // 2), jnp.uint32),
                   jax.ShapeDtypeStruct((m, LANES), F32)],
        compiler_params=_cparams(("parallel",), V7X_VMEM_LIMIT),
        name="moe_route",
    )(x2, norm_g.reshape(1, d), wr_hi, wr_lo)


def _ffn_kernel(blk_e_ref, nvalid_ref, nused_ref,
                src_cur_ref, src_nxt_ref, dst_ref, roww_ref, wg_ref, wu_ref, wd_ref, h_hbm,
                pairs_hbm,
                wgb_ref, wub_ref, wdb_ref, xbuf_ref, ybuf_ref, gsem, ssem, *, tb):
    i = pl.program_id(0)
    n_used = nused_ref[0]
    slot = i % 2

    def gather_copy(src_ref, r, s):
        return pltpu.make_async_copy(h_hbm.at[pl.ds(src_ref[0, 0, r], 1)],
                                     xbuf_ref.at[s, pl.ds(r, 1)], gsem.at[s])

    def scatter_copy(d_row, r, s):
        return pltpu.make_async_copy(ybuf_ref.at[s, pl.ds(r, 1)],
                                     pairs_hbm.at[pl.ds(d_row, 1)], ssem.at[s])

    def start_gather(src_ref, s):
        for r in range(tb):
            gather_copy(src_ref, r, s).start(priority=1)

    def wait_gather(s):
        pltpu.make_async_copy(h_hbm.at[pl.ds(0, tb)], xbuf_ref.at[s], gsem.at[s]).wait()

    def start_scatter(s, n):
        @pl.when(n == tb)
        def _():
            for r in range(tb):
                scatter_copy(dst_ref[0, 0, r], r, s).start(priority=r % 2)

        @pl.when(n != tb)
        def _():
            def body(r, carry):
                scatter_copy(dst_ref[0, 0, r], r, s).start()
                return carry
            lax.fori_loop(0, n, body, 0)

    def wait_scatter(s, n):
        @pl.when(n == tb)
        def _():
            pltpu.make_async_copy(ybuf_ref.at[s], pairs_hbm.at[pl.ds(0, tb)], ssem.at[s]).wait()

        @pl.when(n != tb)
        def _():
            def body(r, carry):
                scatter_copy(0, r, s).wait()
                return carry
            lax.fori_loop(0, n, body, 0)

    @pl.when(i == 0)
    def _():
        start_gather(src_cur_ref, 0)

    @pl.when(i < n_used)
    def _():
        @pl.when(i + 1 < n_used)
        def _():
            start_gather(src_nxt_ref, 1 - slot)

        first = jnp.logical_or(i == 0, blk_e_ref[i] != blk_e_ref[jnp.maximum(i - 1, 0)])

        @pl.when(first)
        def _():
            wgb_ref[...] = wg_ref[...].astype(BF16)
            wub_ref[...] = wu_ref[...].astype(BF16)
            wdb_ref[...] = wd_ref[...].astype(BF16)

        wait_gather(slot)
        x_lo, x_hi = _unpack_bf16_pairs(xbuf_ref[slot])
        xb = jnp.concatenate([x_lo, x_hi], axis=1).astype(BF16)
        hid = (jax.nn.silu(jnp.dot(xb, wgb_ref[...], preferred_element_type=F32))
               * jnp.dot(xb, wub_ref[...], preferred_element_type=F32))
        y = jnp.dot(hid.astype(BF16), wdb_ref[...], preferred_element_type=F32)

        @pl.when(i >= 2)
        def _():
            wait_scatter(slot, nvalid_ref[jnp.maximum(i - 2, 0)])

        yw = y * roww_ref[:, 0:1]
        half = yw.shape[1] // 2
        ybuf_ref[slot] = _pack_bf16_pairs(yw[:, :half], yw[:, half:])
        nv = nvalid_ref[i]
        start_scatter(slot, nv)

        @pl.when(i == n_used - 1)
        def _():
            @pl.when(i >= 1)
            def _():
                wait_scatter(1 - slot, nvalid_ref[jnp.maximum(i - 1, 0)])
            wait_scatter(slot, nv)


def _moe_ffn(h, route, w_gate, w_up, w_down, layer, tb):
    n = h.shape[0]
    _, e, d, f = w_gate.shape
    k = MOE_TOPK
    mcopies = n * k
    flat_e = route[:, 0:2].astype(jnp.int32).reshape(mcopies)
    flat_w = route[:, 2:4].reshape(mcopies)
    order = jnp.argsort(flat_e).astype(jnp.int32)
    counts = jnp.sum((flat_e[:, None] == jnp.arange(e)[None, :]).astype(jnp.int32), axis=0)
    padded = (counts + tb - 1) // tb * tb
    pad_end = jnp.cumsum(padded)
    pad_start = pad_end - padded
    start = jnp.cumsum(counts) - counts
    nb = (mcopies + e * (tb - 1) + tb - 1) // tb
    p = nb * tb
    blk_start = jnp.arange(nb) * tb
    blk_e = jnp.minimum(jnp.sum((blk_start[:, None] >= pad_end[None, :]).astype(jnp.int32), axis=1),
                        e - 1).astype(jnp.int32)
    blk_off = blk_start - pad_start[blk_e]
    nvalid = jnp.clip(counts[blk_e] - blk_off, 0, tb).astype(jnp.int32)
    n_used = (pad_end[-1] // tb).astype(jnp.int32).reshape(1)
    rows = jnp.arange(tb)[None, :]
    valid = rows < nvalid[:, None]
    sorted_idx = jnp.clip((start[blk_e] + blk_off)[:, None] + rows, 0, mcopies - 1)
    copy_id = order[sorted_idx]
    tok = copy_id // k
    src3 = jnp.where(valid, tok, 0).reshape(nb, 1, tb)
    dst3 = jnp.where(valid, (copy_id % k) * n + tok, 0).reshape(nb, 1, tb)
    row_w = jnp.where(valid, flat_w[copy_id], 0.0).reshape(p)
    roww = jnp.broadcast_to(row_w[:, None], (p, LANES))

    smem = functools.partial(pl.BlockSpec, memory_space=pltpu.SMEM)
    grid_spec = pltpu.PrefetchScalarGridSpec(
        num_scalar_prefetch=3,
        grid=(nb,),
        in_specs=[
            smem((1, 1, tb), lambda i, be, nv, nu: (i, 0, 0)),
            smem((1, 1, tb), lambda i, be, nv, nu: (jnp.minimum(i + 1, nb - 1), 0, 0)),
            smem((1, 1, tb), lambda i, be, nv, nu: (i, 0, 0)),
            pl.BlockSpec((tb, LANES), lambda i, be, nv, nu: (i, 0)),
            pl.BlockSpec((None, None, d, f), lambda i, be, nv, nu: (layer, be[i], 0, 0)),
            pl.BlockSpec((None, None, d, f), lambda i, be, nv, nu: (layer, be[i], 0, 0)),
            pl.BlockSpec((None, None, f, d), lambda i, be, nv, nu: (layer, be[i], 0, 0)),
            pl.BlockSpec(memory_space=pl.ANY),
        ],
        out_specs=pl.BlockSpec(memory_space=pl.ANY),
        scratch_shapes=[
            pltpu.VMEM((d, f), BF16), pltpu.VMEM((d, f), BF16), pltpu.VMEM((f, d), BF16),
            pltpu.VMEM((2, tb, d // 2), jnp.uint32), pltpu.VMEM((2, tb, d // 2), jnp.uint32),
            pltpu.SemaphoreType.DMA((2,)), pltpu.SemaphoreType.DMA((2,)),
        ],
    )
    return pl.pallas_call(
        functools.partial(_ffn_kernel, tb=tb),
        grid_spec=grid_spec,
        out_shape=jax.ShapeDtypeStruct((k * n, d // 2), jnp.uint32),
        compiler_params=_cparams(("arbitrary",), V7X_VMEM_LIMIT),
        name="moe_ffn",
    )(blk_e, nvalid, n_used, src3, src3, dst3, roww, w_gate, w_up, w_down, h)


def _combine_kernel(x_ref, p0_ref, p1_ref, *rest, n_norm, want_x):
    g_refs = rest[:n_norm]
    outs = rest[n_norm:]
    lo0, hi0 = _unpack_bf16_pairs(p0_ref[...])
    lo1, hi1 = _unpack_bf16_pairs(p1_ref[...])
    x = x_ref[...] + jnp.concatenate([lo0 + lo1, hi0 + hi1], axis=1)
    oi = 0
    if want_x:
        outs[0][...] = x
        oi = 1
    r = lax.rsqrt(jnp.mean(x * x, axis=-1, keepdims=True) + RMS_EPS)
    y = x * r
    for g_ref, o_ref in zip(g_refs, outs[oi:]):
        o_ref[...] = (y * g_ref[...]).astype(o_ref.dtype)


def _combine(x2, pairs, gains, out_dtypes, want_x):
    m, d = x2.shape
    tm = _tile(m, 256)
    nrow = m // tm
    in_specs = [pl.BlockSpec((tm, d), lambda i: (i, 0)),
                pl.BlockSpec((tm, d // 2), lambda i: (i, 0)),
                pl.BlockSpec((tm, d // 2), lambda i: (i + nrow, 0))]
    in_specs += [pl.BlockSpec((1, d), lambda i: (0, 0)) for _ in gains]
    out_shape = ([jax.ShapeDtypeStruct((m, d), F32)] if want_x else []) + [
        jax.ShapeDtypeStruct((m, d), dt) for dt in out_dtypes]
    out_specs = [pl.BlockSpec((tm, d), lambda i: (i, 0)) for _ in out_shape]
    return pl.pallas_call(
        functools.partial(_combine_kernel, n_norm=len(gains), want_x=want_x),
        grid=(nrow,),
        in_specs=in_specs,
        out_specs=out_specs,
        out_shape=out_shape,
        compiler_params=_cparams(("parallel",), V7X_VMEM_LIMIT),
        name="moe_combine",
    )(x2, pairs, pairs, *[g.reshape(1, d) for g in gains])


def _t5_bucket(dist):
    n = jnp.maximum(dist, 0)
    exact = REL_BUCKETS // 2
    nf = jnp.maximum(n, 1).astype(F32)
    large = exact + (jnp.log(nf / exact) / math.log(REL_MAX_DIST / exact)
                     * (REL_BUCKETS - exact)).astype(jnp.int32)
    return jnp.where(n < exact, n, jnp.minimum(large, REL_BUCKETS - 1))


def _bias_lookup(table, bucket):
    onehot = jax.nn.one_hot(bucket, REL_BUCKETS, dtype=F32)
    return jnp.einsum('...b,bh->...h', onehot, table, precision=lax.Precision.HIGHEST)


def _cmp_kernel(rk_ref, rv_ref, pe_ref, w1_ref, w2_ref, o_ref):
    half = w1_ref.shape[1] // 2
    for s, r_ref in enumerate((rk_ref, rv_ref)):
        r = r_ref[0, 0].astype(F32)
        ra = (r + pe_ref[s, :, :half]).astype(BF16)
        rb = (r + pe_ref[s, :, half:]).astype(BF16)
        ha = jnp.dot(ra, w1_ref[s, :half, :], preferred_element_type=F32)
        hb = jnp.dot(rb, w1_ref[s, half:, :], preferred_element_type=F32)
        hb_next = jnp.concatenate([hb[1:], hb[:1]], axis=0)
        hid = jax.nn.gelu(ha + hb_next)
        o_ref[0, 0, s] = jnp.dot(hid.astype(BF16), w2_ref[s], preferred_element_type=F32).astype(o_ref.dtype)


def _compress(kv, batch, seq, pes, w1s, w2s):
    g, dk = NSA_GROUPS, NSA_DK
    nr = seq // CMP_STRIDE
    assert nr == dk
    ct = kv[:, :2 * g * dk].reshape(batch, nr, CMP_STRIDE, 2, g, dk)
    r = ct.transpose(3, 0, 4, 1, 2, 5).reshape(2, batch, g, nr, CMP_STRIDE * dk)
    pe = jnp.stack(pes).reshape(2, 1, CMP_LEN * dk)
    w1 = jnp.stack(w1s).reshape(2, CMP_LEN * dk, dk).astype(BF16)
    w2 = jnp.stack(w2s).astype(BF16)
    rspec = pl.BlockSpec((1, 1, nr, CMP_STRIDE * dk), lambda b, gi: (b, gi, 0, 0))
    return pl.pallas_call(
        _cmp_kernel,
        grid=(batch, g),
        in_specs=[rspec, rspec,
                  pl.BlockSpec((2, 1, CMP_LEN * dk), lambda b, gi: (0, 0, 0)),
                  pl.BlockSpec((2, CMP_LEN * dk, dk), lambda b, gi: (0, 0, 0)),
                  pl.BlockSpec((2, dk, dk), lambda b, gi: (0, 0, 0))],
        out_specs=pl.BlockSpec((1, 1, 2, nr, dk), lambda b, gi: (b, gi, 0, 0, 0)),
        out_shape=jax.ShapeDtypeStruct((batch, g, 2, nr, dk), BF16),
        compiler_params=_cparams(("parallel", "parallel"), V7X_VMEM_LIMIT),
        name="nsa_compress",
    )(r[0], r[1], pe, w1, w2)


BIAS_DIAG = 0
BIAS_NEXT = 1
BIAS_CONST = 2
BIAS_WIN_EDGE = 3
BIAS_MASKED = 4
N_BIAS_TILES = 5
CMP_PAD = 16
CMP_BAND = CMP_PAD + LANES // CMP_STRIDE
FAR_GROUP = 4


def _nsa_kernel(q_ref, gate_ref, kvc_ref, ks_ref, vs_ref, kw_ref, vw_ref, biasc_ref, bias_ref,
                ovt_ref, ext_ref, o_ref,
                selpen_ref, m_ref, l_ref, acc_ref, sfar_ref, snear_ref, oc_ref, ow_ref, scmp_ref,
                *, hg, tq, n_cmp, n_sblk):
    qb = pl.program_id(2)
    dk = NSA_DK
    nt_dims = (((1,), (1,)), ((), ()))
    q8 = jnp.concatenate([q_ref[:, h * dk:(h + 1) * dk] for h in range(hg)], axis=0)
    kj = lax.broadcasted_iota(jnp.int32, (LANES, tq), 0)
    qi = lax.broadcasted_iota(jnp.int32, (LANES, tq), 1)
    t_abs = qb * tq + qi

    def tile_h(a):
        return jnp.concatenate([a] * hg, axis=1)

    win_tiles = WINDOW // LANES
    n_near = win_tiles + 1
    lo = jnp.maximum(qb - win_tiles, 0)
    c_far = bias_ref[BIAS_CONST, 0:1, :]

    def scores(k_ref, row0, n, addend):
        return lax.dot_general(k_ref[pl.ds(row0, n), :], q8, nt_dims, preferred_element_type=F32) + addend

    def colmax(sc):
        return jnp.max(sc, axis=0, keepdims=True)

    def pv(v_ref, row0, n, p):
        return lax.dot_general(v_ref[pl.ds(row0, n), :], p.astype(BF16), (((0,), (0,)), ((), ())),
                               preferred_element_type=F32)

    near_row0, near_ok = [], []
    for j in range(n_near):
        kb = qb - (win_tiles - j)
        near_row0.append(pl.multiple_of(jnp.maximum(kb, 0) * LANES, LANES))
        near_ok.append(kb >= 0)

    def near_bias(j, window_branch):
        d = win_tiles - j
        tile = BIAS_DIAG if d == 0 else (BIAS_NEXT if d == 1 else BIAS_CONST)
        if window_branch and d == win_tiles:
            tile = BIAS_WIN_EDGE
        return bias_ref[jnp.where(near_ok[j], tile, BIAS_MASKED)]

    m_w = jnp.full(c_far.shape, 0.5 * NEG, F32)
    for j in range(n_near):
        s_w = scores(kw_ref, near_row0[j], LANES, near_bias(j, True))
        snear_ref[1, j] = s_w
        m_w = jnp.maximum(m_w, colmax(s_w))
    l_w = jnp.zeros(c_far.shape, F32)
    acc_w = jnp.zeros((dk, hg * tq), F32)
    for j in range(n_near):
        p_w = jnp.exp2(snear_ref[1, j] - m_w)
        l_w = l_w + jnp.sum(p_w, axis=0, keepdims=True)
        acc_w = acc_w + pv(vw_ref, near_row0[j], LANES, p_w)
    ow_ref[...] = acc_w * (1.0 / jnp.maximum(l_w, 1e-30))

    s = lax.dot_general(kvc_ref[0], q8, nt_dims, preferred_element_type=F32) + c_far
    scmp_ref[pl.ds(0, CMP_PAD), :] = jnp.zeros((CMP_PAD, hg * tq), F32)
    scmp_ref[pl.ds(CMP_PAD, LANES), :] = s
    band0 = pl.multiple_of(qb * (tq // CMP_STRIDE), 8)
    scmp_ref[pl.ds(band0, CMP_BAND), :] = scmp_ref[pl.ds(band0, CMP_BAND), :] + biasc_ref[...]
    s = scmp_ref[pl.ds(CMP_PAD, LANES), :]
    valid_c = (kj * CMP_STRIDE + (CMP_LEN - 1) <= t_abs) & (kj < n_cmp)
    s = s + tile_h(jnp.where(valid_c, 0.0, NEG))
    mx = jnp.maximum(jnp.max(s, axis=0, keepdims=True), 0.5 * NEG)
    p = jnp.exp2(s - mx)
    p = p * (1.0 / jnp.maximum(jnp.sum(p, axis=0, keepdims=True), 1e-30))
    oc_ref[...] = lax.dot_general(kvc_ref[1], p.astype(BF16), (((0,), (0,)), ((), ())),
                                  preferred_element_type=F32)
    psum = p[:, 0:tq]
    for h in range(1, hg):
        psum = psum + p[:, h * tq:(h + 1) * tq]
    p_hi = psum.astype(BF16)
    p_lo = (psum - p_hi.astype(F32)).astype(BF16)
    imp = (jnp.dot(ovt_ref[...], p_hi, preferred_element_type=F32)
           + jnp.dot(ovt_ref[...], p_lo, preferred_element_type=F32))

    imp = imp[:n_sblk]
    kb_i = lax.broadcasted_iota(jnp.int32, (n_sblk, tq), 0)
    t_b = qb * tq + lax.broadcasted_iota(jnp.int32, (n_sblk, tq), 1)
    cur = lax.shift_right_logical(t_b, int(math.log2(SEL_BLOCK)))
    valid_b = kb_i * SEL_BLOCK <= t_b
    forced = (kb_i == 0) | (kb_i == cur) | (kb_i == cur - 1)
    score = jnp.where(valid_b, jnp.where(forced, FORCED_SCORE, imp), INVALID_SCORE)
    rank = jnp.zeros((n_sblk, tq), F32)
    for j in range(n_sblk):
        cj = score[j:j + 1, :]
        ahead = (cj > score) | ((cj == score) & (kb_i > j))
        rank = rank + jnp.where(ahead, 1.0, 0.0)
    sel = jnp.where(rank < min(SEL_TOPK, n_sblk), 1.0, 0.0)
    if n_sblk < LANES:
        sel = jnp.concatenate([sel, jnp.zeros((LANES - n_sblk, tq), F32)], axis=0)
    selmask = jnp.dot(ext_ref[...], sel.astype(BF16), preferred_element_type=F32)
    selpen_ref[...] = ((selmask - 1.0) * (-NEG)).reshape(selpen_ref.shape)

    m_near = jnp.full(c_far.shape, 0.5 * NEG, F32)
    for j in range(n_near):
        kb_c = lax.shift_right_logical(near_row0[j], int(math.log2(LANES)))
        s_s = scores(ks_ref, near_row0[j], LANES, near_bias(j, False) + tile_h(selpen_ref[kb_c]))
        snear_ref[0, j] = s_s
        m_near = jnp.maximum(m_near, colmax(s_s))

    assert FAR_GROUP - 1 <= win_tiles
    n_grp = (lo + FAR_GROUP - 1) // FAR_GROUP
    m_ref[...] = jnp.full(m_ref.shape, 0.5 * NEG, F32)

    def far_scores(gi, carry):
        kb = FAR_GROUP * gi
        pen = jnp.concatenate(
            [selpen_ref[kb + t] + jnp.where(kb + t < lo, 0.0, NEG) for t in range(FAR_GROUP)], axis=0)
        sc = scores(ks_ref, pl.multiple_of(kb * LANES, LANES), FAR_GROUP * LANES, tile_h(pen))
        sfar_ref[gi] = sc
        m_ref[...] = jnp.maximum(m_ref[...], colmax(sc))
        return carry
    lax.fori_loop(0, n_grp, far_scores, 0)

    m_s = jnp.maximum(m_near, m_ref[...] + c_far)
    l_s = jnp.zeros(m_s.shape, F32)
    acc_s = jnp.zeros((dk, hg * tq), F32)
    for j in range(n_near):
        p_s = jnp.exp2(snear_ref[0, j] - m_s)
        l_s = l_s + jnp.sum(p_s, axis=0, keepdims=True)
        acc_s = acc_s + pv(vs_ref, near_row0[j], LANES, p_s)
    l_ref[...] = l_s
    acc_ref[...] = acc_s

    m_far = m_s - c_far

    def far_pv(gi, carry):
        row0 = pl.multiple_of(FAR_GROUP * gi * LANES, LANES)
        p = jnp.exp2(sfar_ref[gi] - m_far)
        l_ref[...] = l_ref[...] + jnp.sum(p, axis=0, keepdims=True)
        acc_ref[...] = acc_ref[...] + pv(vs_ref, row0, FAR_GROUP * LANES, p)
        return carry
    lax.fori_loop(0, n_grp, far_pv, 0)

    o_s = acc_ref[...] * (1.0 / jnp.maximum(l_ref[...], 1e-30))
    o_w = ow_ref[...]
    o_c = oc_ref[...]
    gt = gate_ref[...].T
    for h in range(hg):
        cols = slice(h * tq, (h + 1) * tq)
        out_t = (gt[3 * h:3 * h + 1] * o_c[:, cols] + gt[3 * h + 1:3 * h + 2] * o_s[:, cols]
                 + gt[3 * h + 2:3 * h + 3] * o_w[:, cols])
        o_ref[:, h * dk:(h + 1) * dk] = out_t.T.astype(o_ref.dtype)


def _nsa_attention(q, gates, kv, kvc, rel_bias, batch, seq, heads):
    g, dk = NSA_GROUPS, NSA_DK
    hg = heads // g
    tq = LANES
    nqb = seq // tq
    n_cmp = (seq - CMP_LEN) // CMP_STRIDE + 1
    n_sblk = seq // SEL_BLOCK
    nr = seq // CMP_STRIDE
    assert nr == LANES and n_sblk <= LANES and n_sblk % 8 == 0 and tq == LANES
    nt = seq // LANES
    rows = hg * tq
    table = rel_bias.astype(F32) * LOG2E
    assert (REL_MAX_DIST + CMP_LEN - 1 + CMP_STRIDE - 1) // CMP_STRIDE <= CMP_PAD and tq % CMP_STRIDE == 0
    band_w = jnp.arange(CMP_BAND)[:, None]
    dist_c = jnp.arange(tq)[None, :] - (band_w - CMP_PAD) * CMP_STRIDE - (CMP_LEN - 1)
    bias_c = _bias_lookup(table, _t5_bucket(dist_c)) - table[REL_BUCKETS - 1]
    bias_c = bias_c.reshape(CMP_BAND, tq, g, hg).transpose(2, 0, 3, 1).reshape(g, CMP_BAND, rows)
    ii = jnp.arange(tq)
    n_far = -(-(REL_MAX_DIST + tq - 1) // tq)
    dist = jnp.arange(n_far + 1)[:, None, None] * tq + ii[None, None, :] - ii[None, :, None]
    bias_t = _bias_lookup(table, _t5_bucket(dist))
    bias_t = bias_t.reshape(n_far + 1, tq, tq, g, hg).transpose(0, 3, 1, 4, 2).reshape(n_far + 1, g, tq, rows)
    assert n_far == BIAS_CONST and WINDOW % tq == 0 and WINDOW // tq >= n_far
    key_j = jnp.arange(tq)[:, None]
    qry_i = jnp.tile(jnp.arange(tq), hg)[None, :]
    future = key_j > qry_i
    bias_t = jnp.stack([
        jnp.where(future, NEG, bias_t[0]),
        bias_t[1],
        bias_t[2],
        jnp.where(future, bias_t[2], NEG),
        jnp.full_like(bias_t[2], NEG),
    ])
    cs = np.arange(LANES) * CMP_STRIDE
    ss = np.arange(LANES) * SEL_BLOCK
    ov = np.clip(np.minimum(cs[:, None] + CMP_LEN, ss[None, :] + SEL_BLOCK)
                 - np.maximum(cs[:, None], ss[None, :]), 0, None) / CMP_LEN
    ov[n_cmp:, :] = 0.0
    ov[:, n_sblk:] = 0.0
    overlap_t = jnp.asarray(ov.T, BF16)
    ex = ((np.arange(seq)[:, None] // SEL_BLOCK) == np.arange(LANES)[None, :]).astype(np.float32)
    expand_t = jnp.asarray(ex, BF16)

    kspec = lambda blk: pl.BlockSpec((seq, dk), lambda b, gi, qb: (b, blk * g + gi))
    return pl.pallas_call(
        functools.partial(_nsa_kernel, hg=hg, tq=tq, n_cmp=n_cmp, n_sblk=n_sblk),
        grid=(batch, g, nqb),
        in_specs=[
            pl.BlockSpec((tq, hg * dk), lambda b, gi, qb: (b * nqb + qb, gi)),
            pl.BlockSpec((tq, LANES), lambda b, gi, qb: (b * nqb + qb, gi)),
            pl.BlockSpec((None, None, 2, nr, dk), lambda b, gi, qb: (b, gi, 0, 0, 0)),
            kspec(2), kspec(3), kspec(4), kspec(5),
            pl.BlockSpec((None, CMP_BAND, rows), lambda b, gi, qb: (gi, 0, 0)),
            pl.BlockSpec((N_BIAS_TILES, None, tq, rows), lambda b, gi, qb: (0, gi, 0, 0)),
            pl.BlockSpec((LANES, LANES), lambda b, gi, qb: (0, 0)),
            pl.BlockSpec((seq, LANES), lambda b, gi, qb: (0, 0)),
        ],
        out_specs=pl.BlockSpec((tq, hg * dk), lambda b, gi, qb: (b * nqb + qb, gi)),
        out_shape=jax.ShapeDtypeStruct((batch * seq, heads * dk), BF16),
        scratch_shapes=[pltpu.VMEM((nt, LANES, tq), F32),
                        pltpu.VMEM((1, rows), F32), pltpu.VMEM((1, rows), F32),
                        pltpu.VMEM((dk, rows), F32),
                        pltpu.VMEM((-(-nt // FAR_GROUP), FAR_GROUP * LANES, rows), F32),
                        pltpu.VMEM((2, WINDOW // LANES + 1, LANES, rows), F32),
                        pltpu.VMEM((dk, rows), F32), pltpu.VMEM((dk, rows), F32),
                        pltpu.VMEM((CMP_PAD + LANES, rows), F32)],
        compiler_params=_cparams(("parallel", "parallel", "arbitrary"), V7X_VMEM_LIMIT),
        name="nsa_attention",
    )(q, gates, kvc, kv, kv, kv, kv, bias_c, bias_t, overlap_t, expand_t)


def _nsa_layer(x2, h_kv, h_q, w_kv, cmp_params, rel_bias, w_q, w_o, layer, batch, seq):
    m, d = x2.shape
    g, dk = NSA_GROUPS, NSA_DK
    heads = d // dk
    hg = heads // g
    tm = _tile(seq, 1024)
    kv = _matmul(h_kv, w_kv[None], 0, col0=0, ncols=6 * g * dk, tm=tm, tn=_tile(6 * g * dk, 512),
                 out_dtype=BF16, epilogue=_ep_store, name="nsa_proj_kv")
    (k_pe, k_w1, k_w2, v_pe, v_w1, v_w2) = cmp_params
    kvc = _compress(kv, batch, seq, (k_pe, v_pe), (k_w1, v_w1), (k_w2, v_w2))
    q = _matmul(h_q, w_q, layer, col0=0, ncols=heads * dk, tm=tm, tn=_tile(heads * dk, 512),
                out_dtype=BF16, epilogue=functools.partial(_ep_scale, scale=dk ** -0.5 * LOG2E),
                name="nsa_proj_q")
    wg = w_q[layer, :, heads * dk:].reshape(d, g, hg * 3)
    wg = jnp.pad(wg, ((0, 0), (0, 0), (0, LANES - hg * 3))).reshape(1, d, g * LANES)
    gates = _matmul(h_q, wg, 0, col0=0, ncols=g * LANES, tm=tm, tn=g * LANES, out_dtype=F32,
                    epilogue=_ep_sigmoid, name="nsa_proj_gate")
    o = _nsa_attention(q, gates, kv, kvc, rel_bias, batch, seq, heads)
    tn_o = _tile(d, 512)
    return _matmul(o, w_o, layer, col0=0, ncols=d, tm=tm, tn=tn_o, out_dtype=F32, epilogue=_ep_residual,
                   extra=((x2, (tm, tn_o), lambda j, i: (i, j)),), name="nsa_out_proj")


def kernel(x, ret_norm_g, ret_w_in, ret_w_o, kv_norm_g, nsa_w_kv, cmp_k_pe, cmp_k_w1, cmp_k_w2, cmp_v_pe, cmp_v_w1, cmp_v_w2, rel_bias, nsa_norm_g, nsa_w_q, nsa_w_o, moe_norm_g, moe_w_router_group, moe_w_router_expert, moe_w_gate, moe_w_up, moe_w_down, final_norm_g):
    batch, seq, d = x.shape
    depth = moe_norm_g.shape[0]
    n_a = ret_norm_g.shape[0]
    x2 = x.reshape(batch * seq, d)
    cmp_params = (cmp_k_pe, cmp_k_w1, cmp_k_w2, cmp_v_pe, cmp_v_w1, cmp_v_w2)
    h_kv = h_q = None
    out = None
    for layer in range(depth):
        if layer < n_a:
            x2 = _retention_layer(x2, ret_norm_g[layer], ret_w_in, ret_w_o, layer, batch, seq)
        else:
            j = layer - n_a
            if h_q is None:
                h_kv = _rmsnorm(x2, kv_norm_g)
                h_q = _rmsnorm(x2, nsa_norm_g[j])
            x2 = _nsa_layer(x2, h_kv, h_q, nsa_w_kv, cmp_params, rel_bias, nsa_w_q, nsa_w_o, j,
                            batch, seq)
        hm, route = _route(x2, moe_norm_g[layer], moe_w_router_group[layer], moe_w_router_expert[layer])
        pairs = _moe_ffn(hm, route, moe_w_gate, moe_w_up, moe_w_down, layer, tb=MOE_ROWS)
        if layer == depth - 1:
            (out,) = _combine(x2, pairs, [final_norm_g], [F32], want_x=False)
        elif layer + 1 == n_a:
            j = layer + 1 - n_a
            x2, h_kv, h_q = _combine(x2, pairs, [kv_norm_g, nsa_norm_g[j]], [BF16, BF16], want_x=True)
        elif layer + 1 < n_a:
            (x2,) = _combine(x2, pairs, [], [], want_x=True)
        else:
            j = layer + 1 - n_a
            x2, h_q = _combine(x2, pairs, [nsa_norm_g[j]], [BF16], want_x=True)
    return out.reshape(batch, seq, d)
```

```python
import functools
import math

import numpy as np
import jax
import jax.numpy as jnp
from jax import lax
from jax.experimental import pallas as pl
from jax.experimental.pallas import tpu as pltpu

RMS_EPS = 1e-6
RET_DK = 256
RET_DV = 2 * RET_DK
ROPE_BASE = 10000.0
NSA_DK = 128
NSA_GROUPS = 4
CMP_LEN = 32
CMP_STRIDE = 16
SEL_BLOCK = 64
SEL_TOPK = 16
WINDOW = 512
FORCED_SCORE = 1e4
INVALID_SCORE = -1e9
REL_BUCKETS = 32
REL_MAX_DIST = 128
MOE_GROUPS = 8
MOE_EPG = 8
MOE_EXPERTS = MOE_GROUPS * MOE_EPG
MOE_TOPK = 2
MOE_ROWS = 256

LANES = 128
V7X_VMEM_LIMIT = 56 * 1024 * 1024
NEG = -1e30
LOG2E = math.log2(math.e)

BF16 = jnp.bfloat16
F32 = jnp.float32


def _tile(dim, pref):
    t = min(dim, pref)
    while dim % t:
        t //= 2
    return t


def _cparams(sem, vmem=None):
    return pltpu.CompilerParams(dimension_semantics=sem, vmem_limit_bytes=vmem)


def _norm_kernel(x_ref, g_ref, o_ref):
    x = x_ref[...]
    y = x * lax.rsqrt(jnp.mean(x * x, axis=-1, keepdims=True) + RMS_EPS)
    o_ref[...] = (y * g_ref[...]).astype(o_ref.dtype)


def _rmsnorm(x2, g, out_dtype=BF16):
    m, d = x2.shape
    tm = _tile(m, 256)
    return pl.pallas_call(
        _norm_kernel,
        grid=(m // tm,),
        in_specs=[pl.BlockSpec((tm, d), lambda i: (i, 0)),
                  pl.BlockSpec((1, d), lambda i: (0, 0))],
        out_specs=pl.BlockSpec((tm, d), lambda i: (i, 0)),
        out_shape=jax.ShapeDtypeStruct((m, d), out_dtype),
        compiler_params=_cparams(("parallel",), V7X_VMEM_LIMIT),
        name="rmsnorm",
    )(x2, g.reshape(1, d))


MM_CHUNK = 256


def _mm_kernel(a_ref, w_ref, *rest, n_extra, epilogue, cast_w):
    extra = rest[:n_extra]
    o_ref = rest[n_extra]
    if cast_w:
        wbf_ref = rest[n_extra + 1]

        @pl.when(pl.program_id(1) == 0)
        def _():
            wbf_ref[...] = w_ref[...].astype(BF16)
    else:
        wbf_ref = w_ref

    a = a_ref[...]
    tn = o_ref.shape[1]
    chunk = min(tn, MM_CHUNK)
    for c0 in range(0, tn, chunk):
        cols = slice(c0, c0 + chunk)
        acc = jnp.dot(a, wbf_ref[:, cols], preferred_element_type=F32)
        epilogue(acc, o_ref, cols, *extra)


def _matmul(a, w, layer, *, col0, ncols, tm, tn, out_dtype, epilogue, extra=(), name):
    m, k = a.shape
    assert col0 % tn == 0 and ncols % tn == 0 and m % tm == 0
    joff = col0 // tn
    cast_w = w.dtype != BF16
    in_specs = [pl.BlockSpec((tm, k), lambda j, i: (i, 0)),
                pl.BlockSpec((None, k, tn), lambda j, i: (layer, 0, j + joff))]
    args = [a, w]
    for arr, bshape, imap in extra:
        in_specs.append(pl.BlockSpec(bshape, imap))
        args.append(arr)
    return pl.pallas_call(
        functools.partial(_mm_kernel, n_extra=len(extra), epilogue=epilogue, cast_w=cast_w),
        grid=(ncols // tn, m // tm),
        in_specs=in_specs,
        out_specs=pl.BlockSpec((tm, tn), lambda j, i: (i, j)),
        out_shape=jax.ShapeDtypeStruct((m, ncols), out_dtype),
        scratch_shapes=[pltpu.VMEM((k, tn), BF16)] if cast_w else [],
        compiler_params=_cparams(("arbitrary", "arbitrary"), V7X_VMEM_LIMIT),
        name=name,
    )(*args)


def _cast_kernel(w_ref, o_ref):
    o_ref[...] = w_ref[...].astype(o_ref.dtype)


def _cast_bf16(w, layer):
    _, k, n = w.shape
    tk = _tile(k, 512)
    return pl.pallas_call(
        _cast_kernel,
        grid=(k // tk,),
        in_specs=[pl.BlockSpec((None, tk, n), lambda i: (layer, i, 0))],
        out_specs=pl.BlockSpec((None, tk, n), lambda i: (0, i, 0)),
        out_shape=jax.ShapeDtypeStruct((1, k, n), BF16),
        compiler_params=_cparams(("parallel",), V7X_VMEM_LIMIT),
        name="cast_bf16",
    )(w)


def _ep_store(acc, o_ref, cols):
    o_ref[:, cols] = acc.astype(o_ref.dtype)


def _ep_residual(acc, o_ref, cols, r_ref):
    o_ref[:, cols] = (r_ref[:, cols] + acc).astype(o_ref.dtype)


def _ep_scale(acc, o_ref, cols, *, scale):
    o_ref[:, cols] = (acc * scale).astype(o_ref.dtype)


def _ep_sigmoid(acc, o_ref, cols):
    o_ref[:, cols] = jax.nn.sigmoid(acc).astype(o_ref.dtype)


def _ep_rotary(acc, o_ref, cols, cos_ref, sin_ref, *, k_first_tile, k_scale):
    assert acc.shape[1] == RET_DK
    scale = jnp.where(pl.program_id(0) >= k_first_tile, k_scale, 1.0).astype(F32)
    cos = cos_ref[...] * scale
    sin = sin_ref[...] * scale
    half = RET_DK // 2
    x1 = acc[:, :half]
    x2 = acc[:, half:]
    o_ref[:, cols.start:cols.start + half] = (x1 * cos - x2 * sin).astype(o_ref.dtype)
    o_ref[:, cols.start + half:cols.stop] = (x1 * sin + x2 * cos).astype(o_ref.dtype)


def _ret_kernel(dchunk_ref, q_ref, k_ref, v_ref, g_ref, dintra_ref, dq_ref, dk_ref, y_ref, state_ref, *, hp):
    c = pl.program_id(2)

    @pl.when(c == 0)
    def _():
        state_ref[...] = jnp.zeros_like(state_ref)

    for j in range(hp):
        h = pl.program_id(1) * hp + j
        q = q_ref[:, j * RET_DK:(j + 1) * RET_DK]
        k = k_ref[:, j * RET_DK:(j + 1) * RET_DK]
        v = v_ref[:, j * RET_DV:(j + 1) * RET_DV]
        scores = lax.dot_general(q, k, (((1,), (1,)), ((), ())), preferred_element_type=F32)
        scores = scores * dintra_ref[j]
        intra = jnp.dot(scores.astype(BF16), v, preferred_element_type=F32)
        cross = jnp.dot(q, state_ref[j].astype(BF16), preferred_element_type=F32)
        dq = dq_ref[j]
        out = jnp.concatenate(
            [intra[:, n:n + LANES] + cross[:, n:n + LANES] * dq for n in range(0, RET_DV, LANES)], axis=1)
        dkc = dk_ref[j]
        kd = (k.astype(F32) * jnp.concatenate([dkc] * (RET_DK // LANES), axis=1)).astype(BF16)
        state_ref[j] = dchunk_ref[h] * state_ref[j] + lax.dot_general(
            kd, v, (((0,), (0,)), ((), ())), preferred_element_type=F32)
        out = out * lax.rsqrt(jnp.mean(out * out, axis=-1, keepdims=True) + RMS_EPS)
        gate = g_ref[:, j * RET_DV:(j + 1) * RET_DV].astype(F32)
        y_ref[:, j * RET_DV:(j + 1) * RET_DV] = (jax.nn.silu(gate) * out).astype(y_ref.dtype)


def _retention_core(qk, vg, batch, seq, heads):
    m = batch * seq
    c = _tile(seq, 256)
    nc = seq // c
    log_gamma = jnp.log1p(-jnp.exp2(-5.0 - jnp.arange(heads, dtype=F32)))
    idx = jnp.arange(c, dtype=F32)
    diff = idx[:, None] - idx[None, :]
    dintra = jnp.where(diff >= 0, jnp.exp(log_gamma[:, None, None] * jnp.maximum(diff, 0.0)), 0.0)
    dq = jnp.broadcast_to(jnp.exp(log_gamma[:, None] * (idx + 1.0))[:, :, None], (heads, c, LANES))
    dk = jnp.broadcast_to(jnp.exp(log_gamma[:, None] * (c - 1.0 - idx))[:, :, None], (heads, c, LANES))
    dchunk = jnp.exp(log_gamma * c)
    hp = next(n for n in (4, 2, 1) if heads % n == 0)
    ng = heads // hp
    grid_spec = pltpu.PrefetchScalarGridSpec(
        num_scalar_prefetch=1,
        grid=(batch, ng, nc),
        in_specs=[
            pl.BlockSpec((c, hp * RET_DK), lambda b, h, i, s: (b * nc + i, h)),
            pl.BlockSpec((c, hp * RET_DK), lambda b, h, i, s: (b * nc + i, ng + h)),
            pl.BlockSpec((c, hp * RET_DV), lambda b, h, i, s: (b * nc + i, h)),
            pl.BlockSpec((c, hp * RET_DV), lambda b, h, i, s: (b * nc + i, ng + h)),
            pl.BlockSpec((hp, c, c), lambda b, h, i, s: (h, 0, 0)),
            pl.BlockSpec((hp, c, LANES), lambda b, h, i, s: (h, 0, 0)),
            pl.BlockSpec((hp, c, LANES), lambda b, h, i, s: (h, 0, 0)),
        ],
        out_specs=pl.BlockSpec((c, hp * RET_DV), lambda b, h, i, s: (b * nc + i, h)),
        scratch_shapes=[pltpu.VMEM((hp, RET_DK, RET_DV), F32)],
    )
    return pl.pallas_call(
        functools.partial(_ret_kernel, hp=hp),
        grid_spec=grid_spec,
        out_shape=jax.ShapeDtypeStruct((m, heads * RET_DV), BF16),
        compiler_params=_cparams(("parallel", "parallel", "arbitrary"), V7X_VMEM_LIMIT),
        name="retention_core",
    )(dchunk, qk, qk, vg, vg, dintra, dq, dk)


def _retention_layer(x2, norm_g, w_in, w_o, layer, batch, seq):
    m, d = x2.shape
    heads = d // RET_DK
    hq = heads * RET_DK
    h = _rmsnorm(x2, norm_g)
    pos = jnp.arange(seq, dtype=F32)
    half = RET_DK // 2
    inv = 1.0 / (ROPE_BASE ** (jnp.arange(half, dtype=F32) / half))
    ang = pos[:, None] * inv[None, :]
    cos, sin = jnp.cos(ang), jnp.sin(ang)
    tm = _tile(seq, 1024)
    tn = _tile(hq, 512)
    nrow = seq // tm
    rot = functools.partial(_ep_rotary, k_first_tile=hq // tn, k_scale=RET_DK ** -0.5)
    qk = _matmul(h, w_in, layer, col0=0, ncols=2 * hq, tm=tm, tn=tn, out_dtype=BF16, epilogue=rot,
                 extra=((cos, (tm, half), lambda j, i: (i % nrow, 0)),
                        (sin, (tm, half), lambda j, i: (i % nrow, 0))),
                 name="ret_proj_qk")
    vg = _matmul(h, w_in, layer, col0=2 * hq, ncols=2 * heads * RET_DV, tm=tm, tn=tn, out_dtype=BF16,
                 epilogue=_ep_store, name="ret_proj_vg")
    y = _retention_core(qk, vg, batch, seq, heads)
    tm_o = _tile(seq, 512)
    tn_o = _tile(d, 512)
    return _matmul(y, _cast_bf16(w_o, layer), 0, col0=0, ncols=d, tm=tm_o, tn=tn_o, out_dtype=F32,
                   epilogue=_ep_residual, extra=((x2, (tm_o, tn_o), lambda j, i: (i, j)),),
                   name="ret_out_proj")


def _pack_bf16_pairs(lo, hi):
    lo_bits = pltpu.bitcast(lo.astype(BF16).astype(F32), jnp.uint32)
    hi_bits = pltpu.bitcast(hi.astype(BF16).astype(F32), jnp.uint32)
    return hi_bits | lax.shift_right_logical(lo_bits, jnp.uint32(16))


def _unpack_bf16_pairs(words):
    lo = pltpu.bitcast(lax.shift_left(words, jnp.uint32(16)), F32)
    hi = pltpu.bitcast(words & jnp.uint32(0xFFFF0000), F32)
    return lo, hi


def _route_kernel(x_ref, g_ref, wcat_ref, h_ref, r_ref):
    x = x_ref[...]
    hn = x * lax.rsqrt(jnp.mean(x * x, axis=-1, keepdims=True) + RMS_EPS) * g_ref[...]
    half = hn.shape[1] // 2
    h_ref[...] = _pack_bf16_pairs(hn[:, :half], hn[:, half:])
    hn_hi = hn.astype(BF16)
    hn_lo = (hn - hn_hi.astype(F32)).astype(BF16)
    both = jnp.dot(hn_hi, wcat_ref[...], preferred_element_type=F32)
    logits = (both[:, :LANES] + both[:, LANES:]
              + jnp.dot(hn_lo, wcat_ref[:, :LANES], preferred_element_type=F32))
    lane = lax.broadcasted_iota(jnp.int32, logits.shape, 1)
    is_g = lane < MOE_GROUPS
    lg = jnp.where(is_g, logits, -jnp.inf)
    mg = jnp.max(lg, axis=-1, keepdims=True)
    grp = jnp.min(jnp.where(lg == mg, lane, LANES), axis=-1, keepdims=True)
    p_grp = 1.0 / jnp.sum(jnp.where(is_g, jnp.exp(lg - mg), 0.0), axis=-1, keepdims=True)
    e_lane = lane - MOE_GROUPS
    is_e = (e_lane >= 0) & (e_lane < MOE_EXPERTS) & ((e_lane >> 3) == grp)
    le = jnp.where(is_e, logits, -jnp.inf)
    m1 = jnp.max(le, axis=-1, keepdims=True)
    i1 = jnp.min(jnp.where(le == m1, lane, LANES), axis=-1, keepdims=True)
    le2 = jnp.where(lane == i1, -jnp.inf, le)
    m2 = jnp.max(le2, axis=-1, keepdims=True)
    i2 = jnp.min(jnp.where(le2 == m2, lane, LANES), axis=-1, keepdims=True)
    t = jnp.exp(m2 - m1)
    w1 = p_grp / (1.0 + t)
    w2 = p_grp * t / (1.0 + t)
    e1 = (i1 - MOE_GROUPS).astype(F32)
    e2 = (i2 - MOE_GROUPS).astype(F32)
    r_ref[...] = jnp.where(lane == 0, e1, jnp.where(lane == 1, e2,
                           jnp.where(lane == 2, w1, jnp.where(lane == 3, w2, 0.0))))


def _route(x2, norm_g, w_rg, w_re):
    m, d = x2.shape
    tm = _tile(m, 256)
    wr = jnp.concatenate(
        [w_rg, w_re, jnp.zeros((d, LANES - MOE_GROUPS - MOE_EXPERTS), F32)], axis=1)
    wr_hi = wr.astype(BF16)
    wr_lo = (wr - wr_hi.astype(F32)).astype(BF16)
    wr_cat = jnp.concatenate([wr_hi, wr_lo], axis=1)
    return pl.pallas_call(
        _route_kernel,
        grid=(m // tm,),
        in_specs=[pl.BlockSpec((tm, d), lambda i: (i, 0)),
                  pl.BlockSpec((1, d), lambda i: (0, 0)),
                  pl.BlockSpec((d, 2 * LANES), lambda i: (0, 0))],
        out_specs=[pl.BlockSpec((tm, d // 2), lambda i: (i, 0)),
                   pl.BlockSpec((tm, LANES), lambda i: (i, 0))],
        out_shape=[jax.ShapeDtypeStruct((m, d // 2), jnp.uint32),
                   jax.ShapeDtypeStruct((m, LANES), F32)],
        compiler_params=_cparams(("parallel",), V7X_VMEM_LIMIT),
        name="moe_route",
    )(x2, norm_g.reshape(1, d), wr_cat)


def _ffn_kernel(blk_e_ref, nvalid_ref, nused_ref,
                src_cur_ref, src_nxt_ref, dst_ref, wg_ref, wu_ref, wd_ref, h_hbm,
                pairs_hbm,
                wgb_ref, wub_ref, wdb_ref, xbuf_ref, ybuf_ref, gsem, ssem, *, tb):
    i = pl.program_id(0)
    n_used = nused_ref[0]
    slot = i % 2

    def gather_copy(src_ref, r, s):
        return pltpu.make_async_copy(h_hbm.at[pl.ds(src_ref[0, 0, r], 1)],
                                     xbuf_ref.at[s, pl.ds(r, 1)], gsem.at[s])

    def scatter_copy(d_row, r, s):
        return pltpu.make_async_copy(ybuf_ref.at[s, pl.ds(r, 1)],
                                     pairs_hbm.at[pl.ds(d_row, 1)], ssem.at[s])

    def start_gather(src_ref, s):
        for r in range(tb):
            gather_copy(src_ref, r, s).start(priority=1)

    def wait_gather(s):
        pltpu.make_async_copy(h_hbm.at[pl.ds(0, tb)], xbuf_ref.at[s], gsem.at[s]).wait()

    def start_scatter(s, n):
        @pl.when(n == tb)
        def _():
            for r in range(tb):
                scatter_copy(dst_ref[0, 0, r], r, s).start(priority=r % 2)

        @pl.when(n != tb)
        def _():
            def body(r, carry):
                scatter_copy(dst_ref[0, 0, r], r, s).start()
                return carry
            lax.fori_loop(0, n, body, 0)

    def wait_scatter(s, n):
        @pl.when(n == tb)
        def _():
            pltpu.make_async_copy(ybuf_ref.at[s], pairs_hbm.at[pl.ds(0, tb)], ssem.at[s]).wait()

        @pl.when(n != tb)
        def _():
            def body(r, carry):
                scatter_copy(0, r, s).wait()
                return carry
            lax.fori_loop(0, n, body, 0)

    @pl.when(i == 0)
    def _():
        start_gather(src_cur_ref, 0)

    @pl.when(i < n_used)
    def _():
        @pl.when(i + 1 < n_used)
        def _():
            start_gather(src_nxt_ref, 1 - slot)

        first = jnp.logical_or(i == 0, blk_e_ref[i] != blk_e_ref[jnp.maximum(i - 1, 0)])

        @pl.when(first)
        def _():
            wgb_ref[...] = wg_ref[...].astype(BF16)
            wub_ref[...] = wu_ref[...].astype(BF16)
            wdb_ref[...] = wd_ref[...].astype(BF16)

        wait_gather(slot)
        x_lo, x_hi = _unpack_bf16_pairs(xbuf_ref[slot])
        xb = jnp.concatenate([x_lo, x_hi], axis=1).astype(BF16)
        hid = (jax.nn.silu(jnp.dot(xb, wgb_ref[...], preferred_element_type=F32))
               * jnp.dot(xb, wub_ref[...], preferred_element_type=F32))
        y = jnp.dot(hid.astype(BF16), wdb_ref[...], preferred_element_type=F32)

        @pl.when(i >= 2)
        def _():
            wait_scatter(slot, nvalid_ref[jnp.maximum(i - 2, 0)])

        half = y.shape[1] // 2
        ybuf_ref[slot] = _pack_bf16_pairs(y[:, :half], y[:, half:])
        nv = nvalid_ref[i]
        start_scatter(slot, nv)

        @pl.when(i == n_used - 1)
        def _():
            @pl.when(i >= 1)
            def _():
                wait_scatter(1 - slot, nvalid_ref[jnp.maximum(i - 1, 0)])
            wait_scatter(slot, nv)


def _moe_ffn(h, route, w_gate, w_up, w_down, layer, tb):
    n = h.shape[0]
    _, e, d, f = w_gate.shape
    k = MOE_TOPK
    mcopies = n * k
    flat_e = route[:, 0:2].astype(jnp.int32).reshape(mcopies)
    order = jnp.argsort(flat_e).astype(jnp.int32)
    counts = jnp.sum((flat_e[:, None] == jnp.arange(e)[None, :]).astype(jnp.int32), axis=0)
    padded = (counts + tb - 1) // tb * tb
    pad_end = jnp.cumsum(padded)
    pad_start = pad_end - padded
    start = jnp.cumsum(counts) - counts
    nb = (mcopies + e * (tb - 1) + tb - 1) // tb
    p = nb * tb
    blk_start = jnp.arange(nb) * tb
    blk_e = jnp.minimum(jnp.sum((blk_start[:, None] >= pad_end[None, :]).astype(jnp.int32), axis=1),
                        e - 1).astype(jnp.int32)
    blk_off = blk_start - pad_start[blk_e]
    nvalid = jnp.clip(counts[blk_e] - blk_off, 0, tb).astype(jnp.int32)
    n_used = (pad_end[-1] // tb).astype(jnp.int32).reshape(1)
    rows = jnp.arange(tb)[None, :]
    valid = rows < nvalid[:, None]
    sorted_idx = jnp.clip((start[blk_e] + blk_off)[:, None] + rows, 0, mcopies - 1)
    copy_id = order[sorted_idx]
    tok = copy_id // k
    src3 = jnp.where(valid, tok, 0).reshape(nb, 1, tb)
    dst3 = jnp.where(valid, (copy_id % k) * n + tok, 0).reshape(nb, 1, tb)

    smem = functools.partial(pl.BlockSpec, memory_space=pltpu.SMEM)
    grid_spec = pltpu.PrefetchScalarGridSpec(
        num_scalar_prefetch=3,
        grid=(nb,),
        in_specs=[
            smem((1, 1, tb), lambda i, be, nv, nu: (i, 0, 0)),
            smem((1, 1, tb), lambda i, be, nv, nu: (jnp.minimum(i + 1, nb - 1), 0, 0)),
            smem((1, 1, tb), lambda i, be, nv, nu: (i, 0, 0)),
            pl.BlockSpec((None, None, d, f), lambda i, be, nv, nu: (layer, be[i], 0, 0)),
            pl.BlockSpec((None, None, d, f), lambda i, be, nv, nu: (layer, be[i], 0, 0)),
            pl.BlockSpec((None, None, f, d), lambda i, be, nv, nu: (layer, be[i], 0, 0)),
            pl.BlockSpec(memory_space=pl.ANY),
        ],
        out_specs=pl.BlockSpec(memory_space=pl.ANY),
        scratch_shapes=[
            pltpu.VMEM((d, f), BF16), pltpu.VMEM((d, f), BF16), pltpu.VMEM((f, d), BF16),
            pltpu.VMEM((2, tb, d // 2), jnp.uint32), pltpu.VMEM((2, tb, d // 2), jnp.uint32),
            pltpu.SemaphoreType.DMA((2,)), pltpu.SemaphoreType.DMA((2,)),
        ],
    )
    return pl.pallas_call(
        functools.partial(_ffn_kernel, tb=tb),
        grid_spec=grid_spec,
        out_shape=jax.ShapeDtypeStruct((k * n, d // 2), jnp.uint32),
        compiler_params=_cparams(("arbitrary",), V7X_VMEM_LIMIT),
        name="moe_ffn",
    )(blk_e, nvalid, n_used, src3, src3, dst3, w_gate, w_up, w_down, h)


def _combine_kernel(x_ref, r_ref, p0_ref, p1_ref, *rest, n_norm, want_x):
    g_refs = rest[:n_norm]
    outs = rest[n_norm:]
    w0 = r_ref[:, 2:3]
    w1 = r_ref[:, 3:4]
    lo0, hi0 = _unpack_bf16_pairs(p0_ref[...])
    lo1, hi1 = _unpack_bf16_pairs(p1_ref[...])
    x = x_ref[...] + jnp.concatenate([lo0 * w0 + lo1 * w1, hi0 * w0 + hi1 * w1], axis=1)
    oi = 0
    if want_x:
        outs[0][...] = x
        oi = 1
    r = lax.rsqrt(jnp.mean(x * x, axis=-1, keepdims=True) + RMS_EPS)
    y = x * r
    for g_ref, o_ref in zip(g_refs, outs[oi:]):
        o_ref[...] = (y * g_ref[...]).astype(o_ref.dtype)


def _combine(x2, route, pairs, gains, out_dtypes, want_x):
    m, d = x2.shape
    tm = _tile(m, 256)
    nrow = m // tm
    in_specs = [pl.BlockSpec((tm, d), lambda i: (i, 0)),
                pl.BlockSpec((tm, LANES), lambda i: (i, 0)),
                pl.BlockSpec((tm, d // 2), lambda i: (i, 0)),
                pl.BlockSpec((tm, d // 2), lambda i: (i + nrow, 0))]
    in_specs += [pl.BlockSpec((1, d), lambda i: (0, 0)) for _ in gains]
    out_shape = ([jax.ShapeDtypeStruct((m, d), F32)] if want_x else []) + [
        jax.ShapeDtypeStruct((m, d), dt) for dt in out_dtypes]
    out_specs = [pl.BlockSpec((tm, d), lambda i: (i, 0)) for _ in out_shape]
    return pl.pallas_call(
        functools.partial(_combine_kernel, n_norm=len(gains), want_x=want_x),
        grid=(nrow,),
        in_specs=in_specs,
        out_specs=out_specs,
        out_shape=out_shape,
        compiler_params=_cparams(("parallel",), V7X_VMEM_LIMIT),
        name="moe_combine",
    )(x2, route, pairs, pairs, *[g.reshape(1, d) for g in gains])


def _t5_bucket(dist):
    n = jnp.maximum(dist, 0)
    exact = REL_BUCKETS // 2
    nf = jnp.maximum(n, 1).astype(F32)
    large = exact + (jnp.log(nf / exact) / math.log(REL_MAX_DIST / exact)
                     * (REL_BUCKETS - exact)).astype(jnp.int32)
    return jnp.where(n < exact, n, jnp.minimum(large, REL_BUCKETS - 1))


def _bias_lookup(table, bucket):
    onehot = jax.nn.one_hot(bucket, REL_BUCKETS, dtype=F32)
    return jnp.einsum('...b,bh->...h', onehot, table, precision=lax.Precision.HIGHEST)


def _cmp_kernel(rk_ref, rv_ref, pe_ref, w1_ref, w2_ref, o_ref):
    half = w1_ref.shape[1] // 2
    for s, r_ref in enumerate((rk_ref, rv_ref)):
        r = r_ref[0, 0].astype(F32)
        ra = (r + pe_ref[s, :, :half]).astype(BF16)
        rb = (r + pe_ref[s, :, half:]).astype(BF16)
        ha = jnp.dot(ra, w1_ref[s, :half, :], preferred_element_type=F32)
        hb = jnp.dot(rb, w1_ref[s, half:, :], preferred_element_type=F32)
        hb_next = jnp.concatenate([hb[1:], hb[:1]], axis=0)
        hid = jax.nn.gelu(ha + hb_next)
        o_ref[0, 0, s] = jnp.dot(hid.astype(BF16), w2_ref[s], preferred_element_type=F32).astype(o_ref.dtype)


def _compress(kv, batch, seq, pes, w1s, w2s):
    g, dk = NSA_GROUPS, NSA_DK
    nr = seq // CMP_STRIDE
    assert nr == dk
    ct = kv[:, :2 * g * dk].reshape(batch, nr, CMP_STRIDE, 2, g, dk)
    r = ct.transpose(3, 0, 4, 1, 2, 5).reshape(2, batch, g, nr, CMP_STRIDE * dk)
    pe = jnp.stack(pes).reshape(2, 1, CMP_LEN * dk)
    w1 = jnp.stack(w1s).reshape(2, CMP_LEN * dk, dk).astype(BF16)
    w2 = jnp.stack(w2s).astype(BF16)
    rspec = pl.BlockSpec((1, 1, nr, CMP_STRIDE * dk), lambda b, gi: (b, gi, 0, 0))
    return pl.pallas_call(
        _cmp_kernel,
        grid=(batch, g),
        in_specs=[rspec, rspec,
                  pl.BlockSpec((2, 1, CMP_LEN * dk), lambda b, gi: (0, 0, 0)),
                  pl.BlockSpec((2, CMP_LEN * dk, dk), lambda b, gi: (0, 0, 0)),
                  pl.BlockSpec((2, dk, dk), lambda b, gi: (0, 0, 0))],
        out_specs=pl.BlockSpec((1, 1, 2, nr, dk), lambda b, gi: (b, gi, 0, 0, 0)),
        out_shape=jax.ShapeDtypeStruct((batch, g, 2, nr, dk), BF16),
        compiler_params=_cparams(("parallel", "parallel"), V7X_VMEM_LIMIT),
        name="nsa_compress",
    )(r[0], r[1], pe, w1, w2)


BIAS_DIAG = 0
BIAS_NEXT = 1
BIAS_CONST = 2
BIAS_WIN_EDGE = 3
BIAS_MASKED = 4
N_BIAS_TILES = 5
CMP_PAD = 16
CMP_BAND = CMP_PAD + LANES // CMP_STRIDE
FAR_GROUP = 4


def _nsa_kernel(q_ref, gate_ref, kvc_ref, ks_ref, vs_ref, kw_ref, vw_ref, biasc_ref, bias_ref,
                ovt_ref, ext_ref, o_ref,
                selpen_ref, m_ref, l_ref, acc_ref, sfar_ref, snear_ref, oc_ref, ow_ref, scmp_ref,
                *, hg, tq, n_cmp, n_sblk):
    qb = pl.program_id(2)
    dk = NSA_DK
    nt_dims = (((1,), (1,)), ((), ()))
    q8 = jnp.concatenate([q_ref[:, h * dk:(h + 1) * dk] for h in range(hg)], axis=0)
    kj = lax.broadcasted_iota(jnp.int32, (LANES, tq), 0)
    qi = lax.broadcasted_iota(jnp.int32, (LANES, tq), 1)
    t_abs = qb * tq + qi

    def tile_h(a):
        return jnp.concatenate([a] * hg, axis=1)

    win_tiles = WINDOW // LANES
    n_near = win_tiles + 1
    lo = jnp.maximum(qb - win_tiles, 0)
    c_far = bias_ref[BIAS_CONST, 0:1, :]

    def scores(k_ref, row0, n, addend):
        return lax.dot_general(k_ref[pl.ds(row0, n), :], q8, nt_dims, preferred_element_type=F32) + addend

    def colmax(sc):
        return jnp.max(sc, axis=0, keepdims=True)

    def pv(v_ref, row0, n, p):
        return lax.dot_general(v_ref[pl.ds(row0, n), :], p.astype(BF16), (((0,), (0,)), ((), ())),
                               preferred_element_type=F32)

    near_row0, near_ok = [], []
    for j in range(n_near):
        kb = qb - (win_tiles - j)
        near_row0.append(pl.multiple_of(jnp.maximum(kb, 0) * LANES, LANES))
        near_ok.append(kb >= 0)

    def near_bias(j, window_branch):
        d = win_tiles - j
        tile = BIAS_DIAG if d == 0 else (BIAS_NEXT if d == 1 else BIAS_CONST)
        if window_branch and d == win_tiles:
            tile = BIAS_WIN_EDGE
        return bias_ref[jnp.where(near_ok[j], tile, BIAS_MASKED)]

    m_w = jnp.full(c_far.shape, 0.5 * NEG, F32)
    for j in range(n_near):
        s_w = scores(kw_ref, near_row0[j], LANES, near_bias(j, True))
        snear_ref[1, j] = s_w
        m_w = jnp.maximum(m_w, colmax(s_w))
    l_w = jnp.zeros(c_far.shape, F32)
    acc_w = jnp.zeros((dk, hg * tq), F32)
    for j in range(n_near):
        p_w = jnp.exp2(snear_ref[1, j] - m_w)
        l_w = l_w + jnp.sum(p_w, axis=0, keepdims=True)
        acc_w = acc_w + pv(vw_ref, near_row0[j], LANES, p_w)
    ow_ref[...] = acc_w * (1.0 / jnp.maximum(l_w, 1e-30))

    s = lax.dot_general(kvc_ref[0], q8, nt_dims, preferred_element_type=F32) + c_far
    scmp_ref[pl.ds(0, CMP_PAD), :] = jnp.zeros((CMP_PAD, hg * tq), F32)
    scmp_ref[pl.ds(CMP_PAD, LANES), :] = s
    band0 = pl.multiple_of(qb * (tq // CMP_STRIDE), 8)
    scmp_ref[pl.ds(band0, CMP_BAND), :] = scmp_ref[pl.ds(band0, CMP_BAND), :] + biasc_ref[...]
    s = scmp_ref[pl.ds(CMP_PAD, LANES), :]
    valid_c = (kj * CMP_STRIDE + (CMP_LEN - 1) <= t_abs) & (kj < n_cmp)
    s = s + tile_h(jnp.where(valid_c, 0.0, NEG))
    mx = jnp.maximum(jnp.max(s, axis=0, keepdims=True), 0.5 * NEG)
    p = jnp.exp2(s - mx)
    p = p * (1.0 / jnp.maximum(jnp.sum(p, axis=0, keepdims=True), 1e-30))
    oc_ref[...] = lax.dot_general(kvc_ref[1], p.astype(BF16), (((0,), (0,)), ((), ())),
                                  preferred_element_type=F32)
    psum = p[:, 0:tq]
    for h in range(1, hg):
        psum = psum + p[:, h * tq:(h + 1) * tq]
    p_hi = psum.astype(BF16)
    p_lo = (psum - p_hi.astype(F32)).astype(BF16)
    imp = (jnp.dot(ovt_ref[...], p_hi, preferred_element_type=F32)
           + jnp.dot(ovt_ref[...], p_lo, preferred_element_type=F32))

    imp = imp[:n_sblk]
    kb_i = lax.broadcasted_iota(jnp.int32, (n_sblk, tq), 0)
    t_b = qb * tq + lax.broadcasted_iota(jnp.int32, (n_sblk, tq), 1)
    cur = lax.shift_right_logical(t_b, int(math.log2(SEL_BLOCK)))
    valid_b = kb_i * SEL_BLOCK <= t_b
    forced = (kb_i == 0) | (kb_i == cur) | (kb_i == cur - 1)
    score = jnp.where(valid_b, jnp.where(forced, FORCED_SCORE, imp), INVALID_SCORE)
    rank = jnp.zeros((n_sblk, tq), F32)
    for j in range(n_sblk):
        cj = score[j:j + 1, :]
        ahead = (cj > score) | ((cj == score) & (kb_i > j))
        rank = rank + jnp.where(ahead, 1.0, 0.0)
    sel = jnp.where(rank < min(SEL_TOPK, n_sblk), 1.0, 0.0)
    if n_sblk < LANES:
        sel = jnp.concatenate([sel, jnp.zeros((LANES - n_sblk, tq), F32)], axis=0)
    selmask = jnp.dot(ext_ref[...], sel.astype(BF16), preferred_element_type=F32)
    selpen_ref[...] = ((selmask - 1.0) * (-NEG)).reshape(selpen_ref.shape)

    m_near = jnp.full(c_far.shape, 0.5 * NEG, F32)
    for j in range(n_near):
        kb_c = lax.shift_right_logical(near_row0[j], int(math.log2(LANES)))
        s_s = scores(ks_ref, near_row0[j], LANES, near_bias(j, False) + tile_h(selpen_ref[kb_c]))
        snear_ref[0, j] = s_s
        m_near = jnp.maximum(m_near, colmax(s_s))

    assert FAR_GROUP - 1 <= win_tiles
    n_grp = (lo + FAR_GROUP - 1) // FAR_GROUP
    m_ref[...] = jnp.full(m_ref.shape, 0.5 * NEG, F32)

    def far_scores(gi, carry):
        kb = FAR_GROUP * gi
        pen = jnp.concatenate(
            [selpen_ref[kb + t] + jnp.where(kb + t < lo, 0.0, NEG) for t in range(FAR_GROUP)], axis=0)
        sc = scores(ks_ref, pl.multiple_of(kb * LANES, LANES), FAR_GROUP * LANES, tile_h(pen))
        sfar_ref[gi] = sc
        m_ref[...] = jnp.maximum(m_ref[...], colmax(sc))
        return carry
    lax.fori_loop(0, n_grp, far_scores, 0)

    m_s = jnp.maximum(m_near, m_ref[...] + c_far)
    l_s = jnp.zeros(m_s.shape, F32)
    acc_s = jnp.zeros((dk, hg * tq), F32)
    for j in range(n_near):
        p_s = jnp.exp2(snear_ref[0, j] - m_s)
        l_s = l_s + jnp.sum(p_s, axis=0, keepdims=True)
        acc_s = acc_s + pv(vs_ref, near_row0[j], LANES, p_s)
    l_ref[...] = l_s
    acc_ref[...] = acc_s

    m_far = m_s - c_far

    def far_pv(gi, carry):
        row0 = pl.multiple_of(FAR_GROUP * gi * LANES, LANES)
        p = jnp.exp2(sfar_ref[gi] - m_far)
        l_ref[...] = l_ref[...] + jnp.sum(p, axis=0, keepdims=True)
        acc_ref[...] = acc_ref[...] + pv(vs_ref, row0, FAR_GROUP * LANES, p)
        return carry
    lax.fori_loop(0, n_grp, far_pv, 0)

    o_s = acc_ref[...] * (1.0 / jnp.maximum(l_ref[...], 1e-30))
    o_w = ow_ref[...]
    o_c = oc_ref[...]
    gt = gate_ref[...].T
    for h in range(hg):
        cols = slice(h * tq, (h + 1) * tq)
        out_t = (gt[3 * h:3 * h + 1] * o_c[:, cols] + gt[3 * h + 1:3 * h + 2] * o_s[:, cols]
                 + gt[3 * h + 2:3 * h + 3] * o_w[:, cols])
        o_ref[:, h * dk:(h + 1) * dk] = out_t.T.astype(o_ref.dtype)


def _nsa_attention(q, gates, kv, kvc, rel_bias, batch, seq, heads):
    g, dk = NSA_GROUPS, NSA_DK
    hg = heads // g
    tq = LANES
    nqb = seq // tq
    n_cmp = (seq - CMP_LEN) // CMP_STRIDE + 1
    n_sblk = seq // SEL_BLOCK
    nr = seq // CMP_STRIDE
    assert nr == LANES and n_sblk <= LANES and n_sblk % 8 == 0 and tq == LANES
    nt = seq // LANES
    rows = hg * tq
    table = rel_bias.astype(F32) * LOG2E
    assert (REL_MAX_DIST + CMP_LEN - 1 + CMP_STRIDE - 1) // CMP_STRIDE <= CMP_PAD and tq % CMP_STRIDE == 0
    band_w = jnp.arange(CMP_BAND)[:, None]
    dist_c = jnp.arange(tq)[None, :] - (band_w - CMP_PAD) * CMP_STRIDE - (CMP_LEN - 1)
    bias_c = _bias_lookup(table, _t5_bucket(dist_c)) - table[REL_BUCKETS - 1]
    bias_c = bias_c.reshape(CMP_BAND, tq, g, hg).transpose(2, 0, 3, 1).reshape(g, CMP_BAND, rows)
    ii = jnp.arange(tq)
    n_far = -(-(REL_MAX_DIST + tq - 1) // tq)
    dist = jnp.arange(n_far + 1)[:, None, None] * tq + ii[None, None, :] - ii[None, :, None]
    bias_t = _bias_lookup(table, _t5_bucket(dist))
    bias_t = bias_t.reshape(n_far + 1, tq, tq, g, hg).transpose(0, 3, 1, 4, 2).reshape(n_far + 1, g, tq, rows)
    assert n_far == BIAS_CONST and WINDOW % tq == 0 and WINDOW // tq >= n_far
    key_j = jnp.arange(tq)[:, None]
    qry_i = jnp.tile(jnp.arange(tq), hg)[None, :]
    future = key_j > qry_i
    bias_t = jnp.stack([
        jnp.where(future, NEG, bias_t[0]),
        bias_t[1],
        bias_t[2],
        jnp.where(future, bias_t[2], NEG),
        jnp.full_like(bias_t[2], NEG),
    ])
    cs = np.arange(LANES) * CMP_STRIDE
    ss = np.arange(LANES) * SEL_BLOCK
    ov = np.clip(np.minimum(cs[:, None] + CMP_LEN, ss[None, :] + SEL_BLOCK)
                 - np.maximum(cs[:, None], ss[None, :]), 0, None) / CMP_LEN
    ov[n_cmp:, :] = 0.0
    ov[:, n_sblk:] = 0.0
    overlap_t = jnp.asarray(ov.T, BF16)
    ex = ((np.arange(seq)[:, None] // SEL_BLOCK) == np.arange(LANES)[None, :]).astype(np.float32)
    expand_t = jnp.asarray(ex, BF16)

    kspec = lambda blk: pl.BlockSpec((seq, dk), lambda b, gi, qb: (b, blk * g + gi))
    return pl.pallas_call(
        functools.partial(_nsa_kernel, hg=hg, tq=tq, n_cmp=n_cmp, n_sblk=n_sblk),
        grid=(batch, g, nqb),
        in_specs=[
            pl.BlockSpec((tq, hg * dk), lambda b, gi, qb: (b * nqb + qb, gi)),
            pl.BlockSpec((tq, LANES), lambda b, gi, qb: (b * nqb + qb, gi)),
            pl.BlockSpec((None, None, 2, nr, dk), lambda b, gi, qb: (b, gi, 0, 0, 0)),
            kspec(2), kspec(3), kspec(4), kspec(5),
            pl.BlockSpec((None, CMP_BAND, rows), lambda b, gi, qb: (gi, 0, 0)),
            pl.BlockSpec((N_BIAS_TILES, None, tq, rows), lambda b, gi, qb: (0, gi, 0, 0)),
            pl.BlockSpec((LANES, LANES), lambda b, gi, qb: (0, 0)),
            pl.BlockSpec((seq, LANES), lambda b, gi, qb: (0, 0)),
        ],
        out_specs=pl.BlockSpec((tq, hg * dk), lambda b, gi, qb: (b * nqb + qb, gi)),
        out_shape=jax.ShapeDtypeStruct((batch * seq, heads * dk), BF16),
        scratch_shapes=[pltpu.VMEM((nt, LANES, tq), F32),
                        pltpu.VMEM((1, rows), F32), pltpu.VMEM((1, rows), F32),
                        pltpu.VMEM((dk, rows), F32),
                        pltpu.VMEM((-(-nt // FAR_GROUP), FAR_GROUP * LANES, rows), F32),
                        pltpu.VMEM((2, WINDOW // LANES + 1, LANES, rows), F32),
                        pltpu.VMEM((dk, rows), F32), pltpu.VMEM((dk, rows), F32),
                        pltpu.VMEM((CMP_PAD + LANES, rows), F32)],
        compiler_params=_cparams(("parallel", "parallel", "arbitrary"), V7X_VMEM_LIMIT),
        name="nsa_attention",
    )(q, gates, kvc, kv, kv, kv, kv, bias_c, bias_t, overlap_t, expand_t)


def _nsa_layer(x2, h_kv, h_q, w_kv, cmp_params, rel_bias, w_q, w_o, layer, batch, seq):
    m, d = x2.shape
    g, dk = NSA_GROUPS, NSA_DK
    heads = d // dk
    hg = heads // g
    tm = _tile(seq, 1024)
    kv = _matmul(h_kv, w_kv[None], 0, col0=0, ncols=6 * g * dk, tm=tm, tn=_tile(6 * g * dk, 512),
                 out_dtype=BF16, epilogue=_ep_store, name="nsa_proj_kv")
    (k_pe, k_w1, k_w2, v_pe, v_w1, v_w2) = cmp_params
    kvc = _compress(kv, batch, seq, (k_pe, v_pe), (k_w1, v_w1), (k_w2, v_w2))
    q = _matmul(h_q, w_q, layer, col0=0, ncols=heads * dk, tm=tm, tn=_tile(heads * dk, 512),
                out_dtype=BF16, epilogue=functools.partial(_ep_scale, scale=dk ** -0.5 * LOG2E),
                name="nsa_proj_q")
    wg = w_q[layer, :, heads * dk:].reshape(d, g, hg * 3)
    wg = jnp.pad(wg, ((0, 0), (0, 0), (0, LANES - hg * 3))).reshape(1, d, g * LANES)
    gates = _matmul(h_q, wg, 0, col0=0, ncols=g * LANES, tm=tm, tn=g * LANES, out_dtype=F32,
                    epilogue=_ep_sigmoid, name="nsa_proj_gate")
    o = _nsa_attention(q, gates, kv, kvc, rel_bias, batch, seq, heads)
    tn_o = _tile(d, 512)
    return _matmul(o, w_o, layer, col0=0, ncols=d, tm=tm, tn=tn_o, out_dtype=F32, epilogue=_ep_residual,
                   extra=((x2, (tm, tn_o), lambda j, i: (i, j)),), name="nsa_out_proj")


def kernel(x, ret_norm_g, ret_w_in, ret_w_o, kv_norm_g, nsa_w_kv, cmp_k_pe, cmp_k_w1, cmp_k_w2, cmp_v_pe, cmp_v_w1, cmp_v_w2, rel_bias, nsa_norm_g, nsa_w_q, nsa_w_o, moe_norm_g, moe_w_router_group, moe_w_router_expert, moe_w_gate, moe_w_up, moe_w_down, final_norm_g):
    batch, seq, d = x.shape
    depth = moe_norm_g.shape[0]
    n_a = ret_norm_g.shape[0]
    x2 = x.reshape(batch * seq, d)
    cmp_params = (cmp_k_pe, cmp_k_w1, cmp_k_w2, cmp_v_pe, cmp_v_w1, cmp_v_w2)
    h_kv = h_q = None
    out = None
    for layer in range(depth):
        if layer < n_a:
            x2 = _retention_layer(x2, ret_norm_g[layer], ret_w_in, ret_w_o, layer, batch, seq)
        else:
            j = layer - n_a
            if h_q is None:
                h_kv = _rmsnorm(x2, kv_norm_g)
                h_q = _rmsnorm(x2, nsa_norm_g[j])
            x2 = _nsa_layer(x2, h_kv, h_q, nsa_w_kv, cmp_params, rel_bias, nsa_w_q, nsa_w_o, j,
                            batch, seq)
        hm, route = _route(x2, moe_norm_g[layer], moe_w_router_group[layer], moe_w_router_expert[layer])
        pairs = _moe_ffn(hm, route, moe_w_gate, moe_w_up, moe_w_down, layer, tb=MOE_ROWS)
        if layer == depth - 1:
            (out,) = _combine(x2, route, pairs, [final_norm_g], [F32], want_x=False)
        elif layer + 1 == n_a:
            j = layer + 1 - n_a
            x2, h_kv, h_q = _combine(x2, route, pairs, [kv_norm_g, nsa_norm_g[j]], [BF16, BF16], want_x=True)
        elif layer + 1 < n_a:
            (x2,) = _combine(x2, route, pairs, [], [], want_x=True)
        else:
            j = layer + 1 - n_a
            x2, h_q = _combine(x2, route, pairs, [nsa_norm_g[j]], [BF16], want_x=True)
    return out.reshape(batch, seq, d)
```

```python
import functools
import math

import numpy as np
import jax
import jax.numpy as jnp
from jax import lax
from jax.experimental import pallas as pl
from jax.experimental.pallas import tpu as pltpu

RMS_EPS = 1e-6
RET_DK = 256
RET_DV = 2 * RET_DK
ROPE_BASE = 10000.0
NSA_DK = 128
NSA_GROUPS = 4
CMP_LEN = 32
CMP_STRIDE = 16
SEL_BLOCK = 64
SEL_TOPK = 16
WINDOW = 512
FORCED_SCORE = 1e4
INVALID_SCORE = -1e9
REL_BUCKETS = 32
REL_MAX_DIST = 128
MOE_GROUPS = 8
MOE_EPG = 8
MOE_EXPERTS = MOE_GROUPS * MOE_EPG
MOE_TOPK = 2
MOE_ROWS = 256

LANES = 128
V7X_VMEM_LIMIT = 56 * 1024 * 1024
NEG = -1e30
LOG2E = math.log2(math.e)

BF16 = jnp.bfloat16
F32 = jnp.float32


def _tile(dim, pref):
    t = min(dim, pref)
    while dim % t:
        t //= 2
    return t


def _cparams(sem, vmem=None):
    return pltpu.CompilerParams(dimension_semantics=sem, vmem_limit_bytes=vmem)


def _norm_kernel(x_ref, g_ref, o_ref):
    x = x_ref[...]
    y = x * lax.rsqrt(jnp.mean(x * x, axis=-1, keepdims=True) + RMS_EPS)
    o_ref[...] = (y * g_ref[...]).astype(o_ref.dtype)


def _rmsnorm(x2, g, out_dtype=BF16):
    m, d = x2.shape
    tm = _tile(m, 256)
    return pl.pallas_call(
        _norm_kernel,
        grid=(m // tm,),
        in_specs=[pl.BlockSpec((tm, d), lambda i: (i, 0)),
                  pl.BlockSpec((1, d), lambda i: (0, 0))],
        out_specs=pl.BlockSpec((tm, d), lambda i: (i, 0)),
        out_shape=jax.ShapeDtypeStruct((m, d), out_dtype),
        compiler_params=_cparams(("parallel",), V7X_VMEM_LIMIT),
        name="rmsnorm",
    )(x2, g.reshape(1, d))


MM_CHUNK = 256


def _mm_kernel(a_ref, w_ref, *rest, n_extra, epilogue, cast_w):
    extra = rest[:n_extra]
    o_ref = rest[n_extra]
    if cast_w:
        wbf_ref = rest[n_extra + 1]

        @pl.when(pl.program_id(1) == 0)
        def _():
            wbf_ref[...] = w_ref[...].astype(BF16)
    else:
        wbf_ref = w_ref

    a = a_ref[...]
    tn = o_ref.shape[1]
    chunk = min(tn, MM_CHUNK)
    for c0 in range(0, tn, chunk):
        cols = slice(c0, c0 + chunk)
        acc = jnp.dot(a, wbf_ref[:, cols], preferred_element_type=F32)
        epilogue(acc, o_ref, cols, *extra)


def _matmul(a, w, layer, *, col0, ncols, tm, tn, out_dtype, epilogue, extra=(), name):
    m, k = a.shape
    assert col0 % tn == 0 and ncols % tn == 0 and m % tm == 0
    joff = col0 // tn
    cast_w = w.dtype != BF16
    in_specs = [pl.BlockSpec((tm, k), lambda j, i: (i, 0)),
                pl.BlockSpec((None, k, tn), lambda j, i: (layer, 0, j + joff))]
    args = [a, w]
    for arr, bshape, imap in extra:
        in_specs.append(pl.BlockSpec(bshape, imap))
        args.append(arr)
    return pl.pallas_call(
        functools.partial(_mm_kernel, n_extra=len(extra), epilogue=epilogue, cast_w=cast_w),
        grid=(ncols // tn, m // tm),
        in_specs=in_specs,
        out_specs=pl.BlockSpec((tm, tn), lambda j, i: (i, j)),
        out_shape=jax.ShapeDtypeStruct((m, ncols), out_dtype),
        scratch_shapes=[pltpu.VMEM((k, tn), BF16)] if cast_w else [],
        compiler_params=_cparams(("arbitrary", "arbitrary"), V7X_VMEM_LIMIT),
        name=name,
    )(*args)


def _cast_kernel(w_ref, o_ref):
    o_ref[...] = w_ref[...].astype(o_ref.dtype)


def _cast_bf16(w, layer):
    _, k, n = w.shape
    tk = _tile(k, 512)
    return pl.pallas_call(
        _cast_kernel,
        grid=(k // tk,),
        in_specs=[pl.BlockSpec((None, tk, n), lambda i: (layer, i, 0))],
        out_specs=pl.BlockSpec((None, tk, n), lambda i: (0, i, 0)),
        out_shape=jax.ShapeDtypeStruct((1, k, n), BF16),
        compiler_params=_cparams(("parallel",), V7X_VMEM_LIMIT),
        name="cast_bf16",
    )(w)


def _ep_store(acc, o_ref, cols):
    o_ref[:, cols] = acc.astype(o_ref.dtype)


def _ep_residual(acc, o_ref, cols, r_ref):
    o_ref[:, cols] = (r_ref[:, cols] + acc).astype(o_ref.dtype)


def _ep_scale(acc, o_ref, cols, *, scale):
    o_ref[:, cols] = (acc * scale).astype(o_ref.dtype)


def _ep_sigmoid(acc, o_ref, cols):
    o_ref[:, cols] = jax.nn.sigmoid(acc).astype(o_ref.dtype)


def _ep_rotary(acc, o_ref, cols, cos_ref, sin_ref, *, k_first_tile, k_scale):
    assert acc.shape[1] == RET_DK
    scale = jnp.where(pl.program_id(0) >= k_first_tile, k_scale, 1.0).astype(F32)
    cos = cos_ref[...] * scale
    sin = sin_ref[...] * scale
    half = RET_DK // 2
    x1 = acc[:, :half]
    x2 = acc[:, half:]
    o_ref[:, cols.start:cols.start + half] = (x1 * cos - x2 * sin).astype(o_ref.dtype)
    o_ref[:, cols.start + half:cols.stop] = (x1 * sin + x2 * cos).astype(o_ref.dtype)


def _ret_kernel(dchunk_ref, q_ref, k_ref, v_ref, g_ref, dintra_ref, dq_ref, dk_ref, y_ref, state_ref, *, hp):
    c = pl.program_id(2)

    @pl.when(c == 0)
    def _():
        state_ref[...] = jnp.zeros_like(state_ref)

    for j in range(hp):
        h = pl.program_id(1) * hp + j
        q = q_ref[:, j * RET_DK:(j + 1) * RET_DK]
        k = k_ref[:, j * RET_DK:(j + 1) * RET_DK]
        v = v_ref[:, j * RET_DV:(j + 1) * RET_DV]
        scores = lax.dot_general(q, k, (((1,), (1,)), ((), ())), preferred_element_type=F32)
        scores = scores * dintra_ref[j]
        intra = jnp.dot(scores.astype(BF16), v, preferred_element_type=F32)
        cross = jnp.dot(q, state_ref[j].astype(BF16), preferred_element_type=F32)
        dq = dq_ref[j]
        out = jnp.concatenate(
            [intra[:, n:n + LANES] + cross[:, n:n + LANES] * dq for n in range(0, RET_DV, LANES)], axis=1)
        dkc = dk_ref[j]
        kd = (k.astype(F32) * jnp.concatenate([dkc] * (RET_DK // LANES), axis=1)).astype(BF16)
        state_ref[j] = dchunk_ref[h] * state_ref[j] + lax.dot_general(
            kd, v, (((0,), (0,)), ((), ())), preferred_element_type=F32)
        out = out * lax.rsqrt(jnp.mean(out * out, axis=-1, keepdims=True) + RMS_EPS)
        gate = g_ref[:, j * RET_DV:(j + 1) * RET_DV].astype(F32)
        y_ref[:, j * RET_DV:(j + 1) * RET_DV] = (jax.nn.silu(gate) * out).astype(y_ref.dtype)


def _retention_core(qk, vg, batch, seq, heads):
    m = batch * seq
    c = _tile(seq, 256)
    nc = seq // c
    log_gamma = jnp.log1p(-jnp.exp2(-5.0 - jnp.arange(heads, dtype=F32)))
    idx = jnp.arange(c, dtype=F32)
    diff = idx[:, None] - idx[None, :]
    dintra = jnp.where(diff >= 0, jnp.exp(log_gamma[:, None, None] * jnp.maximum(diff, 0.0)), 0.0)
    dq = jnp.broadcast_to(jnp.exp(log_gamma[:, None] * (idx + 1.0))[:, :, None], (heads, c, LANES))
    dk = jnp.broadcast_to(jnp.exp(log_gamma[:, None] * (c - 1.0 - idx))[:, :, None], (heads, c, LANES))
    dchunk = jnp.exp(log_gamma * c)
    hp = next(n for n in (4, 2, 1) if heads % n == 0)
    ng = heads // hp
    grid_spec = pltpu.PrefetchScalarGridSpec(
        num_scalar_prefetch=1,
        grid=(batch, ng, nc),
        in_specs=[
            pl.BlockSpec((c, hp * RET_DK), lambda b, h, i, s: (b * nc + i, h)),
            pl.BlockSpec((c, hp * RET_DK), lambda b, h, i, s: (b * nc + i, ng + h)),
            pl.BlockSpec((c, hp * RET_DV), lambda b, h, i, s: (b * nc + i, h)),
            pl.BlockSpec((c, hp * RET_DV), lambda b, h, i, s: (b * nc + i, ng + h)),
            pl.BlockSpec((hp, c, c), lambda b, h, i, s: (h, 0, 0)),
            pl.BlockSpec((hp, c, LANES), lambda b, h, i, s: (h, 0, 0)),
            pl.BlockSpec((hp, c, LANES), lambda b, h, i, s: (h, 0, 0)),
        ],
        out_specs=pl.BlockSpec((c, hp * RET_DV), lambda b, h, i, s: (b * nc + i, h)),
        scratch_shapes=[pltpu.VMEM((hp, RET_DK, RET_DV), F32)],
    )
    return pl.pallas_call(
        functools.partial(_ret_kernel, hp=hp),
        grid_spec=grid_spec,
        out_shape=jax.ShapeDtypeStruct((m, heads * RET_DV), BF16),
        compiler_params=_cparams(("parallel", "parallel", "arbitrary"), V7X_VMEM_LIMIT),
        name="retention_core",
    )(dchunk, qk, qk, vg, vg, dintra, dq, dk)


def _retention_layer(x2, norm_g, w_in, w_o, layer, batch, seq):
    m, d = x2.shape
    heads = d // RET_DK
    hq = heads * RET_DK
    h = _rmsnorm(x2, norm_g)
    pos = jnp.arange(seq, dtype=F32)
    half = RET_DK // 2
    inv = 1.0 / (ROPE_BASE ** (jnp.arange(half, dtype=F32) / half))
    ang = pos[:, None] * inv[None, :]
    cos, sin = jnp.cos(ang), jnp.sin(ang)
    tm = _tile(seq, 1024)
    tn = _tile(hq, 512)
    nrow = seq // tm
    rot = functools.partial(_ep_rotary, k_first_tile=hq // tn, k_scale=RET_DK ** -0.5)
    qk = _matmul(h, w_in, layer, col0=0, ncols=2 * hq, tm=tm, tn=tn, out_dtype=BF16, epilogue=rot,
                 extra=((cos, (tm, half), lambda j, i: (i % nrow, 0)),
                        (sin, (tm, half), lambda j, i: (i % nrow, 0))),
                 name="ret_proj_qk")
    vg = _matmul(h, w_in, layer, col0=2 * hq, ncols=2 * heads * RET_DV, tm=tm, tn=tn, out_dtype=BF16,
                 epilogue=_ep_store, name="ret_proj_vg")
    y = _retention_core(qk, vg, batch, seq, heads)
    tm_o = _tile(seq, 512)
    tn_o = _tile(d, 512)
    return _matmul(y, _cast_bf16(w_o, layer), 0, col0=0, ncols=d, tm=tm_o, tn=tn_o, out_dtype=F32,
                   epilogue=_ep_residual, extra=((x2, (tm_o, tn_o), lambda j, i: (i, j)),),
                   name="ret_out_proj")


def _pack_bf16_pairs(lo, hi):
    lo_bits = pltpu.bitcast(lo.astype(BF16).astype(F32), jnp.uint32)
    hi_bits = pltpu.bitcast(hi.astype(BF16).astype(F32), jnp.uint32)
    return hi_bits | lax.shift_right_logical(lo_bits, jnp.uint32(16))


def _unpack_bf16_pairs(words):
    lo = pltpu.bitcast(lax.shift_left(words, jnp.uint32(16)), F32)
    hi = pltpu.bitcast(words & jnp.uint32(0xFFFF0000), F32)
    return lo, hi


def _route_kernel(x_ref, g_ref, wcat_ref, h_ref, r_ref):
    x = x_ref[...]
    hn = x * lax.rsqrt(jnp.mean(x * x, axis=-1, keepdims=True) + RMS_EPS) * g_ref[...]
    half = hn.shape[1] // 2
    h_ref[...] = _pack_bf16_pairs(hn[:, :half], hn[:, half:])
    hn_hi = hn.astype(BF16)
    hn_lo = (hn - hn_hi.astype(F32)).astype(BF16)
    both = jnp.dot(hn_hi, wcat_ref[...], preferred_element_type=F32)
    logits = (both[:, :LANES] + both[:, LANES:]
              + jnp.dot(hn_lo, wcat_ref[:, :LANES], preferred_element_type=F32))
    lane = lax.broadcasted_iota(jnp.int32, logits.shape, 1)
    is_g = lane < MOE_GROUPS
    lg = jnp.where(is_g, logits, -jnp.inf)
    mg = jnp.max(lg, axis=-1, keepdims=True)
    grp = jnp.min(jnp.where(lg == mg, lane, LANES), axis=-1, keepdims=True)
    p_grp = 1.0 / jnp.sum(jnp.where(is_g, jnp.exp(lg - mg), 0.0), axis=-1, keepdims=True)
    e_lane = lane - MOE_GROUPS
    is_e = (e_lane >= 0) & (e_lane < MOE_EXPERTS) & ((e_lane >> 3) == grp)
    le = jnp.where(is_e, logits, -jnp.inf)
    m1 = jnp.max(le, axis=-1, keepdims=True)
    i1 = jnp.min(jnp.where(le == m1, lane, LANES), axis=-1, keepdims=True)
    le2 = jnp.where(lane == i1, -jnp.inf, le)
    m2 = jnp.max(le2, axis=-1, keepdims=True)
    i2 = jnp.min(jnp.where(le2 == m2, lane, LANES), axis=-1, keepdims=True)
    t = jnp.exp(m2 - m1)
    w1 = p_grp / (1.0 + t)
    w2 = p_grp * t / (1.0 + t)
    e1 = (i1 - MOE_GROUPS).astype(F32)
    e2 = (i2 - MOE_GROUPS).astype(F32)
    r_ref[...] = jnp.where(lane == 0, e1, jnp.where(lane == 1, e2,
                           jnp.where(lane == 2, w1, jnp.where(lane == 3, w2, 0.0))))


def _route(x2, norm_g, w_rg, w_re):
    m, d = x2.shape
    tm = _tile(m, 256)
    wr = jnp.concatenate(
        [w_rg, w_re, jnp.zeros((d, LANES - MOE_GROUPS - MOE_EXPERTS), F32)], axis=1)
    wr_hi = wr.astype(BF16)
    wr_lo = (wr - wr_hi.astype(F32)).astype(BF16)
    wr_cat = jnp.concatenate([wr_hi, wr_lo], axis=1)
    return pl.pallas_call(
        _route_kernel,
        grid=(m // tm,),
        in_specs=[pl.BlockSpec((tm, d), lambda i: (i, 0)),
                  pl.BlockSpec((1, d), lambda i: (0, 0)),
                  pl.BlockSpec((d, 2 * LANES), lambda i: (0, 0))],
        out_specs=[pl.BlockSpec((tm, d // 2), lambda i: (i, 0)),
                   pl.BlockSpec((tm, LANES), lambda i: (i, 0))],
        out_shape=[jax.ShapeDtypeStruct((m, d // 2), jnp.uint32),
                   jax.ShapeDtypeStruct((m, LANES), F32)],
        compiler_params=_cparams(("parallel",), V7X_VMEM_LIMIT),
        name="moe_route",
    )(x2, norm_g.reshape(1, d), wr_cat)


def _ffn_kernel(blk_e_ref, nused_ref,
                src_cur_ref, src_nxt_ref, wg_ref, wu_ref, wd_ref, h_hbm,
                y_ref,
                wgb_ref, wub_ref, wdb_ref, xbuf_ref, gsem, *, tb):
    i = pl.program_id(0)
    n_used = nused_ref[0]
    slot = i % 2

    def gather_copy(src_ref, r, s):
        return pltpu.make_async_copy(h_hbm.at[pl.ds(src_ref[0, 0, r], 1)],
                                     xbuf_ref.at[s, pl.ds(r, 1)], gsem.at[s])

    def start_gather(src_ref, s):
        for r in range(tb):
            gather_copy(src_ref, r, s).start(priority=1)

    def wait_gather(s):
        pltpu.make_async_copy(h_hbm.at[pl.ds(0, tb)], xbuf_ref.at[s], gsem.at[s]).wait()

    @pl.when(i == 0)
    def _():
        start_gather(src_cur_ref, 0)

    @pl.when(i >= n_used)
    def _():
        y_ref[...] = jnp.zeros(y_ref.shape, y_ref.dtype)

    @pl.when(i < n_used)
    def _():
        @pl.when(i + 1 < n_used)
        def _():
            start_gather(src_nxt_ref, 1 - slot)

        first = jnp.logical_or(i == 0, blk_e_ref[i] != blk_e_ref[jnp.maximum(i - 1, 0)])

        @pl.when(first)
        def _():
            wgb_ref[...] = wg_ref[...].astype(BF16)
            wub_ref[...] = wu_ref[...].astype(BF16)
            wdb_ref[...] = wd_ref[...].astype(BF16)

        wait_gather(slot)
        x_lo, x_hi = _unpack_bf16_pairs(xbuf_ref[slot])
        xb = jnp.concatenate([x_lo, x_hi], axis=1).astype(BF16)
        hid = (jax.nn.silu(jnp.dot(xb, wgb_ref[...], preferred_element_type=F32))
               * jnp.dot(xb, wub_ref[...], preferred_element_type=F32))
        y = jnp.dot(hid.astype(BF16), wdb_ref[...], preferred_element_type=F32)
        half = y.shape[1] // 2
        y_ref[...] = _pack_bf16_pairs(y[:, :half], y[:, half:])


def _moe_ffn(h, route, w_gate, w_up, w_down, layer, tb, tm):
    n = h.shape[0]
    _, e, d, f = w_gate.shape
    k = MOE_TOPK
    mcopies = n * k
    flat_e = route[:, 0:2].astype(jnp.int32).reshape(mcopies)
    se, order = lax.sort((flat_e, jnp.arange(mcopies, dtype=jnp.int32)), num_keys=1)
    counts = jnp.sum((flat_e[:, None] == jnp.arange(e)[None, :]).astype(jnp.int32), axis=0)
    padded = (counts + tb - 1) // tb * tb
    pad_end = jnp.cumsum(padded)
    pad_start = pad_end - padded
    start = jnp.cumsum(counts) - counts
    nb = (mcopies + e * (tb - 1) + tb - 1) // tb
    p = nb * tb
    blk_start = jnp.arange(nb) * tb
    blk_e = jnp.minimum(jnp.sum((blk_start[:, None] >= pad_end[None, :]).astype(jnp.int32), axis=1),
                        e - 1).astype(jnp.int32)
    blk_off = blk_start - pad_start[blk_e]
    nvalid = jnp.clip(counts[blk_e] - blk_off, 0, tb).astype(jnp.int32)
    n_used = (pad_end[-1] // tb).astype(jnp.int32).reshape(1)
    rows = jnp.arange(tb)[None, :]
    valid = rows < nvalid[:, None]
    sorted_idx = jnp.clip((start[blk_e] + blk_off)[:, None] + rows, 0, mcopies - 1)
    copy_id = order[sorted_idx]
    src3 = jnp.where(valid, copy_id // k, 0).reshape(nb, 1, tb)
    pos_sorted = pad_start[se] + jnp.arange(mcopies, dtype=jnp.int32) - start[se]
    pos = pos_sorted[jnp.argsort(order)].astype(jnp.int32)
    pos3 = pos.reshape(n // tm, tm, k).transpose(0, 2, 1).reshape(n // tm, 1, k * tm)

    smem = functools.partial(pl.BlockSpec, memory_space=pltpu.SMEM)
    grid_spec = pltpu.PrefetchScalarGridSpec(
        num_scalar_prefetch=2,
        grid=(nb,),
        in_specs=[
            smem((1, 1, tb), lambda i, be, nu: (i, 0, 0)),
            smem((1, 1, tb), lambda i, be, nu: (jnp.minimum(i + 1, nb - 1), 0, 0)),
            pl.BlockSpec((None, None, d, f), lambda i, be, nu: (layer, be[i], 0, 0)),
            pl.BlockSpec((None, None, d, f), lambda i, be, nu: (layer, be[i], 0, 0)),
            pl.BlockSpec((None, None, f, d), lambda i, be, nu: (layer, be[i], 0, 0)),
            pl.BlockSpec(memory_space=pl.ANY),
        ],
        out_specs=pl.BlockSpec((tb, d // 2), lambda i, be, nu: (i, 0)),
        scratch_shapes=[
            pltpu.VMEM((d, f), BF16), pltpu.VMEM((d, f), BF16), pltpu.VMEM((f, d), BF16),
            pltpu.VMEM((2, tb, d // 2), jnp.uint32),
            pltpu.SemaphoreType.DMA((2,)),
        ],
    )
    y = pl.pallas_call(
        functools.partial(_ffn_kernel, tb=tb),
        grid_spec=grid_spec,
        out_shape=jax.ShapeDtypeStruct((p, d // 2), jnp.uint32),
        compiler_params=_cparams(("arbitrary",), V7X_VMEM_LIMIT),
        name="moe_ffn",
    )(blk_e, n_used, src3, src3, w_gate, w_up, w_down, h)
    return y, pos3


def _combine_kernel(x_ref, r_ref, pos_cur_ref, pos_nxt_ref, y_hbm, *rest, n_norm, want_x):
    g_refs = rest[:n_norm]
    outs = rest[n_norm:-2]
    pbuf_ref, sem = rest[-2:]
    i = pl.program_id(0)
    slot = i % 2
    tm = x_ref.shape[0]

    def start_rows(pos_ref, s):
        for kk in range(MOE_TOPK):
            for r in range(tm):
                pltpu.make_async_copy(y_hbm.at[pl.ds(pos_ref[0, 0, kk * tm + r], 1)],
                                      pbuf_ref.at[s, kk, pl.ds(r, 1)], sem.at[s]).start(priority=r % 2)

    @pl.when(i == 0)
    def _():
        start_rows(pos_cur_ref, 0)

    @pl.when(i + 1 < pl.num_programs(0))
    def _():
        start_rows(pos_nxt_ref, 1 - slot)

    for kk in range(MOE_TOPK):
        pltpu.make_async_copy(y_hbm.at[pl.ds(0, tm)], pbuf_ref.at[slot, kk], sem.at[slot]).wait()

    w0 = r_ref[:, 2:3]
    w1 = r_ref[:, 3:4]
    lo0, hi0 = _unpack_bf16_pairs(pbuf_ref[slot, 0])
    lo1, hi1 = _unpack_bf16_pairs(pbuf_ref[slot, 1])
    x = x_ref[...] + jnp.concatenate([lo0 * w0 + lo1 * w1, hi0 * w0 + hi1 * w1], axis=1)
    oi = 0
    if want_x:
        outs[0][...] = x
        oi = 1
    r = lax.rsqrt(jnp.mean(x * x, axis=-1, keepdims=True) + RMS_EPS)
    y = x * r
    for g_ref, o_ref in zip(g_refs, outs[oi:]):
        o_ref[...] = (y * g_ref[...]).astype(o_ref.dtype)


def _combine_tile(m):
    return _tile(m, 256)


def _combine(x2, route, y_sorted, pos3, gains, out_dtypes, want_x):
    m, d = x2.shape
    tm = _combine_tile(m)
    nrow = m // tm
    smem = functools.partial(pl.BlockSpec, memory_space=pltpu.SMEM)
    in_specs = [pl.BlockSpec((tm, d), lambda i: (i, 0)),
                pl.BlockSpec((tm, LANES), lambda i: (i, 0)),
                smem((1, 1, MOE_TOPK * tm), lambda i: (i, 0, 0)),
                smem((1, 1, MOE_TOPK * tm), lambda i: (jnp.minimum(i + 1, nrow - 1), 0, 0)),
                pl.BlockSpec(memory_space=pl.ANY)]
    in_specs += [pl.BlockSpec((1, d), lambda i: (0, 0)) for _ in gains]
    out_shape = ([jax.ShapeDtypeStruct((m, d), F32)] if want_x else []) + [
        jax.ShapeDtypeStruct((m, d), dt) for dt in out_dtypes]
    out_specs = [pl.BlockSpec((tm, d), lambda i: (i, 0)) for _ in out_shape]
    return pl.pallas_call(
        functools.partial(_combine_kernel, n_norm=len(gains), want_x=want_x),
        grid=(nrow,),
        in_specs=in_specs,
        out_specs=out_specs,
        out_shape=out_shape,
        scratch_shapes=[pltpu.VMEM((2, MOE_TOPK, tm, d // 2), jnp.uint32), pltpu.SemaphoreType.DMA((2,))],
        compiler_params=_cparams(("arbitrary",), V7X_VMEM_LIMIT),
        name="moe_combine",
    )(x2, route, pos3, pos3, y_sorted, *[g.reshape(1, d) for g in gains])


def _t5_bucket(dist):
    n = jnp.maximum(dist, 0)
    exact = REL_BUCKETS // 2
    nf = jnp.maximum(n, 1).astype(F32)
    large = exact + (jnp.log(nf / exact) / math.log(REL_MAX_DIST / exact)
                     * (REL_BUCKETS - exact)).astype(jnp.int32)
    return jnp.where(n < exact, n, jnp.minimum(large, REL_BUCKETS - 1))


def _bias_lookup(table, bucket):
    onehot = jax.nn.one_hot(bucket, REL_BUCKETS, dtype=F32)
    return jnp.einsum('...b,bh->...h', onehot, table, precision=lax.Precision.HIGHEST)


def _cmp_kernel(rk_ref, rv_ref, pe_ref, w1_ref, w2_ref, o_ref):
    half = w1_ref.shape[1] // 2
    for s, r_ref in enumerate((rk_ref, rv_ref)):
        r = r_ref[0, 0].astype(F32)
        ra = (r + pe_ref[s, :, :half]).astype(BF16)
        rb = (r + pe_ref[s, :, half:]).astype(BF16)
        ha = jnp.dot(ra, w1_ref[s, :half, :], preferred_element_type=F32)
        hb = jnp.dot(rb, w1_ref[s, half:, :], preferred_element_type=F32)
        hb_next = jnp.concatenate([hb[1:], hb[:1]], axis=0)
        hid = jax.nn.gelu(ha + hb_next)
        o_ref[0, 0, s] = jnp.dot(hid.astype(BF16), w2_ref[s], preferred_element_type=F32).astype(o_ref.dtype)


def _compress(kv, batch, seq, pes, w1s, w2s):
    g, dk = NSA_GROUPS, NSA_DK
    nr = seq // CMP_STRIDE
    assert nr == dk
    ct = kv[:, :2 * g * dk].reshape(batch, nr, CMP_STRIDE, 2, g, dk)
    r = ct.transpose(3, 0, 4, 1, 2, 5).reshape(2, batch, g, nr, CMP_STRIDE * dk)
    pe = jnp.stack(pes).reshape(2, 1, CMP_LEN * dk)
    w1 = jnp.stack(w1s).reshape(2, CMP_LEN * dk, dk).astype(BF16)
    w2 = jnp.stack(w2s).astype(BF16)
    rspec = pl.BlockSpec((1, 1, nr, CMP_STRIDE * dk), lambda b, gi: (b, gi, 0, 0))
    return pl.pallas_call(
        _cmp_kernel,
        grid=(batch, g),
        in_specs=[rspec, rspec,
                  pl.BlockSpec((2, 1, CMP_LEN * dk), lambda b, gi: (0, 0, 0)),
                  pl.BlockSpec((2, CMP_LEN * dk, dk), lambda b, gi: (0, 0, 0)),
                  pl.BlockSpec((2, dk, dk), lambda b, gi: (0, 0, 0))],
        out_specs=pl.BlockSpec((1, 1, 2, nr, dk), lambda b, gi: (b, gi, 0, 0, 0)),
        out_shape=jax.ShapeDtypeStruct((batch, g, 2, nr, dk), BF16),
        compiler_params=_cparams(("parallel", "parallel"), V7X_VMEM_LIMIT),
        name="nsa_compress",
    )(r[0], r[1], pe, w1, w2)


BIAS_DIAG = 0
BIAS_NEXT = 1
BIAS_CONST = 2
BIAS_WIN_EDGE = 3
BIAS_MASKED = 4
N_BIAS_TILES = 5
CMP_PAD = 16
CMP_BAND = CMP_PAD + LANES // CMP_STRIDE
FAR_GROUP = 4


def _nsa_kernel(q_ref, gate_ref, kvc_ref, ks_ref, vs_ref, kw_ref, vw_ref, biasc_ref, bias_ref,
                ovt_ref, ext_ref, o_ref,
                selpen_ref, m_ref, l_ref, acc_ref, sfar_ref, snear_ref, oc_ref, ow_ref, scmp_ref,
                *, hg, tq, n_cmp, n_sblk):
    qb = pl.program_id(2)
    dk = NSA_DK
    nt_dims = (((1,), (1,)), ((), ()))
    q8 = jnp.concatenate([q_ref[:, h * dk:(h + 1) * dk] for h in range(hg)], axis=0)
    kj = lax.broadcasted_iota(jnp.int32, (LANES, tq), 0)
    qi = lax.broadcasted_iota(jnp.int32, (LANES, tq), 1)
    t_abs = qb * tq + qi

    def tile_h(a):
        return jnp.concatenate([a] * hg, axis=1)

    win_tiles = WINDOW // LANES
    n_near = win_tiles + 1
    lo = jnp.maximum(qb - win_tiles, 0)
    c_far = bias_ref[BIAS_CONST, 0:1, :]

    def scores(k_ref, row0, n, addend):
        return lax.dot_general(k_ref[pl.ds(row0, n), :], q8, nt_dims, preferred_element_type=F32) + addend

    def colmax(sc):
        return jnp.max(sc, axis=0, keepdims=True)

    def pv(v_ref, row0, n, p):
        return lax.dot_general(v_ref[pl.ds(row0, n), :], p.astype(BF16), (((0,), (0,)), ((), ())),
                               preferred_element_type=F32)

    near_row0, near_ok = [], []
    for j in range(n_near):
        kb = qb - (win_tiles - j)
        near_row0.append(pl.multiple_of(jnp.maximum(kb, 0) * LANES, LANES))
        near_ok.append(kb >= 0)

    def near_bias(j, window_branch):
        d = win_tiles - j
        tile = BIAS_DIAG if d == 0 else (BIAS_NEXT if d == 1 else BIAS_CONST)
        if window_branch and d == win_tiles:
            tile = BIAS_WIN_EDGE
        return bias_ref[jnp.where(near_ok[j], tile, BIAS_MASKED)]

    m_w = jnp.full(c_far.shape, 0.5 * NEG, F32)
    for j in range(n_near):
        s_w = scores(kw_ref, near_row0[j], LANES, near_bias(j, True))
        snear_ref[1, j] = s_w
        m_w = jnp.maximum(m_w, colmax(s_w))
    l_w = jnp.zeros(c_far.shape, F32)
    acc_w = jnp.zeros((dk, hg * tq), F32)
    for j in range(n_near):
        p_w = jnp.exp2(snear_ref[1, j] - m_w)
        l_w = l_w + jnp.sum(p_w, axis=0, keepdims=True)
        acc_w = acc_w + pv(vw_ref, near_row0[j], LANES, p_w)
    ow_ref[...] = acc_w * (1.0 / jnp.maximum(l_w, 1e-30))

    s = lax.dot_general(kvc_ref[0], q8, nt_dims, preferred_element_type=F32) + c_far
    scmp_ref[pl.ds(0, CMP_PAD), :] = jnp.zeros((CMP_PAD, hg * tq), F32)
    scmp_ref[pl.ds(CMP_PAD, LANES), :] = s
    band0 = pl.multiple_of(qb * (tq // CMP_STRIDE), 8)
    scmp_ref[pl.ds(band0, CMP_BAND), :] = scmp_ref[pl.ds(band0, CMP_BAND), :] + biasc_ref[...]
    s = scmp_ref[pl.ds(CMP_PAD, LANES), :]
    valid_c = (kj * CMP_STRIDE + (CMP_LEN - 1) <= t_abs) & (kj < n_cmp)
    s = s + tile_h(jnp.where(valid_c, 0.0, NEG))
    mx = jnp.maximum(jnp.max(s, axis=0, keepdims=True), 0.5 * NEG)
    p = jnp.exp2(s - mx)
    p = p * (1.0 / jnp.maximum(jnp.sum(p, axis=0, keepdims=True), 1e-30))
    oc_ref[...] = lax.dot_general(kvc_ref[1], p.astype(BF16), (((0,), (0,)), ((), ())),
                                  preferred_element_type=F32)
    psum = p[:, 0:tq]
    for h in range(1, hg):
        psum = psum + p[:, h * tq:(h + 1) * tq]
    p_hi = psum.astype(BF16)
    p_lo = (psum - p_hi.astype(F32)).astype(BF16)
    imp = (jnp.dot(ovt_ref[...], p_hi, preferred_element_type=F32)
           + jnp.dot(ovt_ref[...], p_lo, preferred_element_type=F32))

    imp = imp[:n_sblk]
    kb_i = lax.broadcasted_iota(jnp.int32, (n_sblk, tq), 0)
    t_b = qb * tq + lax.broadcasted_iota(jnp.int32, (n_sblk, tq), 1)
    cur = lax.shift_right_logical(t_b, int(math.log2(SEL_BLOCK)))
    valid_b = kb_i * SEL_BLOCK <= t_b
    forced = (kb_i == 0) | (kb_i == cur) | (kb_i == cur - 1)
    score = jnp.where(valid_b, jnp.where(forced, FORCED_SCORE, imp), INVALID_SCORE)
    rank = jnp.zeros((n_sblk, tq), F32)
    for j in range(n_sblk):
        cj = score[j:j + 1, :]
        ahead = (cj > score) | ((cj == score) & (kb_i > j))
        rank = rank + jnp.where(ahead, 1.0, 0.0)
    sel = jnp.where(rank < min(SEL_TOPK, n_sblk), 1.0, 0.0)
    if n_sblk < LANES:
        sel = jnp.concatenate([sel, jnp.zeros((LANES - n_sblk, tq), F32)], axis=0)
    selmask = jnp.dot(ext_ref[...], sel.astype(BF16), preferred_element_type=F32)
    selpen_ref[...] = ((selmask - 1.0) * (-NEG)).reshape(selpen_ref.shape)

    m_near = jnp.full(c_far.shape, 0.5 * NEG, F32)
    for j in range(n_near):
        kb_c = lax.shift_right_logical(near_row0[j], int(math.log2(LANES)))
        s_s = scores(ks_ref, near_row0[j], LANES, near_bias(j, False) + tile_h(selpen_ref[kb_c]))
        snear_ref[0, j] = s_s
        m_near = jnp.maximum(m_near, colmax(s_s))

    assert FAR_GROUP - 1 <= win_tiles
    n_grp = (lo + FAR_GROUP - 1) // FAR_GROUP
    m_ref[...] = jnp.full(m_ref.shape, 0.5 * NEG, F32)

    def far_scores(gi, carry):
        kb = FAR_GROUP * gi
        pen = jnp.concatenate(
            [selpen_ref[kb + t] + jnp.where(kb + t < lo, 0.0, NEG) for t in range(FAR_GROUP)], axis=0)
        sc = scores(ks_ref, pl.multiple_of(kb * LANES, LANES), FAR_GROUP * LANES, tile_h(pen))
        sfar_ref[gi] = sc
        m_ref[...] = jnp.maximum(m_ref[...], colmax(sc))
        return carry
    lax.fori_loop(0, n_grp, far_scores, 0)

    m_s = jnp.maximum(m_near, m_ref[...] + c_far)
    l_s = jnp.zeros(m_s.shape, F32)
    acc_s = jnp.zeros((dk, hg * tq), F32)
    for j in range(n_near):
        p_s = jnp.exp2(snear_ref[0, j] - m_s)
        l_s = l_s + jnp.sum(p_s, axis=0, keepdims=True)
        acc_s = acc_s + pv(vs_ref, near_row0[j], LANES, p_s)
    l_ref[...] = l_s
    acc_ref[...] = acc_s

    m_far = m_s - c_far

    def far_pv(gi, carry):
        row0 = pl.multiple_of(FAR_GROUP * gi * LANES, LANES)
        p = jnp.exp2(sfar_ref[gi] - m_far)
        l_ref[...] = l_ref[...] + jnp.sum(p, axis=0, keepdims=True)
        acc_ref[...] = acc_ref[...] + pv(vs_ref, row0, FAR_GROUP * LANES, p)
        return carry
    lax.fori_loop(0, n_grp, far_pv, 0)

    o_s = acc_ref[...] * (1.0 / jnp.maximum(l_ref[...], 1e-30))
    o_w = ow_ref[...]
    o_c = oc_ref[...]
    gt = gate_ref[...].T
    for h in range(hg):
        cols = slice(h * tq, (h + 1) * tq)
        out_t = (gt[3 * h:3 * h + 1] * o_c[:, cols] + gt[3 * h + 1:3 * h + 2] * o_s[:, cols]
                 + gt[3 * h + 2:3 * h + 3] * o_w[:, cols])
        o_ref[:, h * dk:(h + 1) * dk] = out_t.T.astype(o_ref.dtype)


def _nsa_attention(q, gates, kv, kvc, rel_bias, batch, seq, heads):
    g, dk = NSA_GROUPS, NSA_DK
    hg = heads // g
    tq = LANES
    nqb = seq // tq
    n_cmp = (seq - CMP_LEN) // CMP_STRIDE + 1
    n_sblk = seq // SEL_BLOCK
    nr = seq // CMP_STRIDE
    assert nr == LANES and n_sblk <= LANES and n_sblk % 8 == 0 and tq == LANES
    nt = seq // LANES
    rows = hg * tq
    table = rel_bias.astype(F32) * LOG2E
    assert (REL_MAX_DIST + CMP_LEN - 1 + CMP_STRIDE - 1) // CMP_STRIDE <= CMP_PAD and tq % CMP_STRIDE == 0
    band_w = jnp.arange(CMP_BAND)[:, None]
    dist_c = jnp.arange(tq)[None, :] - (band_w - CMP_PAD) * CMP_STRIDE - (CMP_LEN - 1)
    bias_c = _bias_lookup(table, _t5_bucket(dist_c)) - table[REL_BUCKETS - 1]
    bias_c = bias_c.reshape(CMP_BAND, tq, g, hg).transpose(2, 0, 3, 1).reshape(g, CMP_BAND, rows)
    ii = jnp.arange(tq)
    n_far = -(-(REL_MAX_DIST + tq - 1) // tq)
    dist = jnp.arange(n_far + 1)[:, None, None] * tq + ii[None, None, :] - ii[None, :, None]
    bias_t = _bias_lookup(table, _t5_bucket(dist))
    bias_t = bias_t.reshape(n_far + 1, tq, tq, g, hg).transpose(0, 3, 1, 4, 2).reshape(n_far + 1, g, tq, rows)
    assert n_far == BIAS_CONST and WINDOW % tq == 0 and WINDOW // tq >= n_far
    key_j = jnp.arange(tq)[:, None]
    qry_i = jnp.tile(jnp.arange(tq), hg)[None, :]
    future = key_j > qry_i
    bias_t = jnp.stack([
        jnp.where(future, NEG, bias_t[0]),
        bias_t[1],
        bias_t[2],
        jnp.where(future, bias_t[2], NEG),
        jnp.full_like(bias_t[2], NEG),
    ])
    cs = np.arange(LANES) * CMP_STRIDE
    ss = np.arange(LANES) * SEL_BLOCK
    ov = np.clip(np.minimum(cs[:, None] + CMP_LEN, ss[None, :] + SEL_BLOCK)
                 - np.maximum(cs[:, None], ss[None, :]), 0, None) / CMP_LEN
    ov[n_cmp:, :] = 0.0
    ov[:, n_sblk:] = 0.0
    overlap_t = jnp.asarray(ov.T, BF16)
    ex = ((np.arange(seq)[:, None] // SEL_BLOCK) == np.arange(LANES)[None, :]).astype(np.float32)
    expand_t = jnp.asarray(ex, BF16)

    kspec = lambda blk: pl.BlockSpec((seq, dk), lambda b, gi, qb: (b, blk * g + gi))
    return pl.pallas_call(
        functools.partial(_nsa_kernel, hg=hg, tq=tq, n_cmp=n_cmp, n_sblk=n_sblk),
        grid=(batch, g, nqb),
        in_specs=[
            pl.BlockSpec((tq, hg * dk), lambda b, gi, qb: (b * nqb + qb, gi)),
            pl.BlockSpec((tq, LANES), lambda b, gi, qb: (b * nqb + qb, gi)),
            pl.BlockSpec((None, None, 2, nr, dk), lambda b, gi, qb: (b, gi, 0, 0, 0)),
            kspec(2), kspec(3), kspec(4), kspec(5),
            pl.BlockSpec((None, CMP_BAND, rows), lambda b, gi, qb: (gi, 0, 0)),
            pl.BlockSpec((N_BIAS_TILES, None, tq, rows), lambda b, gi, qb: (0, gi, 0, 0)),
            pl.BlockSpec((LANES, LANES), lambda b, gi, qb: (0, 0)),
            pl.BlockSpec((seq, LANES), lambda b, gi, qb: (0, 0)),
        ],
        out_specs=pl.BlockSpec((tq, hg * dk), lambda b, gi, qb: (b * nqb + qb, gi)),
        out_shape=jax.ShapeDtypeStruct((batch * seq, heads * dk), BF16),
        scratch_shapes=[pltpu.VMEM((nt, LANES, tq), F32),
                        pltpu.VMEM((1, rows), F32), pltpu.VMEM((1, rows), F32),
                        pltpu.VMEM((dk, rows), F32),
                        pltpu.VMEM((-(-nt // FAR_GROUP), FAR_GROUP * LANES, rows), F32),
                        pltpu.VMEM((2, WINDOW // LANES + 1, LANES, rows), F32),
                        pltpu.VMEM((dk, rows), F32), pltpu.VMEM((dk, rows), F32),
                        pltpu.VMEM((CMP_PAD + LANES, rows), F32)],
        compiler_params=_cparams(("parallel", "parallel", "arbitrary"), V7X_VMEM_LIMIT),
        name="nsa_attention",
    )(q, gates, kvc, kv, kv, kv, kv, bias_c, bias_t, overlap_t, expand_t)


def _nsa_layer(x2, h_kv, h_q, w_kv, cmp_params, rel_bias, w_q, w_o, layer, batch, seq):
    m, d = x2.shape
    g, dk = NSA_GROUPS, NSA_DK
    heads = d // dk
    hg = heads // g
    tm = _tile(seq, 1024)
    kv = _matmul(h_kv, w_kv[None], 0, col0=0, ncols=6 * g * dk, tm=tm, tn=_tile(6 * g * dk, 512),
                 out_dtype=BF16, epilogue=_ep_store, name="nsa_proj_kv")
    (k_pe, k_w1, k_w2, v_pe, v_w1, v_w2) = cmp_params
    kvc = _compress(kv, batch, seq, (k_pe, v_pe), (k_w1, v_w1), (k_w2, v_w2))
    q = _matmul(h_q, w_q, layer, col0=0, ncols=heads * dk, tm=tm, tn=_tile(heads * dk, 512),
                out_dtype=BF16, epilogue=functools.partial(_ep_scale, scale=dk ** -0.5 * LOG2E),
                name="nsa_proj_q")
    wg = w_q[layer, :, heads * dk:].reshape(d, g, hg * 3)
    wg = jnp.pad(wg, ((0, 0), (0, 0), (0, LANES - hg * 3))).reshape(1, d, g * LANES)
    gates = _matmul(h_q, wg, 0, col0=0, ncols=g * LANES, tm=tm, tn=g * LANES, out_dtype=F32,
                    epilogue=_ep_sigmoid, name="nsa_proj_gate")
    o = _nsa_attention(q, gates, kv, kvc, rel_bias, batch, seq, heads)
    tn_o = _tile(d, 512)
    return _matmul(o, w_o, layer, col0=0, ncols=d, tm=tm, tn=tn_o, out_dtype=F32, epilogue=_ep_residual,
                   extra=((x2, (tm, tn_o), lambda j, i: (i, j)),), name="nsa_out_proj")


def kernel(x, ret_norm_g, ret_w_in, ret_w_o, kv_norm_g, nsa_w_kv, cmp_k_pe, cmp_k_w1, cmp_k_w2, cmp_v_pe, cmp_v_w1, cmp_v_w2, rel_bias, nsa_norm_g, nsa_w_q, nsa_w_o, moe_norm_g, moe_w_router_group, moe_w_router_expert, moe_w_gate, moe_w_up, moe_w_down, final_norm_g):
    batch, seq, d = x.shape
    depth = moe_norm_g.shape[0]
    n_a = ret_norm_g.shape[0]
    x2 = x.reshape(batch * seq, d)
    cmp_params = (cmp_k_pe, cmp_k_w1, cmp_k_w2, cmp_v_pe, cmp_v_w1, cmp_v_w2)
    h_kv = h_q = None
    out = None
    for layer in range(depth):
        if layer < n_a:
            x2 = _retention_layer(x2, ret_norm_g[layer], ret_w_in, ret_w_o, layer, batch, seq)
        else:
            j = layer - n_a
            if h_q is None:
                h_kv = _rmsnorm(x2, kv_norm_g)
                h_q = _rmsnorm(x2, nsa_norm_g[j])
            x2 = _nsa_layer(x2, h_kv, h_q, nsa_w_kv, cmp_params, rel_bias, nsa_w_q, nsa_w_o, j,
                            batch, seq)
        hm, route = _route(x2, moe_norm_g[layer], moe_w_router_group[layer], moe_w_router_expert[layer])
        ys, pos = _moe_ffn(hm, route, moe_w_gate, moe_w_up, moe_w_down, layer, tb=MOE_ROWS,
                           tm=_combine_tile(batch * seq))
        if layer == depth - 1:
            (out,) = _combine(x2, route, ys, pos, [final_norm_g], [F32], want_x=False)
        elif layer + 1 == n_a:
            j = layer + 1 - n_a
            x2, h_kv, h_q = _combine(x2, route, ys, pos, [kv_norm_g, nsa_norm_g[j]], [BF16, BF16],
                                     want_x=True)
        elif layer + 1 < n_a:
            (x2,) = _combine(x2, route, ys, pos, [], [], want_x=True)
        else:
            j = layer + 1 - n_a
            x2, h_q = _combine(x2, route, ys, pos, [nsa_norm_g[j]], [BF16], want_x=True)
    return out.reshape(batch, seq, d)
```

```python
import functools
import math

import numpy as np
import jax
import jax.numpy as jnp
from jax import lax
from jax.experimental import pallas as pl
from jax.experimental.pallas import tpu as pltpu

RMS_EPS = 1e-6
RET_DK = 256
RET_DV = 2 * RET_DK
ROPE_BASE = 10000.0
NSA_DK = 128
NSA_GROUPS = 4
CMP_LEN = 32
CMP_STRIDE = 16
SEL_BLOCK = 64
SEL_TOPK = 16
WINDOW = 512
FORCED_SCORE = 1e4
INVALID_SCORE = -1e9
REL_BUCKETS = 32
REL_MAX_DIST = 128
MOE_GROUPS = 8
MOE_EPG = 8
MOE_EXPERTS = MOE_GROUPS * MOE_EPG
MOE_TOPK = 2
MOE_ROWS = 256

LANES = 128
V7X_VMEM_LIMIT = 56 * 1024 * 1024
NEG = -1e30
LOG2E = math.log2(math.e)

BF16 = jnp.bfloat16
F32 = jnp.float32


def _tile(dim, pref):
    t = min(dim, pref)
    while dim % t:
        t //= 2
    return t


def _cparams(sem, vmem=None):
    return pltpu.CompilerParams(dimension_semantics=sem, vmem_limit_bytes=vmem)


def _norm_kernel(x_ref, g_ref, o_ref):
    x = x_ref[...]
    y = x * lax.rsqrt(jnp.mean(x * x, axis=-1, keepdims=True) + RMS_EPS)
    o_ref[...] = (y * g_ref[...]).astype(o_ref.dtype)


def _rmsnorm(x2, g, out_dtype=BF16):
    m, d = x2.shape
    tm = _tile(m, 256)
    return pl.pallas_call(
        _norm_kernel,
        grid=(m // tm,),
        in_specs=[pl.BlockSpec((tm, d), lambda i: (i, 0)),
                  pl.BlockSpec((1, d), lambda i: (0, 0))],
        out_specs=pl.BlockSpec((tm, d), lambda i: (i, 0)),
        out_shape=jax.ShapeDtypeStruct((m, d), out_dtype),
        compiler_params=_cparams(("parallel",), V7X_VMEM_LIMIT),
        name="rmsnorm",
    )(x2, g.reshape(1, d))


MM_CHUNK = 256


def _mm_kernel(a_ref, w_ref, *rest, n_extra, epilogue, cast_w):
    extra = rest[:n_extra]
    o_ref = rest[n_extra]
    if cast_w:
        wbf_ref = rest[n_extra + 1]

        @pl.when(pl.program_id(1) == 0)
        def _():
            wbf_ref[...] = w_ref[...].astype(BF16)
    else:
        wbf_ref = w_ref

    a = a_ref[...]
    tn = o_ref.shape[1]
    chunk = min(tn, MM_CHUNK)
    for c0 in range(0, tn, chunk):
        cols = slice(c0, c0 + chunk)
        acc = jnp.dot(a, wbf_ref[:, cols], preferred_element_type=F32)
        epilogue(acc, o_ref, cols, *extra)


def _matmul(a, w, layer, *, col0, ncols, tm, tn, out_dtype, epilogue, extra=(), name):
    m, k = a.shape
    assert col0 % tn == 0 and ncols % tn == 0 and m % tm == 0
    joff = col0 // tn
    cast_w = w.dtype != BF16
    in_specs = [pl.BlockSpec((tm, k), lambda j, i: (i, 0)),
                pl.BlockSpec((None, k, tn), lambda j, i: (layer, 0, j + joff))]
    args = [a, w]
    for arr, bshape, imap in extra:
        in_specs.append(pl.BlockSpec(bshape, imap))
        args.append(arr)
    return pl.pallas_call(
        functools.partial(_mm_kernel, n_extra=len(extra), epilogue=epilogue, cast_w=cast_w),
        grid=(ncols // tn, m // tm),
        in_specs=in_specs,
        out_specs=pl.BlockSpec((tm, tn), lambda j, i: (i, j)),
        out_shape=jax.ShapeDtypeStruct((m, ncols), out_dtype),
        scratch_shapes=[pltpu.VMEM((k, tn), BF16)] if cast_w else [],
        compiler_params=_cparams(("arbitrary", "arbitrary"), V7X_VMEM_LIMIT),
        name=name,
    )(*args)


def _cast_kernel(w_ref, o_ref):
    o_ref[...] = w_ref[...].astype(o_ref.dtype)


def _cast_bf16(w, layer):
    _, k, n = w.shape
    tk = _tile(k, 512)
    return pl.pallas_call(
        _cast_kernel,
        grid=(k // tk,),
        in_specs=[pl.BlockSpec((None, tk, n), lambda i: (layer, i, 0))],
        out_specs=pl.BlockSpec((None, tk, n), lambda i: (0, i, 0)),
        out_shape=jax.ShapeDtypeStruct((1, k, n), BF16),
        compiler_params=_cparams(("parallel",), V7X_VMEM_LIMIT),
        name="cast_bf16",
    )(w)


def _ep_store(acc, o_ref, cols):
    o_ref[:, cols] = acc.astype(o_ref.dtype)


def _ep_residual(acc, o_ref, cols, r_ref):
    o_ref[:, cols] = (r_ref[:, cols] + acc).astype(o_ref.dtype)


def _ep_scale(acc, o_ref, cols, *, scale):
    o_ref[:, cols] = (acc * scale).astype(o_ref.dtype)


def _ep_sigmoid(acc, o_ref, cols):
    o_ref[:, cols] = jax.nn.sigmoid(acc).astype(o_ref.dtype)


def _ep_rotary(acc, o_ref, cols, cos_ref, sin_ref, *, k_first_tile, k_scale):
    assert acc.shape[1] == RET_DK
    scale = jnp.where(pl.program_id(0) >= k_first_tile, k_scale, 1.0).astype(F32)
    cos = cos_ref[...] * scale
    sin = sin_ref[...] * scale
    half = RET_DK // 2
    x1 = acc[:, :half]
    x2 = acc[:, half:]
    o_ref[:, cols.start:cols.start + half] = (x1 * cos - x2 * sin).astype(o_ref.dtype)
    o_ref[:, cols.start + half:cols.stop] = (x1 * sin + x2 * cos).astype(o_ref.dtype)


def _ret_kernel(dchunk_ref, q_ref, k_ref, v_ref, g_ref, dintra_ref, dq_ref, dk_ref, y_ref, state_ref, *, hp):
    c = pl.program_id(2)

    @pl.when(c == 0)
    def _():
        state_ref[...] = jnp.zeros_like(state_ref)

    for j in range(hp):
        h = pl.program_id(1) * hp + j
        q = q_ref[:, j * RET_DK:(j + 1) * RET_DK]
        k = k_ref[:, j * RET_DK:(j + 1) * RET_DK]
        v = v_ref[:, j * RET_DV:(j + 1) * RET_DV]
        scores = lax.dot_general(q, k, (((1,), (1,)), ((), ())), preferred_element_type=F32)
        scores = scores * dintra_ref[j]
        intra = jnp.dot(scores.astype(BF16), v, preferred_element_type=F32)
        cross = jnp.dot(q, state_ref[j].astype(BF16), preferred_element_type=F32)
        dq = dq_ref[j]
        out = jnp.concatenate(
            [intra[:, n:n + LANES] + cross[:, n:n + LANES] * dq for n in range(0, RET_DV, LANES)], axis=1)
        dkc = dk_ref[j]
        kd = (k.astype(F32) * jnp.concatenate([dkc] * (RET_DK // LANES), axis=1)).astype(BF16)
        state_ref[j] = dchunk_ref[h] * state_ref[j] + lax.dot_general(
            kd, v, (((0,), (0,)), ((), ())), preferred_element_type=F32)
        out = out * lax.rsqrt(jnp.mean(out * out, axis=-1, keepdims=True) + RMS_EPS)
        gate = g_ref[:, j * RET_DV:(j + 1) * RET_DV].astype(F32)
        y_ref[:, j * RET_DV:(j + 1) * RET_DV] = (jax.nn.silu(gate) * out).astype(y_ref.dtype)


def _retention_core(qk, vg, batch, seq, heads):
    m = batch * seq
    c = _tile(seq, 256)
    nc = seq // c
    log_gamma = jnp.log1p(-jnp.exp2(-5.0 - jnp.arange(heads, dtype=F32)))
    idx = jnp.arange(c, dtype=F32)
    diff = idx[:, None] - idx[None, :]
    dintra = jnp.where(diff >= 0, jnp.exp(log_gamma[:, None, None] * jnp.maximum(diff, 0.0)), 0.0)
    dq = jnp.broadcast_to(jnp.exp(log_gamma[:, None] * (idx + 1.0))[:, :, None], (heads, c, LANES))
    dk = jnp.broadcast_to(jnp.exp(log_gamma[:, None] * (c - 1.0 - idx))[:, :, None], (heads, c, LANES))
    dchunk = jnp.exp(log_gamma * c)
    hp = next(n for n in (4, 2, 1) if heads % n == 0)
    ng = heads // hp
    grid_spec = pltpu.PrefetchScalarGridSpec(
        num_scalar_prefetch=1,
        grid=(batch, ng, nc),
        in_specs=[
            pl.BlockSpec((c, hp * RET_DK), lambda b, h, i, s: (b * nc + i, h)),
            pl.BlockSpec((c, hp * RET_DK), lambda b, h, i, s: (b * nc + i, ng + h)),
            pl.BlockSpec((c, hp * RET_DV), lambda b, h, i, s: (b * nc + i, h)),
            pl.BlockSpec((c, hp * RET_DV), lambda b, h, i, s: (b * nc + i, ng + h)),
            pl.BlockSpec((hp, c, c), lambda b, h, i, s: (h, 0, 0)),
            pl.BlockSpec((hp, c, LANES), lambda b, h, i, s: (h, 0, 0)),
            pl.BlockSpec((hp, c, LANES), lambda b, h, i, s: (h, 0, 0)),
        ],
        out_specs=pl.BlockSpec((c, hp * RET_DV), lambda b, h, i, s: (b * nc + i, h)),
        scratch_shapes=[pltpu.VMEM((hp, RET_DK, RET_DV), F32)],
    )
    return pl.pallas_call(
        functools.partial(_ret_kernel, hp=hp),
        grid_spec=grid_spec,
        out_shape=jax.ShapeDtypeStruct((m, heads * RET_DV), BF16),
        compiler_params=_cparams(("parallel", "parallel", "arbitrary"), V7X_VMEM_LIMIT),
        name="retention_core",
    )(dchunk, qk, qk, vg, vg, dintra, dq, dk)


def _retention_layer(x2, norm_g, w_in, w_o, layer, batch, seq):
    m, d = x2.shape
    heads = d // RET_DK
    hq = heads * RET_DK
    h = _rmsnorm(x2, norm_g)
    pos = jnp.arange(seq, dtype=F32)
    half = RET_DK // 2
    inv = 1.0 / (ROPE_BASE ** (jnp.arange(half, dtype=F32) / half))
    ang = pos[:, None] * inv[None, :]
    cos, sin = jnp.cos(ang), jnp.sin(ang)
    tm = _tile(seq, 1024)
    tn = _tile(hq, 512)
    nrow = seq // tm
    rot = functools.partial(_ep_rotary, k_first_tile=hq // tn, k_scale=RET_DK ** -0.5)
    qk = _matmul(h, w_in, layer, col0=0, ncols=2 * hq, tm=tm, tn=tn, out_dtype=BF16, epilogue=rot,
                 extra=((cos, (tm, half), lambda j, i: (i % nrow, 0)),
                        (sin, (tm, half), lambda j, i: (i % nrow, 0))),
                 name="ret_proj_qk")
    vg = _matmul(h, w_in, layer, col0=2 * hq, ncols=2 * heads * RET_DV, tm=tm, tn=tn, out_dtype=BF16,
                 epilogue=_ep_store, name="ret_proj_vg")
    y = _retention_core(qk, vg, batch, seq, heads)
    tm_o = _tile(seq, 512)
    tn_o = _tile(d, 512)
    return _matmul(y, _cast_bf16(w_o, layer), 0, col0=0, ncols=d, tm=tm_o, tn=tn_o, out_dtype=F32,
                   epilogue=_ep_residual, extra=((x2, (tm_o, tn_o), lambda j, i: (i, j)),),
                   name="ret_out_proj")


def _pack_bf16_pairs(lo, hi):
    lo_bits = pltpu.bitcast(lo.astype(BF16).astype(F32), jnp.uint32)
    hi_bits = pltpu.bitcast(hi.astype(BF16).astype(F32), jnp.uint32)
    return hi_bits | lax.shift_right_logical(lo_bits, jnp.uint32(16))


def _unpack_bf16_pairs(words):
    lo = pltpu.bitcast(lax.shift_left(words, jnp.uint32(16)), F32)
    hi = pltpu.bitcast(words & jnp.uint32(0xFFFF0000), F32)
    return lo, hi


def _route_kernel(x_ref, g_ref, wcat_ref, h_ref, r_ref):
    x = x_ref[...]
    hn = x * lax.rsqrt(jnp.mean(x * x, axis=-1, keepdims=True) + RMS_EPS) * g_ref[...]
    half = hn.shape[1] // 2
    h_ref[...] = _pack_bf16_pairs(hn[:, :half], hn[:, half:])
    hn_hi = hn.astype(BF16)
    hn_lo = (hn - hn_hi.astype(F32)).astype(BF16)
    both = jnp.dot(hn_hi, wcat_ref[...], preferred_element_type=F32)
    logits = (both[:, :LANES] + both[:, LANES:]
              + jnp.dot(hn_lo, wcat_ref[:, :LANES], preferred_element_type=F32))
    lane = lax.broadcasted_iota(jnp.int32, logits.shape, 1)
    is_g = lane < MOE_GROUPS
    lg = jnp.where(is_g, logits, -jnp.inf)
    mg = jnp.max(lg, axis=-1, keepdims=True)
    grp = jnp.min(jnp.where(lg == mg, lane, LANES), axis=-1, keepdims=True)
    p_grp = 1.0 / jnp.sum(jnp.where(is_g, jnp.exp(lg - mg), 0.0), axis=-1, keepdims=True)
    e_lane = lane - MOE_GROUPS
    is_e = (e_lane >= 0) & (e_lane < MOE_EXPERTS) & ((e_lane >> 3) == grp)
    le = jnp.where(is_e, logits, -jnp.inf)
    m1 = jnp.max(le, axis=-1, keepdims=True)
    i1 = jnp.min(jnp.where(le == m1, lane, LANES), axis=-1, keepdims=True)
    le2 = jnp.where(lane == i1, -jnp.inf, le)
    m2 = jnp.max(le2, axis=-1, keepdims=True)
    i2 = jnp.min(jnp.where(le2 == m2, lane, LANES), axis=-1, keepdims=True)
    t = jnp.exp(m2 - m1)
    w1 = p_grp / (1.0 + t)
    w2 = p_grp * t / (1.0 + t)
    e1 = (i1 - MOE_GROUPS).astype(F32)
    e2 = (i2 - MOE_GROUPS).astype(F32)
    r_ref[...] = jnp.where(lane == 0, e1, jnp.where(lane == 1, e2,
                           jnp.where(lane == 2, w1, jnp.where(lane == 3, w2, 0.0))))


def _route(x2, norm_g, w_rg, w_re):
    m, d = x2.shape
    tm = _tile(m, 256)
    wr = jnp.concatenate(
        [w_rg, w_re, jnp.zeros((d, LANES - MOE_GROUPS - MOE_EXPERTS), F32)], axis=1)
    wr_hi = wr.astype(BF16)
    wr_lo = (wr - wr_hi.astype(F32)).astype(BF16)
    wr_cat = jnp.concatenate([wr_hi, wr_lo], axis=1)
    return pl.pallas_call(
        _route_kernel,
        grid=(m // tm,),
        in_specs=[pl.BlockSpec((tm, d), lambda i: (i, 0)),
                  pl.BlockSpec((1, d), lambda i: (0, 0)),
                  pl.BlockSpec((d, 2 * LANES), lambda i: (0, 0))],
        out_specs=[pl.BlockSpec((tm, d // 2), lambda i: (i, 0)),
                   pl.BlockSpec((tm, LANES), lambda i: (i, 0))],
        out_shape=[jax.ShapeDtypeStruct((m, d // 2), jnp.uint32),
                   jax.ShapeDtypeStruct((m, LANES), F32)],
        compiler_params=_cparams(("parallel",), V7X_VMEM_LIMIT),
        name="moe_route",
    )(x2, norm_g.reshape(1, d), wr_cat)


def _ffn_kernel(blk_e_ref, seg_ref, nxt_e_ref, nused_ref,
                src_cur_ref, src_nxt_ref, wg_hbm, wu_hbm, wd_hbm, h_hbm,
                y_ref,
                wgs_ref, wus_ref, wds_ref, wgb_ref, wub_ref, wdb_ref, xbuf_ref, gsem, wsem, *, tb, layer):
    i = pl.program_id(0)
    n_used = nused_ref[0]
    slot = i % 2

    def weight_copies(e, ws):
        return [pltpu.make_async_copy(hbm.at[layer, e], stage.at[ws], wsem.at[ws])
                for hbm, stage in ((wg_hbm, wgs_ref), (wu_hbm, wus_ref), (wd_hbm, wds_ref))]

    def gather_copy(src_ref, r, s):
        return pltpu.make_async_copy(h_hbm.at[pl.ds(src_ref[0, 0, r], 1)],
                                     xbuf_ref.at[s, pl.ds(r, 1)], gsem.at[s])

    def start_gather(src_ref, s):
        for r in range(tb):
            gather_copy(src_ref, r, s).start(priority=1)

    def wait_gather(s):
        pltpu.make_async_copy(h_hbm.at[pl.ds(0, tb)], xbuf_ref.at[s], gsem.at[s]).wait()

    @pl.when(i == 0)
    def _():
        for cp in weight_copies(blk_e_ref[0], 0):
            cp.start()
        start_gather(src_cur_ref, 0)

    @pl.when(i >= n_used)
    def _():
        y_ref[...] = jnp.zeros(y_ref.shape, y_ref.dtype)

    @pl.when(i < n_used)
    def _():
        @pl.when(i + 1 < n_used)
        def _():
            start_gather(src_nxt_ref, 1 - slot)

        first = jnp.logical_or(i == 0, blk_e_ref[i] != blk_e_ref[jnp.maximum(i - 1, 0)])

        @pl.when(first)
        def _():
            ws = seg_ref[i] % 2
            for cp in weight_copies(blk_e_ref[i], ws):
                cp.wait()
            wgb_ref[...] = wgs_ref[ws].astype(BF16)
            wub_ref[...] = wus_ref[ws].astype(BF16)
            wdb_ref[...] = wds_ref[ws].astype(BF16)
            nxt = nxt_e_ref[i]

            @pl.when(nxt >= 0)
            def _():
                for cp in weight_copies(nxt, 1 - ws):
                    cp.start()

        wait_gather(slot)
        x_lo, x_hi = _unpack_bf16_pairs(xbuf_ref[slot])
        xb = jnp.concatenate([x_lo, x_hi], axis=1).astype(BF16)
        hid = (jax.nn.silu(jnp.dot(xb, wgb_ref[...], preferred_element_type=F32))
               * jnp.dot(xb, wub_ref[...], preferred_element_type=F32))
        y = jnp.dot(hid.astype(BF16), wdb_ref[...], preferred_element_type=F32)
        half = y.shape[1] // 2
        y_ref[...] = _pack_bf16_pairs(y[:, :half], y[:, half:])


def _moe_ffn(h, route, w_gate, w_up, w_down, layer, tb, tm):
    n = h.shape[0]
    _, e, d, f = w_gate.shape
    k = MOE_TOPK
    mcopies = n * k
    flat_e = route[:, 0:2].astype(jnp.int32).reshape(mcopies)
    se, order = lax.sort((flat_e, jnp.arange(mcopies, dtype=jnp.int32)), num_keys=1)
    counts = jnp.sum((flat_e[:, None] == jnp.arange(e)[None, :]).astype(jnp.int32), axis=0)
    padded = (counts + tb - 1) // tb * tb
    pad_end = jnp.cumsum(padded)
    pad_start = pad_end - padded
    start = jnp.cumsum(counts) - counts
    nb = (mcopies + e * (tb - 1) + tb - 1) // tb
    p = nb * tb
    blk_start = jnp.arange(nb) * tb
    blk_e = jnp.minimum(jnp.sum((blk_start[:, None] >= pad_end[None, :]).astype(jnp.int32), axis=1),
                        e - 1).astype(jnp.int32)
    blk_off = blk_start - pad_start[blk_e]
    nvalid = jnp.clip(counts[blk_e] - blk_off, 0, tb).astype(jnp.int32)
    n_used = (pad_end[-1] // tb).astype(jnp.int32).reshape(1)
    rows = jnp.arange(tb)[None, :]
    valid = rows < nvalid[:, None]
    sorted_idx = jnp.clip((start[blk_e] + blk_off)[:, None] + rows, 0, mcopies - 1)
    copy_id = order[sorted_idx]
    src3 = jnp.where(valid, copy_id // k, 0).reshape(nb, 1, tb)
    pos_sorted = pad_start[se] + jnp.arange(mcopies, dtype=jnp.int32) - start[se]
    pos = pos_sorted[jnp.argsort(order)].astype(jnp.int32)
    pos3 = pos.reshape(n // tm, tm, k).transpose(0, 2, 1).reshape(n // tm, 1, k * tm)

    change = jnp.concatenate([jnp.ones((1,), jnp.int32), (blk_e[1:] != blk_e[:-1]).astype(jnp.int32)])
    seg = (jnp.cumsum(change) - 1).astype(jnp.int32)
    owner = jnp.where(counts > 0, jnp.arange(e), e)
    later = lax.cummin(owner, axis=0, reverse=True)
    nxt_of = jnp.concatenate([later[1:], jnp.full((1,), e)])
    nxt_e = jnp.where(nxt_of[blk_e] < e, nxt_of[blk_e], -1).astype(jnp.int32)

    smem = functools.partial(pl.BlockSpec, memory_space=pltpu.SMEM)
    any_spec = pl.BlockSpec(memory_space=pl.ANY)
    grid_spec = pltpu.PrefetchScalarGridSpec(
        num_scalar_prefetch=4,
        grid=(nb,),
        in_specs=[
            smem((1, 1, tb), lambda i, *_: (i, 0, 0)),
            smem((1, 1, tb), lambda i, *_: (jnp.minimum(i + 1, nb - 1), 0, 0)),
            any_spec, any_spec, any_spec, any_spec,
        ],
        out_specs=pl.BlockSpec((tb, d // 2), lambda i, *_: (i, 0)),
        scratch_shapes=[
            pltpu.VMEM((2, d, f), F32), pltpu.VMEM((2, d, f), F32), pltpu.VMEM((2, f, d), F32),
            pltpu.VMEM((d, f), BF16), pltpu.VMEM((d, f), BF16), pltpu.VMEM((f, d), BF16),
            pltpu.VMEM((2, tb, d // 2), jnp.uint32),
            pltpu.SemaphoreType.DMA((2,)), pltpu.SemaphoreType.DMA((2,)),
        ],
    )
    y = pl.pallas_call(
        functools.partial(_ffn_kernel, tb=tb, layer=layer),
        grid_spec=grid_spec,
        out_shape=jax.ShapeDtypeStruct((p, d // 2), jnp.uint32),
        compiler_params=_cparams(("arbitrary",), V7X_VMEM_LIMIT),
        name="moe_ffn",
    )(blk_e, seg, nxt_e, n_used, src3, src3, w_gate, w_up, w_down, h)
    return y, pos3


def _combine_kernel(x_ref, r_ref, pos_cur_ref, pos_nxt_ref, y_hbm, *rest, n_norm, want_x):
    g_refs = rest[:n_norm]
    outs = rest[n_norm:-2]
    pbuf_ref, sem = rest[-2:]
    i = pl.program_id(0)
    slot = i % 2
    tm = x_ref.shape[0]

    def start_rows(pos_ref, s):
        for kk in range(MOE_TOPK):
            for r in range(tm):
                pltpu.make_async_copy(y_hbm.at[pl.ds(pos_ref[0, 0, kk * tm + r], 1)],
                                      pbuf_ref.at[s, kk, pl.ds(r, 1)], sem.at[s]).start(priority=r % 2)

    @pl.when(i == 0)
    def _():
        start_rows(pos_cur_ref, 0)

    @pl.when(i + 1 < pl.num_programs(0))
    def _():
        start_rows(pos_nxt_ref, 1 - slot)

    for kk in range(MOE_TOPK):
        pltpu.make_async_copy(y_hbm.at[pl.ds(0, tm)], pbuf_ref.at[slot, kk], sem.at[slot]).wait()

    w0 = r_ref[:, 2:3]
    w1 = r_ref[:, 3:4]
    lo0, hi0 = _unpack_bf16_pairs(pbuf_ref[slot, 0])
    lo1, hi1 = _unpack_bf16_pairs(pbuf_ref[slot, 1])
    x = x_ref[...] + jnp.concatenate([lo0 * w0 + lo1 * w1, hi0 * w0 + hi1 * w1], axis=1)
    oi = 0
    if want_x:
        outs[0][...] = x
        oi = 1
    r = lax.rsqrt(jnp.mean(x * x, axis=-1, keepdims=True) + RMS_EPS)
    y = x * r
    for g_ref, o_ref in zip(g_refs, outs[oi:]):
        o_ref[...] = (y * g_ref[...]).astype(o_ref.dtype)


def _combine_tile(m):
    return _tile(m, 256)


def _combine(x2, route, y_sorted, pos3, gains, out_dtypes, want_x):
    m, d = x2.shape
    tm = _combine_tile(m)
    nrow = m // tm
    smem = functools.partial(pl.BlockSpec, memory_space=pltpu.SMEM)
    in_specs = [pl.BlockSpec((tm, d), lambda i: (i, 0)),
                pl.BlockSpec((tm, LANES), lambda i: (i, 0)),
                smem((1, 1, MOE_TOPK * tm), lambda i: (i, 0, 0)),
                smem((1, 1, MOE_TOPK * tm), lambda i: (jnp.minimum(i + 1, nrow - 1), 0, 0)),
                pl.BlockSpec(memory_space=pl.ANY)]
    in_specs += [pl.BlockSpec((1, d), lambda i: (0, 0)) for _ in gains]
    out_shape = ([jax.ShapeDtypeStruct((m, d), F32)] if want_x else []) + [
        jax.ShapeDtypeStruct((m, d), dt) for dt in out_dtypes]
    out_specs = [pl.BlockSpec((tm, d), lambda i: (i, 0)) for _ in out_shape]
    return pl.pallas_call(
        functools.partial(_combine_kernel, n_norm=len(gains), want_x=want_x),
        grid=(nrow,),
        in_specs=in_specs,
        out_specs=out_specs,
        out_shape=out_shape,
        scratch_shapes=[pltpu.VMEM((2, MOE_TOPK, tm, d // 2), jnp.uint32), pltpu.SemaphoreType.DMA((2,))],
        compiler_params=_cparams(("arbitrary",), V7X_VMEM_LIMIT),
        name="moe_combine",
    )(x2, route, pos3, pos3, y_sorted, *[g.reshape(1, d) for g in gains])


def _t5_bucket(dist):
    n = jnp.maximum(dist, 0)
    exact = REL_BUCKETS // 2
    nf = jnp.maximum(n, 1).astype(F32)
    large = exact + (jnp.log(nf / exact) / math.log(REL_MAX_DIST / exact)
                     * (REL_BUCKETS - exact)).astype(jnp.int32)
    return jnp.where(n < exact, n, jnp.minimum(large, REL_BUCKETS - 1))


def _bias_lookup(table, bucket):
    onehot = jax.nn.one_hot(bucket, REL_BUCKETS, dtype=F32)
    return jnp.einsum('...b,bh->...h', onehot, table, precision=lax.Precision.HIGHEST)


def _cmp_kernel(rk_ref, rv_ref, pe_ref, w1_ref, w2_ref, o_ref):
    half = w1_ref.shape[1] // 2
    for s, r_ref in enumerate((rk_ref, rv_ref)):
        r = r_ref[0, 0].astype(F32)
        ra = (r + pe_ref[s, :, :half]).astype(BF16)
        rb = (r + pe_ref[s, :, half:]).astype(BF16)
        ha = jnp.dot(ra, w1_ref[s, :half, :], preferred_element_type=F32)
        hb = jnp.dot(rb, w1_ref[s, half:, :], preferred_element_type=F32)
        hb_next = jnp.concatenate([hb[1:], hb[:1]], axis=0)
        hid = jax.nn.gelu(ha + hb_next)
        o_ref[0, 0, s] = jnp.dot(hid.astype(BF16), w2_ref[s], preferred_element_type=F32).astype(o_ref.dtype)


def _compress(kv, batch, seq, pes, w1s, w2s):
    g, dk = NSA_GROUPS, NSA_DK
    nr = seq // CMP_STRIDE
    assert nr == dk
    ct = kv[:, :2 * g * dk].reshape(batch, nr, CMP_STRIDE, 2, g, dk)
    r = ct.transpose(3, 0, 4, 1, 2, 5).reshape(2, batch, g, nr, CMP_STRIDE * dk)
    pe = jnp.stack(pes).reshape(2, 1, CMP_LEN * dk)
    w1 = jnp.stack(w1s).reshape(2, CMP_LEN * dk, dk).astype(BF16)
    w2 = jnp.stack(w2s).astype(BF16)
    rspec = pl.BlockSpec((1, 1, nr, CMP_STRIDE * dk), lambda b, gi: (b, gi, 0, 0))
    return pl.pallas_call(
        _cmp_kernel,
        grid=(batch, g),
        in_specs=[rspec, rspec,
                  pl.BlockSpec((2, 1, CMP_LEN * dk), lambda b, gi: (0, 0, 0)),
                  pl.BlockSpec((2, CMP_LEN * dk, dk), lambda b, gi: (0, 0, 0)),
                  pl.BlockSpec((2, dk, dk), lambda b, gi: (0, 0, 0))],
        out_specs=pl.BlockSpec((1, 1, 2, nr, dk), lambda b, gi: (b, gi, 0, 0, 0)),
        out_shape=jax.ShapeDtypeStruct((batch, g, 2, nr, dk), BF16),
        compiler_params=_cparams(("parallel", "parallel"), V7X_VMEM_LIMIT),
        name="nsa_compress",
    )(r[0], r[1], pe, w1, w2)


BIAS_DIAG = 0
BIAS_NEXT = 1
BIAS_CONST = 2
BIAS_WIN_EDGE = 3
BIAS_MASKED = 4
N_BIAS_TILES = 5
CMP_PAD = 16
CMP_BAND = CMP_PAD + LANES // CMP_STRIDE
FAR_GROUP = 4


def _nsa_kernel(q_ref, gate_ref, kvc_ref, ks_ref, vs_ref, kw_ref, vw_ref, biasc_ref, bias_ref,
                ovt_ref, ext_ref, o_ref,
                selpen_ref, m_ref, l_ref, acc_ref, sfar_ref, snear_ref, oc_ref, ow_ref, scmp_ref,
                *, hg, tq, n_cmp, n_sblk):
    qb = pl.program_id(2)
    dk = NSA_DK
    nt_dims = (((1,), (1,)), ((), ()))
    q8 = jnp.concatenate([q_ref[:, h * dk:(h + 1) * dk] for h in range(hg)], axis=0)
    kj = lax.broadcasted_iota(jnp.int32, (LANES, tq), 0)
    qi = lax.broadcasted_iota(jnp.int32, (LANES, tq), 1)
    t_abs = qb * tq + qi

    def tile_h(a):
        return jnp.concatenate([a] * hg, axis=1)

    win_tiles = WINDOW // LANES
    n_near = win_tiles + 1
    lo = jnp.maximum(qb - win_tiles, 0)
    c_far = bias_ref[BIAS_CONST, 0:1, :]

    def scores(k_ref, row0, n, addend):
        return lax.dot_general(k_ref[pl.ds(row0, n), :], q8, nt_dims, preferred_element_type=F32) + addend

    def colmax(sc):
        return jnp.max(sc, axis=0, keepdims=True)

    def pv(v_ref, row0, n, p):
        return lax.dot_general(v_ref[pl.ds(row0, n), :], p.astype(BF16), (((0,), (0,)), ((), ())),
                               preferred_element_type=F32)

    near_row0, near_ok = [], []
    for j in range(n_near):
        kb = qb - (win_tiles - j)
        near_row0.append(pl.multiple_of(jnp.maximum(kb, 0) * LANES, LANES))
        near_ok.append(kb >= 0)

    def near_bias(j, window_branch):
        d = win_tiles - j
        tile = BIAS_DIAG if d == 0 else (BIAS_NEXT if d == 1 else BIAS_CONST)
        if window_branch and d == win_tiles:
            tile = BIAS_WIN_EDGE
        return bias_ref[jnp.where(near_ok[j], tile, BIAS_MASKED)]

    m_w = jnp.full(c_far.shape, 0.5 * NEG, F32)
    for j in range(n_near):
        s_w = scores(kw_ref, near_row0[j], LANES, near_bias(j, True))
        snear_ref[1, j] = s_w
        m_w = jnp.maximum(m_w, colmax(s_w))
    l_w = jnp.zeros(c_far.shape, F32)
    acc_w = jnp.zeros((dk, hg * tq), F32)
    for j in range(n_near):
        p_w = jnp.exp2(snear_ref[1, j] - m_w)
        l_w = l_w + jnp.sum(p_w, axis=0, keepdims=True)
        acc_w = acc_w + pv(vw_ref, near_row0[j], LANES, p_w)
    ow_ref[...] = acc_w * (1.0 / jnp.maximum(l_w, 1e-30))

    s = lax.dot_general(kvc_ref[0], q8, nt_dims, preferred_element_type=F32) + c_far
    scmp_ref[pl.ds(0, CMP_PAD), :] = jnp.zeros((CMP_PAD, hg * tq), F32)
    scmp_ref[pl.ds(CMP_PAD, LANES), :] = s
    band0 = pl.multiple_of(qb * (tq // CMP_STRIDE), 8)
    scmp_ref[pl.ds(band0, CMP_BAND), :] = scmp_ref[pl.ds(band0, CMP_BAND), :] + biasc_ref[...]
    s = scmp_ref[pl.ds(CMP_PAD, LANES), :]
    valid_c = (kj * CMP_STRIDE + (CMP_LEN - 1) <= t_abs) & (kj < n_cmp)
    s = s + tile_h(jnp.where(valid_c, 0.0, NEG))
    mx = jnp.maximum(jnp.max(s, axis=0, keepdims=True), 0.5 * NEG)
    p = jnp.exp2(s - mx)
    p = p * (1.0 / jnp.maximum(jnp.sum(p, axis=0, keepdims=True), 1e-30))
    oc_ref[...] = lax.dot_general(kvc_ref[1], p.astype(BF16), (((0,), (0,)), ((), ())),
                                  preferred_element_type=F32)
    psum = p[:, 0:tq]
    for h in range(1, hg):
        psum = psum + p[:, h * tq:(h + 1) * tq]
    p_hi = psum.astype(BF16)
    p_lo = (psum - p_hi.astype(F32)).astype(BF16)
    imp = (jnp.dot(ovt_ref[...], p_hi, preferred_element_type=F32)
           + jnp.dot(ovt_ref[...], p_lo, preferred_element_type=F32))

    imp = imp[:n_sblk]
    kb_i = lax.broadcasted_iota(jnp.int32, (n_sblk, tq), 0)
    t_b = qb * tq + lax.broadcasted_iota(jnp.int32, (n_sblk, tq), 1)
    cur = lax.shift_right_logical(t_b, int(math.log2(SEL_BLOCK)))
    valid_b = kb_i * SEL_BLOCK <= t_b
    forced = (kb_i == 0) | (kb_i == cur) | (kb_i == cur - 1)
    score = jnp.where(valid_b, jnp.where(forced, FORCED_SCORE, imp), INVALID_SCORE)
    rank = jnp.zeros((n_sblk, tq), F32)
    for j in range(n_sblk):
        cj = score[j:j + 1, :]
        ahead = (cj > score) | ((cj == score) & (kb_i > j))
        rank = rank + jnp.where(ahead, 1.0, 0.0)
    sel = jnp.where(rank < min(SEL_TOPK, n_sblk), 1.0, 0.0)
    if n_sblk < LANES:
        sel = jnp.concatenate([sel, jnp.zeros((LANES - n_sblk, tq), F32)], axis=0)
    selmask = jnp.dot(ext_ref[...], sel.astype(BF16), preferred_element_type=F32)
    selpen_ref[...] = ((selmask - 1.0) * (-NEG)).reshape(selpen_ref.shape)

    m_near = jnp.full(c_far.shape, 0.5 * NEG, F32)
    for j in range(n_near):
        kb_c = lax.shift_right_logical(near_row0[j], int(math.log2(LANES)))
        s_s = scores(ks_ref, near_row0[j], LANES, near_bias(j, False) + tile_h(selpen_ref[kb_c]))
        snear_ref[0, j] = s_s
        m_near = jnp.maximum(m_near, colmax(s_s))

    assert FAR_GROUP - 1 <= win_tiles
    n_grp = (lo + FAR_GROUP - 1) // FAR_GROUP
    m_ref[...] = jnp.full(m_ref.shape, 0.5 * NEG, F32)

    def far_scores(gi, carry):
        kb = FAR_GROUP * gi
        pen = jnp.concatenate(
            [selpen_ref[kb + t] + jnp.where(kb + t < lo, 0.0, NEG) for t in range(FAR_GROUP)], axis=0)
        sc = scores(ks_ref, pl.multiple_of(kb * LANES, LANES), FAR_GROUP * LANES, tile_h(pen))
        sfar_ref[gi] = sc
        m_ref[...] = jnp.maximum(m_ref[...], colmax(sc))
        return carry
    lax.fori_loop(0, n_grp, far_scores, 0)

    m_s = jnp.maximum(m_near, m_ref[...] + c_far)
    l_s = jnp.zeros(m_s.shape, F32)
    acc_s = jnp.zeros((dk, hg * tq), F32)
    for j in range(n_near):
        p_s = jnp.exp2(snear_ref[0, j] - m_s)
        l_s = l_s + jnp.sum(p_s, axis=0, keepdims=True)
        acc_s = acc_s + pv(vs_ref, near_row0[j], LANES, p_s)
    l_ref[...] = l_s
    acc_ref[...] = acc_s

    m_far = m_s - c_far

    def far_pv(gi, carry):
        row0 = pl.multiple_of(FAR_GROUP * gi * LANES, LANES)
        p = jnp.exp2(sfar_ref[gi] - m_far)
        l_ref[...] = l_ref[...] + jnp.sum(p, axis=0, keepdims=True)
        acc_ref[...] = acc_ref[...] + pv(vs_ref, row0, FAR_GROUP * LANES, p)
        return carry
    lax.fori_loop(0, n_grp, far_pv, 0)

    o_s = acc_ref[...] * (1.0 / jnp.maximum(l_ref[...], 1e-30))
    o_w = ow_ref[...]
    o_c = oc_ref[...]
    gt = gate_ref[...].T
    for h in range(hg):
        cols = slice(h * tq, (h + 1) * tq)
        out_t = (gt[3 * h:3 * h + 1] * o_c[:, cols] + gt[3 * h + 1:3 * h + 2] * o_s[:, cols]
                 + gt[3 * h + 2:3 * h + 3] * o_w[:, cols])
        o_ref[:, h * dk:(h + 1) * dk] = out_t.T.astype(o_ref.dtype)


def _nsa_attention(q, gates, kv, kvc, rel_bias, batch, seq, heads):
    g, dk = NSA_GROUPS, NSA_DK
    hg = heads // g
    tq = LANES
    nqb = seq // tq
    n_cmp = (seq - CMP_LEN) // CMP_STRIDE + 1
    n_sblk = seq // SEL_BLOCK
    nr = seq // CMP_STRIDE
    assert nr == LANES and n_sblk <= LANES and n_sblk % 8 == 0 and tq == LANES
    nt = seq // LANES
    rows = hg * tq
    table = rel_bias.astype(F32) * LOG2E
    assert (REL_MAX_DIST + CMP_LEN - 1 + CMP_STRIDE - 1) // CMP_STRIDE <= CMP_PAD and tq % CMP_STRIDE == 0
    band_w = jnp.arange(CMP_BAND)[:, None]
    dist_c = jnp.arange(tq)[None, :] - (band_w - CMP_PAD) * CMP_STRIDE - (CMP_LEN - 1)
    bias_c = _bias_lookup(table, _t5_bucket(dist_c)) - table[REL_BUCKETS - 1]
    bias_c = bias_c.reshape(CMP_BAND, tq, g, hg).transpose(2, 0, 3, 1).reshape(g, CMP_BAND, rows)
    ii = jnp.arange(tq)
    n_far = -(-(REL_MAX_DIST + tq - 1) // tq)
    dist = jnp.arange(n_far + 1)[:, None, None] * tq + ii[None, None, :] - ii[None, :, None]
    bias_t = _bias_lookup(table, _t5_bucket(dist))
    bias_t = bias_t.reshape(n_far + 1, tq, tq, g, hg).transpose(0, 3, 1, 4, 2).reshape(n_far + 1, g, tq, rows)
    assert n_far == BIAS_CONST and WINDOW % tq == 0 and WINDOW // tq >= n_far
    key_j = jnp.arange(tq)[:, None]
    qry_i = jnp.tile(jnp.arange(tq), hg)[None, :]
    future = key_j > qry_i
    bias_t = jnp.stack([
        jnp.where(future, NEG, bias_t[0]),
        bias_t[1],
        bias_t[2],
        jnp.where(future, bias_t[2], NEG),
        jnp.full_like(bias_t[2], NEG),
    ])
    cs = np.arange(LANES) * CMP_STRIDE
    ss = np.arange(LANES) * SEL_BLOCK
    ov = np.clip(np.minimum(cs[:, None] + CMP_LEN, ss[None, :] + SEL_BLOCK)
                 - np.maximum(cs[:, None], ss[None, :]), 0, None) / CMP_LEN
    ov[n_cmp:, :] = 0.0
    ov[:, n_sblk:] = 0.0
    overlap_t = jnp.asarray(ov.T, BF16)
    ex = ((np.arange(seq)[:, None] // SEL_BLOCK) == np.arange(LANES)[None, :]).astype(np.float32)
    expand_t = jnp.asarray(ex, BF16)

    kspec = lambda blk: pl.BlockSpec((seq, dk), lambda b, gi, qb: (b, blk * g + gi))
    return pl.pallas_call(
        functools.partial(_nsa_kernel, hg=hg, tq=tq, n_cmp=n_cmp, n_sblk=n_sblk),
        grid=(batch, g, nqb),
        in_specs=[
            pl.BlockSpec((tq, hg * dk), lambda b, gi, qb: (b * nqb + qb, gi)),
            pl.BlockSpec((tq, LANES), lambda b, gi, qb: (b * nqb + qb, gi)),
            pl.BlockSpec((None, None, 2, nr, dk), lambda b, gi, qb: (b, gi, 0, 0, 0)),
            kspec(2), kspec(3), kspec(4), kspec(5),
            pl.BlockSpec((None, CMP_BAND, rows), lambda b, gi, qb: (gi, 0, 0)),
            pl.BlockSpec((N_BIAS_TILES, None, tq, rows), lambda b, gi, qb: (0, gi, 0, 0)),
            pl.BlockSpec((LANES, LANES), lambda b, gi, qb: (0, 0)),
            pl.BlockSpec((seq, LANES), lambda b, gi, qb: (0, 0)),
        ],
        out_specs=pl.BlockSpec((tq, hg * dk), lambda b, gi, qb: (b * nqb + qb, gi)),
        out_shape=jax.ShapeDtypeStruct((batch * seq, heads * dk), BF16),
        scratch_shapes=[pltpu.VMEM((nt, LANES, tq), F32),
                        pltpu.VMEM((1, rows), F32), pltpu.VMEM((1, rows), F32),
                        pltpu.VMEM((dk, rows), F32),
                        pltpu.VMEM((-(-nt // FAR_GROUP), FAR_GROUP * LANES, rows), F32),
                        pltpu.VMEM((2, WINDOW // LANES + 1, LANES, rows), F32),
                        pltpu.VMEM((dk, rows), F32), pltpu.VMEM((dk, rows), F32),
                        pltpu.VMEM((CMP_PAD + LANES, rows), F32)],
        compiler_params=_cparams(("parallel", "parallel", "arbitrary"), V7X_VMEM_LIMIT),
        name="nsa_attention",
    )(q, gates, kvc, kv, kv, kv, kv, bias_c, bias_t, overlap_t, expand_t)


def _nsa_layer(x2, h_kv, h_q, w_kv, cmp_params, rel_bias, w_q, w_o, layer, batch, seq):
    m, d = x2.shape
    g, dk = NSA_GROUPS, NSA_DK
    heads = d // dk
    hg = heads // g
    tm = _tile(seq, 1024)
    kv = _matmul(h_kv, w_kv[None], 0, col0=0, ncols=6 * g * dk, tm=tm, tn=_tile(6 * g * dk, 512),
                 out_dtype=BF16, epilogue=_ep_store, name="nsa_proj_kv")
    (k_pe, k_w1, k_w2, v_pe, v_w1, v_w2) = cmp_params
    kvc = _compress(kv, batch, seq, (k_pe, v_pe), (k_w1, v_w1), (k_w2, v_w2))
    q = _matmul(h_q, w_q, layer, col0=0, ncols=heads * dk, tm=tm, tn=_tile(heads * dk, 512),
                out_dtype=BF16, epilogue=functools.partial(_ep_scale, scale=dk ** -0.5 * LOG2E),
                name="nsa_proj_q")
    wg = w_q[layer, :, heads * dk:].reshape(d, g, hg * 3)
    wg = jnp.pad(wg, ((0, 0), (0, 0), (0, LANES - hg * 3))).reshape(1, d, g * LANES)
    gates = _matmul(h_q, wg, 0, col0=0, ncols=g * LANES, tm=tm, tn=g * LANES, out_dtype=F32,
                    epilogue=_ep_sigmoid, name="nsa_proj_gate")
    o = _nsa_attention(q, gates, kv, kvc, rel_bias, batch, seq, heads)
    tn_o = _tile(d, 512)
    return _matmul(o, w_o, layer, col0=0, ncols=d, tm=tm, tn=tn_o, out_dtype=F32, epilogue=_ep_residual,
                   extra=((x2, (tm, tn_o), lambda j, i: (i, j)),), name="nsa_out_proj")


def kernel(x, ret_norm_g, ret_w_in, ret_w_o, kv_norm_g, nsa_w_kv, cmp_k_pe, cmp_k_w1, cmp_k_w2, cmp_v_pe, cmp_v_w1, cmp_v_w2, rel_bias, nsa_norm_g, nsa_w_q, nsa_w_o, moe_norm_g, moe_w_router_group, moe_w_router_expert, moe_w_gate, moe_w_up, moe_w_down, final_norm_g):
    batch, seq, d = x.shape
    depth = moe_norm_g.shape[0]
    n_a = ret_norm_g.shape[0]
    x2 = x.reshape(batch * seq, d)
    cmp_params = (cmp_k_pe, cmp_k_w1, cmp_k_w2, cmp_v_pe, cmp_v_w1, cmp_v_w2)
    h_kv = h_q = None
    out = None
    for layer in range(depth):
        if layer < n_a:
            x2 = _retention_layer(x2, ret_norm_g[layer], ret_w_in, ret_w_o, layer, batch, seq)
        else:
            j = layer - n_a
            if h_q is None:
                h_kv = _rmsnorm(x2, kv_norm_g)
                h_q = _rmsnorm(x2, nsa_norm_g[j])
            x2 = _nsa_layer(x2, h_kv, h_q, nsa_w_kv, cmp_params, rel_bias, nsa_w_q, nsa_w_o, j,
                            batch, seq)
        hm, route = _route(x2, moe_norm_g[layer], moe_w_router_group[layer], moe_w_router_expert[layer])
        ys, pos = _moe_ffn(hm, route, moe_w_gate, moe_w_up, moe_w_down, layer, tb=MOE_ROWS,
                           tm=_combine_tile(batch * seq))
        if layer == depth - 1:
            (out,) = _combine(x2, route, ys, pos, [final_norm_g], [F32], want_x=False)
        elif layer + 1 == n_a:
            j = layer + 1 - n_a
            x2, h_kv, h_q = _combine(x2, route, ys, pos, [kv_norm_g, nsa_norm_g[j]], [BF16, BF16],
                                     want_x=True)
        elif layer + 1 < n_a:
            (x2,) = _combine(x2, route, ys, pos, [], [], want_x=True)
        else:
            j = layer + 1 - n_a
            x2, h_q = _combine(x2, route, ys, pos, [nsa_norm_g[j]], [BF16], want_x=True)
    return out.reshape(batch, seq, d)
```

```python
import functools
import math

import numpy as np
import jax
import jax.numpy as jnp
from jax import lax
from jax.experimental import pallas as pl
from jax.experimental.pallas import tpu as pltpu

RMS_EPS = 1e-6
RET_DK = 256
RET_DV = 2 * RET_DK
ROPE_BASE = 10000.0
NSA_DK = 128
NSA_GROUPS = 4
CMP_LEN = 32
CMP_STRIDE = 16
SEL_BLOCK = 64
SEL_TOPK = 16
WINDOW = 512
FORCED_SCORE = 1e4
INVALID_SCORE = -1e9
REL_BUCKETS = 32
REL_MAX_DIST = 128
MOE_GROUPS = 8
MOE_EPG = 8
MOE_EXPERTS = MOE_GROUPS * MOE_EPG
MOE_TOPK = 2
MOE_ROWS = 256
MOE_W_SPLIT = 4

LANES = 128
V7X_VMEM_LIMIT = 56 * 1024 * 1024
NEG = -1e30
LOG2E = math.log2(math.e)

BF16 = jnp.bfloat16
F32 = jnp.float32


def _tile(dim, pref):
    t = min(dim, pref)
    while dim % t:
        t //= 2
    return t


def _cparams(sem, vmem=None):
    return pltpu.CompilerParams(dimension_semantics=sem, vmem_limit_bytes=vmem)


def _norm_kernel(x_ref, g_ref, o_ref):
    x = x_ref[...]
    y = x * lax.rsqrt(jnp.mean(x * x, axis=-1, keepdims=True) + RMS_EPS)
    o_ref[...] = (y * g_ref[...]).astype(o_ref.dtype)


def _rmsnorm(x2, g, out_dtype=BF16):
    m, d = x2.shape
    tm = _tile(m, 256)
    return pl.pallas_call(
        _norm_kernel,
        grid=(m // tm,),
        in_specs=[pl.BlockSpec((tm, d), lambda i: (i, 0)),
                  pl.BlockSpec((1, d), lambda i: (0, 0))],
        out_specs=pl.BlockSpec((tm, d), lambda i: (i, 0)),
        out_shape=jax.ShapeDtypeStruct((m, d), out_dtype),
        compiler_params=_cparams(("parallel",), V7X_VMEM_LIMIT),
        name="rmsnorm",
    )(x2, g.reshape(1, d))


MM_CHUNK = 256


def _mm_kernel(a_ref, w_ref, *rest, n_extra, epilogue, cast_w):
    extra = rest[:n_extra]
    o_ref = rest[n_extra]
    if cast_w:
        wbf_ref = rest[n_extra + 1]

        @pl.when(pl.program_id(1) == 0)
        def _():
            wbf_ref[...] = w_ref[...].astype(BF16)
    else:
        wbf_ref = w_ref

    a = a_ref[...]
    tn = o_ref.shape[1]
    chunk = min(tn, MM_CHUNK)
    for c0 in range(0, tn, chunk):
        cols = slice(c0, c0 + chunk)
        acc = jnp.dot(a, wbf_ref[:, cols], preferred_element_type=F32)
        epilogue(acc, o_ref, cols, *extra)


def _matmul(a, w, layer, *, col0, ncols, tm, tn, out_dtype, epilogue, extra=(), name):
    m, k = a.shape
    assert col0 % tn == 0 and ncols % tn == 0 and m % tm == 0
    joff = col0 // tn
    cast_w = w.dtype != BF16
    in_specs = [pl.BlockSpec((tm, k), lambda j, i: (i, 0)),
                pl.BlockSpec((None, k, tn), lambda j, i: (layer, 0, j + joff))]
    args = [a, w]
    for arr, bshape, imap in extra:
        in_specs.append(pl.BlockSpec(bshape, imap))
        args.append(arr)
    return pl.pallas_call(
        functools.partial(_mm_kernel, n_extra=len(extra), epilogue=epilogue, cast_w=cast_w),
        grid=(ncols // tn, m // tm),
        in_specs=in_specs,
        out_specs=pl.BlockSpec((tm, tn), lambda j, i: (i, j)),
        out_shape=jax.ShapeDtypeStruct((m, ncols), out_dtype),
        scratch_shapes=[pltpu.VMEM((k, tn), BF16)] if cast_w else [],
        compiler_params=_cparams(("arbitrary", "arbitrary"), V7X_VMEM_LIMIT),
        name=name,
    )(*args)


def _cast_kernel(w_ref, o_ref):
    o_ref[...] = w_ref[...].astype(o_ref.dtype)


def _cast_bf16(w, layer):
    _, k, n = w.shape
    tk = _tile(k, 512)
    return pl.pallas_call(
        _cast_kernel,
        grid=(k // tk,),
        in_specs=[pl.BlockSpec((None, tk, n), lambda i: (layer, i, 0))],
        out_specs=pl.BlockSpec((None, tk, n), lambda i: (0, i, 0)),
        out_shape=jax.ShapeDtypeStruct((1, k, n), BF16),
        compiler_params=_cparams(("parallel",), V7X_VMEM_LIMIT),
        name="cast_bf16",
    )(w)


def _ep_store(acc, o_ref, cols):
    o_ref[:, cols] = acc.astype(o_ref.dtype)


def _ep_residual(acc, o_ref, cols, r_ref):
    o_ref[:, cols] = (r_ref[:, cols] + acc).astype(o_ref.dtype)


def _ep_scale(acc, o_ref, cols, *, scale):
    o_ref[:, cols] = (acc * scale).astype(o_ref.dtype)


def _ep_sigmoid(acc, o_ref, cols):
    o_ref[:, cols] = jax.nn.sigmoid(acc).astype(o_ref.dtype)


def _ep_rotary(acc, o_ref, cols, cos_ref, sin_ref, *, k_first_tile, k_scale):
    assert acc.shape[1] == RET_DK
    scale = jnp.where(pl.program_id(0) >= k_first_tile, k_scale, 1.0).astype(F32)
    cos = cos_ref[...] * scale
    sin = sin_ref[...] * scale
    half = RET_DK // 2
    x1 = acc[:, :half]
    x2 = acc[:, half:]
    o_ref[:, cols.start:cols.start + half] = (x1 * cos - x2 * sin).astype(o_ref.dtype)
    o_ref[:, cols.start + half:cols.stop] = (x1 * sin + x2 * cos).astype(o_ref.dtype)


def _ret_kernel(dchunk_ref, q_ref, k_ref, v_ref, g_ref, dintra_ref, dq_ref, dk_ref, y_ref, state_ref, *, hp):
    c = pl.program_id(2)

    @pl.when(c == 0)
    def _():
        state_ref[...] = jnp.zeros_like(state_ref)

    for j in range(hp):
        h = pl.program_id(1) * hp + j
        q = q_ref[:, j * RET_DK:(j + 1) * RET_DK]
        k = k_ref[:, j * RET_DK:(j + 1) * RET_DK]
        v = v_ref[:, j * RET_DV:(j + 1) * RET_DV]
        scores = lax.dot_general(q, k, (((1,), (1,)), ((), ())), preferred_element_type=F32)
        scores = scores * dintra_ref[j]
        intra = jnp.dot(scores.astype(BF16), v, preferred_element_type=F32)
        cross = jnp.dot(q, state_ref[j].astype(BF16), preferred_element_type=F32)
        dq = dq_ref[j]
        out = jnp.concatenate(
            [intra[:, n:n + LANES] + cross[:, n:n + LANES] * dq for n in range(0, RET_DV, LANES)], axis=1)
        dkc = dk_ref[j]
        kd = (k.astype(F32) * jnp.concatenate([dkc] * (RET_DK // LANES), axis=1)).astype(BF16)
        state_ref[j] = dchunk_ref[h] * state_ref[j] + lax.dot_general(
            kd, v, (((0,), (0,)), ((), ())), preferred_element_type=F32)
        out = out * lax.rsqrt(jnp.mean(out * out, axis=-1, keepdims=True) + RMS_EPS)
        gate = g_ref[:, j * RET_DV:(j + 1) * RET_DV].astype(F32)
        y_ref[:, j * RET_DV:(j + 1) * RET_DV] = (jax.nn.silu(gate) * out).astype(y_ref.dtype)


def _retention_core(qk, vg, batch, seq, heads):
    m = batch * seq
    c = _tile(seq, 256)
    nc = seq // c
    log_gamma = jnp.log1p(-jnp.exp2(-5.0 - jnp.arange(heads, dtype=F32)))
    idx = jnp.arange(c, dtype=F32)
    diff = idx[:, None] - idx[None, :]
    dintra = jnp.where(diff >= 0, jnp.exp(log_gamma[:, None, None] * jnp.maximum(diff, 0.0)), 0.0)
    dq = jnp.broadcast_to(jnp.exp(log_gamma[:, None] * (idx + 1.0))[:, :, None], (heads, c, LANES))
    dk = jnp.broadcast_to(jnp.exp(log_gamma[:, None] * (c - 1.0 - idx))[:, :, None], (heads, c, LANES))
    dchunk = jnp.exp(log_gamma * c)
    hp = next(n for n in (4, 2, 1) if heads % n == 0)
    ng = heads // hp
    grid_spec = pltpu.PrefetchScalarGridSpec(
        num_scalar_prefetch=1,
        grid=(batch, ng, nc),
        in_specs=[
            pl.BlockSpec((c, hp * RET_DK), lambda b, h, i, s: (b * nc + i, h)),
            pl.BlockSpec((c, hp * RET_DK), lambda b, h, i, s: (b * nc + i, ng + h)),
            pl.BlockSpec((c, hp * RET_DV), lambda b, h, i, s: (b * nc + i, h)),
            pl.BlockSpec((c, hp * RET_DV), lambda b, h, i, s: (b * nc + i, ng + h)),
            pl.BlockSpec((hp, c, c), lambda b, h, i, s: (h, 0, 0)),
            pl.BlockSpec((hp, c, LANES), lambda b, h, i, s: (h, 0, 0)),
            pl.BlockSpec((hp, c, LANES), lambda b, h, i, s: (h, 0, 0)),
        ],
        out_specs=pl.BlockSpec((c, hp * RET_DV), lambda b, h, i, s: (b * nc + i, h)),
        scratch_shapes=[pltpu.VMEM((hp, RET_DK, RET_DV), F32)],
    )
    return pl.pallas_call(
        functools.partial(_ret_kernel, hp=hp),
        grid_spec=grid_spec,
        out_shape=jax.ShapeDtypeStruct((m, heads * RET_DV), BF16),
        compiler_params=_cparams(("parallel", "parallel", "arbitrary"), V7X_VMEM_LIMIT),
        name="retention_core",
    )(dchunk, qk, qk, vg, vg, dintra, dq, dk)


def _retention_layer(x2, norm_g, w_in, w_o, layer, batch, seq):
    m, d = x2.shape
    heads = d // RET_DK
    hq = heads * RET_DK
    h = _rmsnorm(x2, norm_g)
    pos = jnp.arange(seq, dtype=F32)
    half = RET_DK // 2
    inv = 1.0 / (ROPE_BASE ** (jnp.arange(half, dtype=F32) / half))
    ang = pos[:, None] * inv[None, :]
    cos, sin = jnp.cos(ang), jnp.sin(ang)
    tm = _tile(seq, 1024)
    tn = _tile(hq, 512)
    nrow = seq // tm
    rot = functools.partial(_ep_rotary, k_first_tile=hq // tn, k_scale=RET_DK ** -0.5)
    qk = _matmul(h, w_in, layer, col0=0, ncols=2 * hq, tm=tm, tn=tn, out_dtype=BF16, epilogue=rot,
                 extra=((cos, (tm, half), lambda j, i: (i % nrow, 0)),
                        (sin, (tm, half), lambda j, i: (i % nrow, 0))),
                 name="ret_proj_qk")
    vg = _matmul(h, w_in, layer, col0=2 * hq, ncols=2 * heads * RET_DV, tm=tm, tn=tn, out_dtype=BF16,
                 epilogue=_ep_store, name="ret_proj_vg")
    y = _retention_core(qk, vg, batch, seq, heads)
    tm_o = _tile(seq, 512)
    tn_o = _tile(d, 512)
    return _matmul(y, _cast_bf16(w_o, layer), 0, col0=0, ncols=d, tm=tm_o, tn=tn_o, out_dtype=F32,
                   epilogue=_ep_residual, extra=((x2, (tm_o, tn_o), lambda j, i: (i, j)),),
                   name="ret_out_proj")


def _pack_bf16_pairs(lo, hi):
    lo_bits = pltpu.bitcast(lo.astype(BF16).astype(F32), jnp.uint32)
    hi_bits = pltpu.bitcast(hi.astype(BF16).astype(F32), jnp.uint32)
    return hi_bits | lax.shift_right_logical(lo_bits, jnp.uint32(16))


def _unpack_bf16_pairs(words):
    lo = pltpu.bitcast(lax.shift_left(words, jnp.uint32(16)), F32)
    hi = pltpu.bitcast(words & jnp.uint32(0xFFFF0000), F32)
    return lo, hi


def _route_kernel(x_ref, g_ref, wcat_ref, h_ref, r_ref):
    x = x_ref[...]
    hn = x * lax.rsqrt(jnp.mean(x * x, axis=-1, keepdims=True) + RMS_EPS) * g_ref[...]
    half = hn.shape[1] // 2
    h_ref[...] = _pack_bf16_pairs(hn[:, :half], hn[:, half:])
    hn_hi = hn.astype(BF16)
    hn_lo = (hn - hn_hi.astype(F32)).astype(BF16)
    both = jnp.dot(hn_hi, wcat_ref[...], preferred_element_type=F32)
    logits = (both[:, :LANES] + both[:, LANES:]
              + jnp.dot(hn_lo, wcat_ref[:, :LANES], preferred_element_type=F32))
    lane = lax.broadcasted_iota(jnp.int32, logits.shape, 1)
    is_g = lane < MOE_GROUPS
    lg = jnp.where(is_g, logits, -jnp.inf)
    mg = jnp.max(lg, axis=-1, keepdims=True)
    grp = jnp.min(jnp.where(lg == mg, lane, LANES), axis=-1, keepdims=True)
    p_grp = 1.0 / jnp.sum(jnp.where(is_g, jnp.exp(lg - mg), 0.0), axis=-1, keepdims=True)
    e_lane = lane - MOE_GROUPS
    is_e = (e_lane >= 0) & (e_lane < MOE_EXPERTS) & ((e_lane >> 3) == grp)
    le = jnp.where(is_e, logits, -jnp.inf)
    m1 = jnp.max(le, axis=-1, keepdims=True)
    i1 = jnp.min(jnp.where(le == m1, lane, LANES), axis=-1, keepdims=True)
    le2 = jnp.where(lane == i1, -jnp.inf, le)
    m2 = jnp.max(le2, axis=-1, keepdims=True)
    i2 = jnp.min(jnp.where(le2 == m2, lane, LANES), axis=-1, keepdims=True)
    t = jnp.exp(m2 - m1)
    w1 = p_grp / (1.0 + t)
    w2 = p_grp * t / (1.0 + t)
    e1 = (i1 - MOE_GROUPS).astype(F32)
    e2 = (i2 - MOE_GROUPS).astype(F32)
    r_ref[...] = jnp.where(lane == 0, e1, jnp.where(lane == 1, e2,
                           jnp.where(lane == 2, w1, jnp.where(lane == 3, w2, 0.0))))


def _route(x2, norm_g, w_rg, w_re):
    m, d = x2.shape
    tm = _tile(m, 256)
    wr = jnp.concatenate(
        [w_rg, w_re, jnp.zeros((d, LANES - MOE_GROUPS - MOE_EXPERTS), F32)], axis=1)
    wr_hi = wr.astype(BF16)
    wr_lo = (wr - wr_hi.astype(F32)).astype(BF16)
    wr_cat = jnp.concatenate([wr_hi, wr_lo], axis=1)
    return pl.pallas_call(
        _route_kernel,
        grid=(m // tm,),
        in_specs=[pl.BlockSpec((tm, d), lambda i: (i, 0)),
                  pl.BlockSpec((1, d), lambda i: (0, 0)),
                  pl.BlockSpec((d, 2 * LANES), lambda i: (0, 0))],
        out_specs=[pl.BlockSpec((tm, d // 2), lambda i: (i, 0)),
                   pl.BlockSpec((tm, LANES), lambda i: (i, 0))],
        out_shape=[jax.ShapeDtypeStruct((m, d // 2), jnp.uint32),
                   jax.ShapeDtypeStruct((m, LANES), F32)],
        compiler_params=_cparams(("parallel",), V7X_VMEM_LIMIT),
        name="moe_route",
    )(x2, norm_g.reshape(1, d), wr_cat)


def _ffn_kernel(blk_e_ref, seg_ref, nxt_e_ref, nused_ref,
                src_cur_ref, src_nxt_ref, wg_hbm, wu_hbm, wd_hbm, h_hbm,
                y_ref,
                wgs_ref, wus_ref, wds_ref, wgb_ref, wub_ref, wdb_ref, xbuf_ref, gsem, wsem, *, tb, layer):
    i = pl.program_id(0)
    n_used = nused_ref[0]
    slot = i % 2

    def weight_copies(e, ws):
        copies = []
        for hbm, stage in ((wg_hbm, wgs_ref), (wu_hbm, wus_ref), (wd_hbm, wds_ref)):
            rows = stage.shape[1] // MOE_W_SPLIT
            for c in range(MOE_W_SPLIT):
                copies.append(pltpu.make_async_copy(hbm.at[layer, e, pl.ds(c * rows, rows)],
                                                    stage.at[ws, pl.ds(c * rows, rows)], wsem.at[ws]))
        return copies

    def start_weights(e, ws):
        for n, cp in enumerate(weight_copies(e, ws)):
            cp.start(priority=n % 2)

    def gather_copy(src_ref, r, s):
        return pltpu.make_async_copy(h_hbm.at[pl.ds(src_ref[0, 0, r], 1)],
                                     xbuf_ref.at[s, pl.ds(r, 1)], gsem.at[s])

    def start_gather(src_ref, s):
        for r in range(tb):
            gather_copy(src_ref, r, s).start(priority=1)

    def wait_gather(s):
        pltpu.make_async_copy(h_hbm.at[pl.ds(0, tb)], xbuf_ref.at[s], gsem.at[s]).wait()

    @pl.when(i == 0)
    def _():
        start_weights(blk_e_ref[0], 0)
        start_gather(src_cur_ref, 0)

    @pl.when(i >= n_used)
    def _():
        y_ref[...] = jnp.zeros(y_ref.shape, y_ref.dtype)

    @pl.when(i < n_used)
    def _():
        @pl.when(i + 1 < n_used)
        def _():
            start_gather(src_nxt_ref, 1 - slot)

        first = jnp.logical_or(i == 0, blk_e_ref[i] != blk_e_ref[jnp.maximum(i - 1, 0)])

        @pl.when(first)
        def _():
            ws = seg_ref[i] % 2
            for cp in weight_copies(blk_e_ref[i], ws):
                cp.wait()
            wgb_ref[...] = wgs_ref[ws].astype(BF16)
            wub_ref[...] = wus_ref[ws].astype(BF16)
            wdb_ref[...] = wds_ref[ws].astype(BF16)
            nxt = nxt_e_ref[i]

            @pl.when(nxt >= 0)
            def _():
                start_weights(nxt, 1 - ws)

        wait_gather(slot)
        x_lo, x_hi = _unpack_bf16_pairs(xbuf_ref[slot])
        xb = jnp.concatenate([x_lo, x_hi], axis=1).astype(BF16)
        hid = (jax.nn.silu(jnp.dot(xb, wgb_ref[...], preferred_element_type=F32))
               * jnp.dot(xb, wub_ref[...], preferred_element_type=F32))
        y = jnp.dot(hid.astype(BF16), wdb_ref[...], preferred_element_type=F32)
        half = y.shape[1] // 2
        y_ref[...] = _pack_bf16_pairs(y[:, :half], y[:, half:])


def _moe_ffn(h, route, w_gate, w_up, w_down, layer, tb, tm):
    n = h.shape[0]
    _, e, d, f = w_gate.shape
    k = MOE_TOPK
    mcopies = n * k
    flat_e = route[:, 0:2].astype(jnp.int32).reshape(mcopies)
    se, order = lax.sort((flat_e, jnp.arange(mcopies, dtype=jnp.int32)), num_keys=1)
    counts = jnp.sum((flat_e[:, None] == jnp.arange(e)[None, :]).astype(jnp.int32), axis=0)
    padded = (counts + tb - 1) // tb * tb
    pad_end = jnp.cumsum(padded)
    pad_start = pad_end - padded
    start = jnp.cumsum(counts) - counts
    nb = (mcopies + e * (tb - 1) + tb - 1) // tb
    p = nb * tb
    blk_start = jnp.arange(nb) * tb
    blk_e = jnp.minimum(jnp.sum((blk_start[:, None] >= pad_end[None, :]).astype(jnp.int32), axis=1),
                        e - 1).astype(jnp.int32)
    blk_off = blk_start - pad_start[blk_e]
    nvalid = jnp.clip(counts[blk_e] - blk_off, 0, tb).astype(jnp.int32)
    n_used = (pad_end[-1] // tb).astype(jnp.int32).reshape(1)
    rows = jnp.arange(tb)[None, :]
    valid = rows < nvalid[:, None]
    sorted_idx = jnp.clip((start[blk_e] + blk_off)[:, None] + rows, 0, mcopies - 1)
    copy_id = order[sorted_idx]
    src3 = jnp.where(valid, copy_id // k, 0).reshape(nb, 1, tb)
    pos_sorted = pad_start[se] + jnp.arange(mcopies, dtype=jnp.int32) - start[se]
    pos = pos_sorted[jnp.argsort(order)].astype(jnp.int32)
    pos3 = pos.reshape(n // tm, tm, k).transpose(0, 2, 1).reshape(n // tm, 1, k * tm)

    change = jnp.concatenate([jnp.ones((1,), jnp.int32), (blk_e[1:] != blk_e[:-1]).astype(jnp.int32)])
    seg = (jnp.cumsum(change) - 1).astype(jnp.int32)
    owner = jnp.where(counts > 0, jnp.arange(e), e)
    later = lax.cummin(owner, axis=0, reverse=True)
    nxt_of = jnp.concatenate([later[1:], jnp.full((1,), e)])
    nxt_e = jnp.where(nxt_of[blk_e] < e, nxt_of[blk_e], -1).astype(jnp.int32)

    smem = functools.partial(pl.BlockSpec, memory_space=pltpu.SMEM)
    any_spec = pl.BlockSpec(memory_space=pl.ANY)
    grid_spec = pltpu.PrefetchScalarGridSpec(
        num_scalar_prefetch=4,
        grid=(nb,),
        in_specs=[
            smem((1, 1, tb), lambda i, *_: (i, 0, 0)),
            smem((1, 1, tb), lambda i, *_: (jnp.minimum(i + 1, nb - 1), 0, 0)),
            any_spec, any_spec, any_spec, any_spec,
        ],
        out_specs=pl.BlockSpec((tb, d // 2), lambda i, *_: (i, 0)),
        scratch_shapes=[
            pltpu.VMEM((2, d, f), F32), pltpu.VMEM((2, d, f), F32), pltpu.VMEM((2, f, d), F32),
            pltpu.VMEM((d, f), BF16), pltpu.VMEM((d, f), BF16), pltpu.VMEM((f, d), BF16),
            pltpu.VMEM((2, tb, d // 2), jnp.uint32),
            pltpu.SemaphoreType.DMA((2,)), pltpu.SemaphoreType.DMA((2,)),
        ],
    )
    y = pl.pallas_call(
        functools.partial(_ffn_kernel, tb=tb, layer=layer),
        grid_spec=grid_spec,
        out_shape=jax.ShapeDtypeStruct((p, d // 2), jnp.uint32),
        compiler_params=_cparams(("arbitrary",), V7X_VMEM_LIMIT),
        name="moe_ffn",
    )(blk_e, seg, nxt_e, n_used, src3, src3, w_gate, w_up, w_down, h)
    return y, pos3


def _combine_kernel(x_ref, r_ref, pos_cur_ref, pos_nxt_ref, y_hbm, *rest, n_norm, want_x):
    g_refs = rest[:n_norm]
    outs = rest[n_norm:-2]
    pbuf_ref, sem = rest[-2:]
    i = pl.program_id(0)
    slot = i % 2
    tm = x_ref.shape[0]

    def start_rows(pos_ref, s):
        for kk in range(MOE_TOPK):
            for r in range(tm):
                pltpu.make_async_copy(y_hbm.at[pl.ds(pos_ref[0, 0, kk * tm + r], 1)],
                                      pbuf_ref.at[s, kk, pl.ds(r, 1)], sem.at[s]).start(priority=r % 2)

    @pl.when(i == 0)
    def _():
        start_rows(pos_cur_ref, 0)

    @pl.when(i + 1 < pl.num_programs(0))
    def _():
        start_rows(pos_nxt_ref, 1 - slot)

    for kk in range(MOE_TOPK):
        pltpu.make_async_copy(y_hbm.at[pl.ds(0, tm)], pbuf_ref.at[slot, kk], sem.at[slot]).wait()

    w0 = r_ref[:, 2:3]
    w1 = r_ref[:, 3:4]
    lo0, hi0 = _unpack_bf16_pairs(pbuf_ref[slot, 0])
    lo1, hi1 = _unpack_bf16_pairs(pbuf_ref[slot, 1])
    x = x_ref[...] + jnp.concatenate([lo0 * w0 + lo1 * w1, hi0 * w0 + hi1 * w1], axis=1)
    oi = 0
    if want_x:
        outs[0][...] = x
        oi = 1
    r = lax.rsqrt(jnp.mean(x * x, axis=-1, keepdims=True) + RMS_EPS)
    y = x * r
    for g_ref, o_ref in zip(g_refs, outs[oi:]):
        o_ref[...] = (y * g_ref[...]).astype(o_ref.dtype)


def _combine_tile(m):
    return _tile(m, 256)


def _combine(x2, route, y_sorted, pos3, gains, out_dtypes, want_x):
    m, d = x2.shape
    tm = _combine_tile(m)
    nrow = m // tm
    smem = functools.partial(pl.BlockSpec, memory_space=pltpu.SMEM)
    in_specs = [pl.BlockSpec((tm, d), lambda i: (i, 0)),
                pl.BlockSpec((tm, LANES), lambda i: (i, 0)),
                smem((1, 1, MOE_TOPK * tm), lambda i: (i, 0, 0)),
                smem((1, 1, MOE_TOPK * tm), lambda i: (jnp.minimum(i + 1, nrow - 1), 0, 0)),
                pl.BlockSpec(memory_space=pl.ANY)]
    in_specs += [pl.BlockSpec((1, d), lambda i: (0, 0)) for _ in gains]
    out_shape = ([jax.ShapeDtypeStruct((m, d), F32)] if want_x else []) + [
        jax.ShapeDtypeStruct((m, d), dt) for dt in out_dtypes]
    out_specs = [pl.BlockSpec((tm, d), lambda i: (i, 0)) for _ in out_shape]
    return pl.pallas_call(
        functools.partial(_combine_kernel, n_norm=len(gains), want_x=want_x),
        grid=(nrow,),
        in_specs=in_specs,
        out_specs=out_specs,
        out_shape=out_shape,
        scratch_shapes=[pltpu.VMEM((2, MOE_TOPK, tm, d // 2), jnp.uint32), pltpu.SemaphoreType.DMA((2,))],
        compiler_params=_cparams(("arbitrary",), V7X_VMEM_LIMIT),
        name="moe_combine",
    )(x2, route, pos3, pos3, y_sorted, *[g.reshape(1, d) for g in gains])


def _t5_bucket(dist):
    n = jnp.maximum(dist, 0)
    exact = REL_BUCKETS // 2
    nf = jnp.maximum(n, 1).astype(F32)
    large = exact + (jnp.log(nf / exact) / math.log(REL_MAX_DIST / exact)
                     * (REL_BUCKETS - exact)).astype(jnp.int32)
    return jnp.where(n < exact, n, jnp.minimum(large, REL_BUCKETS - 1))


def _bias_lookup(table, bucket):
    onehot = jax.nn.one_hot(bucket, REL_BUCKETS, dtype=F32)
    return jnp.einsum('...b,bh->...h', onehot, table, precision=lax.Precision.HIGHEST)


def _cmp_kernel(rk_ref, rv_ref, pe_ref, w1_ref, w2_ref, o_ref):
    half = w1_ref.shape[1] // 2
    for s, r_ref in enumerate((rk_ref, rv_ref)):
        r = r_ref[0, 0].astype(F32)
        ra = (r + pe_ref[s, :, :half]).astype(BF16)
        rb = (r + pe_ref[s, :, half:]).astype(BF16)
        ha = jnp.dot(ra, w1_ref[s, :half, :], preferred_element_type=F32)
        hb = jnp.dot(rb, w1_ref[s, half:, :], preferred_element_type=F32)
        hb_next = jnp.concatenate([hb[1:], hb[:1]], axis=0)
        hid = jax.nn.gelu(ha + hb_next)
        o_ref[0, 0, s] = jnp.dot(hid.astype(BF16), w2_ref[s], preferred_element_type=F32).astype(o_ref.dtype)


def _compress(kv, batch, seq, pes, w1s, w2s):
    g, dk = NSA_GROUPS, NSA_DK
    nr = seq // CMP_STRIDE
    assert nr == dk
    ct = kv[:, :2 * g * dk].reshape(batch, nr, CMP_STRIDE, 2, g, dk)
    r = ct.transpose(3, 0, 4, 1, 2, 5).reshape(2, batch, g, nr, CMP_STRIDE * dk)
    pe = jnp.stack(pes).reshape(2, 1, CMP_LEN * dk)
    w1 = jnp.stack(w1s).reshape(2, CMP_LEN * dk, dk).astype(BF16)
    w2 = jnp.stack(w2s).astype(BF16)
    rspec = pl.BlockSpec((1, 1, nr, CMP_STRIDE * dk), lambda b, gi: (b, gi, 0, 0))
    return pl.pallas_call(
        _cmp_kernel,
        grid=(batch, g),
        in_specs=[rspec, rspec,
                  pl.BlockSpec((2, 1, CMP_LEN * dk), lambda b, gi: (0, 0, 0)),
                  pl.BlockSpec((2, CMP_LEN * dk, dk), lambda b, gi: (0, 0, 0)),
                  pl.BlockSpec((2, dk, dk), lambda b, gi: (0, 0, 0))],
        out_specs=pl.BlockSpec((1, 1, 2, nr, dk), lambda b, gi: (b, gi, 0, 0, 0)),
        out_shape=jax.ShapeDtypeStruct((batch, g, 2, nr, dk), BF16),
        compiler_params=_cparams(("parallel", "parallel"), V7X_VMEM_LIMIT),
        name="nsa_compress",
    )(r[0], r[1], pe, w1, w2)


BIAS_DIAG = 0
BIAS_NEXT = 1
BIAS_CONST = 2
BIAS_WIN_EDGE = 3
BIAS_MASKED = 4
N_BIAS_TILES = 5
CMP_PAD = 16
CMP_BAND = CMP_PAD + LANES // CMP_STRIDE
FAR_GROUP = 4


def _nsa_kernel(q_ref, gate_ref, kvc_ref, ks_ref, vs_ref, kw_ref, vw_ref, biasc_ref, bias_ref,
                ovt_ref, ext_ref, o_ref,
                selpen_ref, m_ref, l_ref, acc_ref, sfar_ref, snear_ref, oc_ref, ow_ref, scmp_ref,
                *, hg, tq, n_cmp, n_sblk):
    qb = pl.program_id(2)
    dk = NSA_DK
    nt_dims = (((1,), (1,)), ((), ()))
    q8 = jnp.concatenate([q_ref[:, h * dk:(h + 1) * dk] for h in range(hg)], axis=0)
    kj = lax.broadcasted_iota(jnp.int32, (LANES, tq), 0)
    qi = lax.broadcasted_iota(jnp.int32, (LANES, tq), 1)
    t_abs = qb * tq + qi

    def tile_h(a):
        return jnp.concatenate([a] * hg, axis=1)

    win_tiles = WINDOW // LANES
    n_near = win_tiles + 1
    lo = jnp.maximum(qb - win_tiles, 0)
    c_far = bias_ref[BIAS_CONST, 0:1, :]

    def scores(k_ref, row0, n, addend):
        return lax.dot_general(k_ref[pl.ds(row0, n), :], q8, nt_dims, preferred_element_type=F32) + addend

    def colmax(sc):
        return jnp.max(sc, axis=0, keepdims=True)

    def pv(v_ref, row0, n, p):
        return lax.dot_general(v_ref[pl.ds(row0, n), :], p.astype(BF16), (((0,), (0,)), ((), ())),
                               preferred_element_type=F32)

    near_row0, near_ok = [], []
    for j in range(n_near):
        kb = qb - (win_tiles - j)
        near_row0.append(pl.multiple_of(jnp.maximum(kb, 0) * LANES, LANES))
        near_ok.append(kb >= 0)

    def near_bias(j, window_branch):
        d = win_tiles - j
        tile = BIAS_DIAG if d == 0 else (BIAS_NEXT if d == 1 else BIAS_CONST)
        if window_branch and d == win_tiles:
            tile = BIAS_WIN_EDGE
        return bias_ref[jnp.where(near_ok[j], tile, BIAS_MASKED)]

    m_w = jnp.full(c_far.shape, 0.5 * NEG, F32)
    for j in range(n_near):
        s_w = scores(kw_ref, near_row0[j], LANES, near_bias(j, True))
        snear_ref[1, j] = s_w
        m_w = jnp.maximum(m_w, colmax(s_w))
    l_w = jnp.zeros(c_far.shape, F32)
    acc_w = jnp.zeros((dk, hg * tq), F32)
    for j in range(n_near):
        p_w = jnp.exp2(snear_ref[1, j] - m_w)
        l_w = l_w + jnp.sum(p_w, axis=0, keepdims=True)
        acc_w = acc_w + pv(vw_ref, near_row0[j], LANES, p_w)
    ow_ref[...] = acc_w * (1.0 / jnp.maximum(l_w, 1e-30))

    s = lax.dot_general(kvc_ref[0], q8, nt_dims, preferred_element_type=F32) + c_far
    scmp_ref[pl.ds(0, CMP_PAD), :] = jnp.zeros((CMP_PAD, hg * tq), F32)
    scmp_ref[pl.ds(CMP_PAD, LANES), :] = s
    band0 = pl.multiple_of(qb * (tq // CMP_STRIDE), 8)
    scmp_ref[pl.ds(band0, CMP_BAND), :] = scmp_ref[pl.ds(band0, CMP_BAND), :] + biasc_ref[...]
    s = scmp_ref[pl.ds(CMP_PAD, LANES), :]
    valid_c = (kj * CMP_STRIDE + (CMP_LEN - 1) <= t_abs) & (kj < n_cmp)
    s = s + tile_h(jnp.where(valid_c, 0.0, NEG))
    mx = jnp.maximum(jnp.max(s, axis=0, keepdims=True), 0.5 * NEG)
    p = jnp.exp2(s - mx)
    p = p * (1.0 / jnp.maximum(jnp.sum(p, axis=0, keepdims=True), 1e-30))
    oc_ref[...] = lax.dot_general(kvc_ref[1], p.astype(BF16), (((0,), (0,)), ((), ())),
                                  preferred_element_type=F32)
    psum = p[:, 0:tq]
    for h in range(1, hg):
        psum = psum + p[:, h * tq:(h + 1) * tq]
    p_hi = psum.astype(BF16)
    p_lo = (psum - p_hi.astype(F32)).astype(BF16)
    imp = (jnp.dot(ovt_ref[...], p_hi, preferred_element_type=F32)
           + jnp.dot(ovt_ref[...], p_lo, preferred_element_type=F32))

    imp = imp[:n_sblk]
    kb_i = lax.broadcasted_iota(jnp.int32, (n_sblk, tq), 0)
    t_b = qb * tq + lax.broadcasted_iota(jnp.int32, (n_sblk, tq), 1)
    cur = lax.shift_right_logical(t_b, int(math.log2(SEL_BLOCK)))
    valid_b = kb_i * SEL_BLOCK <= t_b
    forced = (kb_i == 0) | (kb_i == cur) | (kb_i == cur - 1)
    score = jnp.where(valid_b, jnp.where(forced, FORCED_SCORE, imp), INVALID_SCORE)
    rank = jnp.zeros((n_sblk, tq), F32)
    for j in range(n_sblk):
        cj = score[j:j + 1, :]
        ahead = (cj > score) | ((cj == score) & (kb_i > j))
        rank = rank + jnp.where(ahead, 1.0, 0.0)
    sel = jnp.where(rank < min(SEL_TOPK, n_sblk), 1.0, 0.0)
    if n_sblk < LANES:
        sel = jnp.concatenate([sel, jnp.zeros((LANES - n_sblk, tq), F32)], axis=0)
    selmask = jnp.dot(ext_ref[...], sel.astype(BF16), preferred_element_type=F32)
    selpen_ref[...] = ((selmask - 1.0) * (-NEG)).reshape(selpen_ref.shape)

    m_near = jnp.full(c_far.shape, 0.5 * NEG, F32)
    for j in range(n_near):
        kb_c = lax.shift_right_logical(near_row0[j], int(math.log2(LANES)))
        s_s = scores(ks_ref, near_row0[j], LANES, near_bias(j, False) + tile_h(selpen_ref[kb_c]))
        snear_ref[0, j] = s_s
        m_near = jnp.maximum(m_near, colmax(s_s))

    assert FAR_GROUP - 1 <= win_tiles
    n_grp = (lo + FAR_GROUP - 1) // FAR_GROUP
    m_ref[...] = jnp.full(m_ref.shape, 0.5 * NEG, F32)

    def far_scores(gi, carry):
        kb = FAR_GROUP * gi
        pen = jnp.concatenate(
            [selpen_ref[kb + t] + jnp.where(kb + t < lo, 0.0, NEG) for t in range(FAR_GROUP)], axis=0)
        sc = scores(ks_ref, pl.multiple_of(kb * LANES, LANES), FAR_GROUP * LANES, tile_h(pen))
        sfar_ref[gi] = sc
        m_ref[...] = jnp.maximum(m_ref[...], colmax(sc))
        return carry
    lax.fori_loop(0, n_grp, far_scores, 0)

    m_s = jnp.maximum(m_near, m_ref[...] + c_far)
    l_s = jnp.zeros(m_s.shape, F32)
    acc_s = jnp.zeros((dk, hg * tq), F32)
    for j in range(n_near):
        p_s = jnp.exp2(snear_ref[0, j] - m_s)
        l_s = l_s + jnp.sum(p_s, axis=0, keepdims=True)
        acc_s = acc_s + pv(vs_ref, near_row0[j], LANES, p_s)
    l_ref[...] = l_s
    acc_ref[...] = acc_s

    m_far = m_s - c_far

    def far_pv(gi, carry):
        row0 = pl.multiple_of(FAR_GROUP * gi * LANES, LANES)
        p = jnp.exp2(sfar_ref[gi] - m_far)
        l_ref[...] = l_ref[...] + jnp.sum(p, axis=0, keepdims=True)
        acc_ref[...] = acc_ref[...] + pv(vs_ref, row0, FAR_GROUP * LANES, p)
        return carry
    lax.fori_loop(0, n_grp, far_pv, 0)

    o_s = acc_ref[...] * (1.0 / jnp.maximum(l_ref[...], 1e-30))
    o_w = ow_ref[...]
    o_c = oc_ref[...]
    gt = gate_ref[...].T
    for h in range(hg):
        cols = slice(h * tq, (h + 1) * tq)
        out_t = (gt[3 * h:3 * h + 1] * o_c[:, cols] + gt[3 * h + 1:3 * h + 2] * o_s[:, cols]
                 + gt[3 * h + 2:3 * h + 3] * o_w[:, cols])
        o_ref[:, h * dk:(h + 1) * dk] = out_t.T.astype(o_ref.dtype)


def _nsa_attention(q, gates, kv, kvc, rel_bias, batch, seq, heads):
    g, dk = NSA_GROUPS, NSA_DK
    hg = heads // g
    tq = LANES
    nqb = seq // tq
    n_cmp = (seq - CMP_LEN) // CMP_STRIDE + 1
    n_sblk = seq // SEL_BLOCK
    nr = seq // CMP_STRIDE
    assert nr == LANES and n_sblk <= LANES and n_sblk % 8 == 0 and tq == LANES
    nt = seq // LANES
    rows = hg * tq
    table = rel_bias.astype(F32) * LOG2E
    assert (REL_MAX_DIST + CMP_LEN - 1 + CMP_STRIDE - 1) // CMP_STRIDE <= CMP_PAD and tq % CMP_STRIDE == 0
    band_w = jnp.arange(CMP_BAND)[:, None]
    dist_c = jnp.arange(tq)[None, :] - (band_w - CMP_PAD) * CMP_STRIDE - (CMP_LEN - 1)
    bias_c = _bias_lookup(table, _t5_bucket(dist_c)) - table[REL_BUCKETS - 1]
    bias_c = bias_c.reshape(CMP_BAND, tq, g, hg).transpose(2, 0, 3, 1).reshape(g, CMP_BAND, rows)
    ii = jnp.arange(tq)
    n_far = -(-(REL_MAX_DIST + tq - 1) // tq)
    dist = jnp.arange(n_far + 1)[:, None, None] * tq + ii[None, None, :] - ii[None, :, None]
    bias_t = _bias_lookup(table, _t5_bucket(dist))
    bias_t = bias_t.reshape(n_far + 1, tq, tq, g, hg).transpose(0, 3, 1, 4, 2).reshape(n_far + 1, g, tq, rows)
    assert n_far == BIAS_CONST and WINDOW % tq == 0 and WINDOW // tq >= n_far
    key_j = jnp.arange(tq)[:, None]
    qry_i = jnp.tile(jnp.arange(tq), hg)[None, :]
    future = key_j > qry_i
    bias_t = jnp.stack([
        jnp.where(future, NEG, bias_t[0]),
        bias_t[1],
        bias_t[2],
        jnp.where(future, bias_t[2], NEG),
        jnp.full_like(bias_t[2], NEG),
    ])
    cs = np.arange(LANES) * CMP_STRIDE
    ss = np.arange(LANES) * SEL_BLOCK
    ov = np.clip(np.minimum(cs[:, None] + CMP_LEN, ss[None, :] + SEL_BLOCK)
                 - np.maximum(cs[:, None], ss[None, :]), 0, None) / CMP_LEN
    ov[n_cmp:, :] = 0.0
    ov[:, n_sblk:] = 0.0
    overlap_t = jnp.asarray(ov.T, BF16)
    ex = ((np.arange(seq)[:, None] // SEL_BLOCK) == np.arange(LANES)[None, :]).astype(np.float32)
    expand_t = jnp.asarray(ex, BF16)

    kspec = lambda blk: pl.BlockSpec((seq, dk), lambda b, gi, qb: (b, blk * g + gi))
    return pl.pallas_call(
        functools.partial(_nsa_kernel, hg=hg, tq=tq, n_cmp=n_cmp, n_sblk=n_sblk),
        grid=(batch, g, nqb),
        in_specs=[
            pl.BlockSpec((tq, hg * dk), lambda b, gi, qb: (b * nqb + qb, gi)),
            pl.BlockSpec((tq, LANES), lambda b, gi, qb: (b * nqb + qb, gi)),
            pl.BlockSpec((None, None, 2, nr, dk), lambda b, gi, qb: (b, gi, 0, 0, 0)),
            kspec(2), kspec(3), kspec(4), kspec(5),
            pl.BlockSpec((None, CMP_BAND, rows), lambda b, gi, qb: (gi, 0, 0)),
            pl.BlockSpec((N_BIAS_TILES, None, tq, rows), lambda b, gi, qb: (0, gi, 0, 0)),
            pl.BlockSpec((LANES, LANES), lambda b, gi, qb: (0, 0)),
            pl.BlockSpec((seq, LANES), lambda b, gi, qb: (0, 0)),
        ],
        out_specs=pl.BlockSpec((tq, hg * dk), lambda b, gi, qb: (b * nqb + qb, gi)),
        out_shape=jax.ShapeDtypeStruct((batch * seq, heads * dk), BF16),
        scratch_shapes=[pltpu.VMEM((nt, LANES, tq), F32),
                        pltpu.VMEM((1, rows), F32), pltpu.VMEM((1, rows), F32),
                        pltpu.VMEM((dk, rows), F32),
                        pltpu.VMEM((-(-nt // FAR_GROUP), FAR_GROUP * LANES, rows), F32),
                        pltpu.VMEM((2, WINDOW // LANES + 1, LANES, rows), F32),
                        pltpu.VMEM((dk, rows), F32), pltpu.VMEM((dk, rows), F32),
                        pltpu.VMEM((CMP_PAD + LANES, rows), F32)],
        compiler_params=_cparams(("parallel", "parallel", "arbitrary"), V7X_VMEM_LIMIT),
        name="nsa_attention",
    )(q, gates, kvc, kv, kv, kv, kv, bias_c, bias_t, overlap_t, expand_t)


def _nsa_layer(x2, h_kv, h_q, w_kv, cmp_params, rel_bias, w_q, w_o, layer, batch, seq):
    m, d = x2.shape
    g, dk = NSA_GROUPS, NSA_DK
    heads = d // dk
    hg = heads // g
    tm = _tile(seq, 1024)
    kv = _matmul(h_kv, w_kv[None], 0, col0=0, ncols=6 * g * dk, tm=tm, tn=_tile(6 * g * dk, 512),
                 out_dtype=BF16, epilogue=_ep_store, name="nsa_proj_kv")
    (k_pe, k_w1, k_w2, v_pe, v_w1, v_w2) = cmp_params
    kvc = _compress(kv, batch, seq, (k_pe, v_pe), (k_w1, v_w1), (k_w2, v_w2))
    q = _matmul(h_q, w_q, layer, col0=0, ncols=heads * dk, tm=tm, tn=_tile(heads * dk, 512),
                out_dtype=BF16, epilogue=functools.partial(_ep_scale, scale=dk ** -0.5 * LOG2E),
                name="nsa_proj_q")
    wg = w_q[layer, :, heads * dk:].reshape(d, g, hg * 3)
    wg = jnp.pad(wg, ((0, 0), (0, 0), (0, LANES - hg * 3))).reshape(1, d, g * LANES)
    gates = _matmul(h_q, wg, 0, col0=0, ncols=g * LANES, tm=tm, tn=g * LANES, out_dtype=F32,
                    epilogue=_ep_sigmoid, name="nsa_proj_gate")
    o = _nsa_attention(q, gates, kv, kvc, rel_bias, batch, seq, heads)
    tn_o = _tile(d, 512)
    return _matmul(o, w_o, layer, col0=0, ncols=d, tm=tm, tn=tn_o, out_dtype=F32, epilogue=_ep_residual,
                   extra=((x2, (tm, tn_o), lambda j, i: (i, j)),), name="nsa_out_proj")


def kernel(x, ret_norm_g, ret_w_in, ret_w_o, kv_norm_g, nsa_w_kv, cmp_k_pe, cmp_k_w1, cmp_k_w2, cmp_v_pe, cmp_v_w1, cmp_v_w2, rel_bias, nsa_norm_g, nsa_w_q, nsa_w_o, moe_norm_g, moe_w_router_group, moe_w_router_expert, moe_w_gate, moe_w_up, moe_w_down, final_norm_g):
    batch, seq, d = x.shape
    depth = moe_norm_g.shape[0]
    n_a = ret_norm_g.shape[0]
    x2 = x.reshape(batch * seq, d)
    cmp_params = (cmp_k_pe, cmp_k_w1, cmp_k_w2, cmp_v_pe, cmp_v_w1, cmp_v_w2)
    h_kv = h_q = None
    out = None
    for layer in range(depth):
        if layer < n_a:
            x2 = _retention_layer(x2, ret_norm_g[layer], ret_w_in, ret_w_o, layer, batch, seq)
        else:
            j = layer - n_a
            if h_q is None:
                h_kv = _rmsnorm(x2, kv_norm_g)
                h_q = _rmsnorm(x2, nsa_norm_g[j])
            x2 = _nsa_layer(x2, h_kv, h_q, nsa_w_kv, cmp_params, rel_bias, nsa_w_q, nsa_w_o, j,
                            batch, seq)
        hm, route = _route(x2, moe_norm_g[layer], moe_w_router_group[layer], moe_w_router_expert[layer])
        ys, pos = _moe_ffn(hm, route, moe_w_gate, moe_w_up, moe_w_down, layer, tb=MOE_ROWS,
                           tm=_combine_tile(batch * seq))
        if layer == depth - 1:
            (out,) = _combine(x2, route, ys, pos, [final_norm_g], [F32], want_x=False)
        elif layer + 1 == n_a:
            j = layer + 1 - n_a
            x2, h_kv, h_q = _combine(x2, route, ys, pos, [kv_norm_g, nsa_norm_g[j]], [BF16, BF16],
                                     want_x=True)
        elif layer + 1 < n_a:
            (x2,) = _combine(x2, route, ys, pos, [], [], want_x=True)
        else:
            j = layer + 1 - n_a
            x2, h_q = _combine(x2, route, ys, pos, [nsa_norm_g[j]], [BF16], want_x=True)
    return out.reshape(batch, seq, d)
```

```python
import functools
import math

import numpy as np
import jax
import jax.numpy as jnp
from jax import lax
from jax.experimental import pallas as pl
from jax.experimental.pallas import tpu as pltpu

RMS_EPS = 1e-6
RET_DK = 256
RET_DV = 2 * RET_DK
ROPE_BASE = 10000.0
NSA_DK = 128
NSA_GROUPS = 4
CMP_LEN = 32
CMP_STRIDE = 16
SEL_BLOCK = 64
SEL_TOPK = 16
WINDOW = 512
FORCED_SCORE = 1e4
INVALID_SCORE = -1e9
REL_BUCKETS = 32
REL_MAX_DIST = 128
MOE_GROUPS = 8
MOE_EPG = 8
MOE_EXPERTS = MOE_GROUPS * MOE_EPG
MOE_TOPK = 2
MOE_ROWS = 256
MOE_W_SPLIT = 4

LANES = 128
V7X_VMEM_LIMIT = 56 * 1024 * 1024
NEG = -1e30
LOG2E = math.log2(math.e)

BF16 = jnp.bfloat16
F32 = jnp.float32


def _tile(dim, pref):
    t = min(dim, pref)
    while dim % t:
        t //= 2
    return t


def _cparams(sem, vmem=None):
    return pltpu.CompilerParams(dimension_semantics=sem, vmem_limit_bytes=vmem)


def _norm_kernel(x_ref, g_ref, o_ref):
    x = x_ref[...]
    y = x * lax.rsqrt(jnp.mean(x * x, axis=-1, keepdims=True) + RMS_EPS)
    o_ref[...] = (y * g_ref[...]).astype(o_ref.dtype)


def _rmsnorm(x2, g, out_dtype=BF16):
    m, d = x2.shape
    tm = _tile(m, 256)
    return pl.pallas_call(
        _norm_kernel,
        grid=(m // tm,),
        in_specs=[pl.BlockSpec((tm, d), lambda i: (i, 0)),
                  pl.BlockSpec((1, d), lambda i: (0, 0))],
        out_specs=pl.BlockSpec((tm, d), lambda i: (i, 0)),
        out_shape=jax.ShapeDtypeStruct((m, d), out_dtype),
        compiler_params=_cparams(("parallel",), V7X_VMEM_LIMIT),
        name="rmsnorm",
    )(x2, g.reshape(1, d))


MM_CHUNK = 256


def _mm_kernel(a_ref, w_ref, *rest, n_extra, epilogue, cast_w):
    extra = rest[:n_extra]
    o_ref = rest[n_extra]
    if cast_w:
        wbf_ref = rest[n_extra + 1]

        @pl.when(pl.program_id(1) == 0)
        def _():
            wbf_ref[...] = w_ref[...].astype(BF16)
    else:
        wbf_ref = w_ref

    a = a_ref[...]
    tn = o_ref.shape[1]
    chunk = min(tn, MM_CHUNK)
    for c0 in range(0, tn, chunk):
        cols = slice(c0, c0 + chunk)
        acc = jnp.dot(a, wbf_ref[:, cols], preferred_element_type=F32)
        epilogue(acc, o_ref, cols, *extra)


def _matmul(a, w, layer, *, col0, ncols, tm, tn, out_dtype, epilogue, extra=(), name):
    m, k = a.shape
    assert col0 % tn == 0 and ncols % tn == 0 and m % tm == 0
    joff = col0 // tn
    cast_w = w.dtype != BF16
    in_specs = [pl.BlockSpec((tm, k), lambda j, i: (i, 0)),
                pl.BlockSpec((None, k, tn), lambda j, i: (layer, 0, j + joff))]
    args = [a, w]
    for arr, bshape, imap in extra:
        in_specs.append(pl.BlockSpec(bshape, imap))
        args.append(arr)
    return pl.pallas_call(
        functools.partial(_mm_kernel, n_extra=len(extra), epilogue=epilogue, cast_w=cast_w),
        grid=(ncols // tn, m // tm),
        in_specs=in_specs,
        out_specs=pl.BlockSpec((tm, tn), lambda j, i: (i, j)),
        out_shape=jax.ShapeDtypeStruct((m, ncols), out_dtype),
        scratch_shapes=[pltpu.VMEM((k, tn), BF16)] if cast_w else [],
        compiler_params=_cparams(("arbitrary", "arbitrary"), V7X_VMEM_LIMIT),
        name=name,
    )(*args)


def _cast_kernel(w_ref, o_ref):
    o_ref[...] = w_ref[...].astype(o_ref.dtype)


def _cast_bf16(w, layer):
    _, k, n = w.shape
    tk = _tile(k, 512)
    return pl.pallas_call(
        _cast_kernel,
        grid=(k // tk,),
        in_specs=[pl.BlockSpec((None, tk, n), lambda i: (layer, i, 0))],
        out_specs=pl.BlockSpec((None, tk, n), lambda i: (0, i, 0)),
        out_shape=jax.ShapeDtypeStruct((1, k, n), BF16),
        compiler_params=_cparams(("parallel",), V7X_VMEM_LIMIT),
        name="cast_bf16",
    )(w)


def _ep_store(acc, o_ref, cols):
    o_ref[:, cols] = acc.astype(o_ref.dtype)


def _ep_residual(acc, o_ref, cols, r_ref):
    o_ref[:, cols] = (r_ref[:, cols] + acc).astype(o_ref.dtype)


def _ep_scale(acc, o_ref, cols, *, scale):
    o_ref[:, cols] = (acc * scale).astype(o_ref.dtype)


def _ep_sigmoid(acc, o_ref, cols):
    o_ref[:, cols] = jax.nn.sigmoid(acc).astype(o_ref.dtype)


def _ep_rotary(acc, o_ref, cols, cos_ref, sin_ref, *, k_first_tile, k_scale):
    assert acc.shape[1] == RET_DK
    scale = jnp.where(pl.program_id(0) >= k_first_tile, k_scale, 1.0).astype(F32)
    cos = cos_ref[...] * scale
    sin = sin_ref[...] * scale
    half = RET_DK // 2
    x1 = acc[:, :half]
    x2 = acc[:, half:]
    o_ref[:, cols.start:cols.start + half] = (x1 * cos - x2 * sin).astype(o_ref.dtype)
    o_ref[:, cols.start + half:cols.stop] = (x1 * sin + x2 * cos).astype(o_ref.dtype)


def _ret_kernel(dchunk_ref, q_ref, k_ref, v_ref, g_ref, dintra_ref, dq_ref, dk_ref, y_ref, state_ref, *, hp):
    c = pl.program_id(2)

    @pl.when(c == 0)
    def _():
        state_ref[...] = jnp.zeros_like(state_ref)

    for j in range(hp):
        h = pl.program_id(1) * hp + j
        q = q_ref[:, j * RET_DK:(j + 1) * RET_DK]
        k = k_ref[:, j * RET_DK:(j + 1) * RET_DK]
        v = v_ref[:, j * RET_DV:(j + 1) * RET_DV]
        scores = lax.dot_general(q, k, (((1,), (1,)), ((), ())), preferred_element_type=F32)
        scores = scores * dintra_ref[j]
        intra = jnp.dot(scores.astype(BF16), v, preferred_element_type=F32)
        cross = jnp.dot(q, state_ref[j].astype(BF16), preferred_element_type=F32)
        dq = dq_ref[j]
        out = jnp.concatenate(
            [intra[:, n:n + LANES] + cross[:, n:n + LANES] * dq for n in range(0, RET_DV, LANES)], axis=1)
        dkc = dk_ref[j]
        kd = (k.astype(F32) * jnp.concatenate([dkc] * (RET_DK // LANES), axis=1)).astype(BF16)
        state_ref[j] = dchunk_ref[h] * state_ref[j] + lax.dot_general(
            kd, v, (((0,), (0,)), ((), ())), preferred_element_type=F32)
        out = out * lax.rsqrt(jnp.mean(out * out, axis=-1, keepdims=True) + RMS_EPS)
        gate = g_ref[:, j * RET_DV:(j + 1) * RET_DV].astype(F32)
        y_ref[:, j * RET_DV:(j + 1) * RET_DV] = (jax.nn.silu(gate) * out).astype(y_ref.dtype)


def _retention_core(qk, vg, batch, seq, heads):
    m = batch * seq
    c = _tile(seq, 256)
    nc = seq // c
    log_gamma = jnp.log1p(-jnp.exp2(-5.0 - jnp.arange(heads, dtype=F32)))
    idx = jnp.arange(c, dtype=F32)
    diff = idx[:, None] - idx[None, :]
    dintra = jnp.where(diff >= 0, jnp.exp(log_gamma[:, None, None] * jnp.maximum(diff, 0.0)), 0.0)
    dq = jnp.broadcast_to(jnp.exp(log_gamma[:, None] * (idx + 1.0))[:, :, None], (heads, c, LANES))
    dk = jnp.broadcast_to(jnp.exp(log_gamma[:, None] * (c - 1.0 - idx))[:, :, None], (heads, c, LANES))
    dchunk = jnp.exp(log_gamma * c)
    hp = next(n for n in (4, 2, 1) if heads % n == 0)
    ng = heads // hp
    grid_spec = pltpu.PrefetchScalarGridSpec(
        num_scalar_prefetch=1,
        grid=(batch, ng, nc),
        in_specs=[
            pl.BlockSpec((c, hp * RET_DK), lambda b, h, i, s: (b * nc + i, h)),
            pl.BlockSpec((c, hp * RET_DK), lambda b, h, i, s: (b * nc + i, ng + h)),
            pl.BlockSpec((c, hp * RET_DV), lambda b, h, i, s: (b * nc + i, h)),
            pl.BlockSpec((c, hp * RET_DV), lambda b, h, i, s: (b * nc + i, ng + h)),
            pl.BlockSpec((hp, c, c), lambda b, h, i, s: (h, 0, 0)),
            pl.BlockSpec((hp, c, LANES), lambda b, h, i, s: (h, 0, 0)),
            pl.BlockSpec((hp, c, LANES), lambda b, h, i, s: (h, 0, 0)),
        ],
        out_specs=pl.BlockSpec((c, hp * RET_DV), lambda b, h, i, s: (b * nc + i, h)),
        scratch_shapes=[pltpu.VMEM((hp, RET_DK, RET_DV), F32)],
    )
    return pl.pallas_call(
        functools.partial(_ret_kernel, hp=hp),
        grid_spec=grid_spec,
        out_shape=jax.ShapeDtypeStruct((m, heads * RET_DV), BF16),
        compiler_params=_cparams(("parallel", "parallel", "arbitrary"), V7X_VMEM_LIMIT),
        name="retention_core",
    )(dchunk, qk, qk, vg, vg, dintra, dq, dk)


def _retention_layer(x2, norm_g, w_in, w_o, layer, batch, seq):
    m, d = x2.shape
    heads = d // RET_DK
    hq = heads * RET_DK
    h = _rmsnorm(x2, norm_g)
    pos = jnp.arange(seq, dtype=F32)
    half = RET_DK // 2
    inv = 1.0 / (ROPE_BASE ** (jnp.arange(half, dtype=F32) / half))
    ang = pos[:, None] * inv[None, :]
    cos, sin = jnp.cos(ang), jnp.sin(ang)
    tm = _tile(seq, 1024)
    tn = _tile(hq, 512)
    nrow = seq // tm
    rot = functools.partial(_ep_rotary, k_first_tile=hq // tn, k_scale=RET_DK ** -0.5)
    qk = _matmul(h, w_in, layer, col0=0, ncols=2 * hq, tm=tm, tn=tn, out_dtype=BF16, epilogue=rot,
                 extra=((cos, (tm, half), lambda j, i: (i % nrow, 0)),
                        (sin, (tm, half), lambda j, i: (i % nrow, 0))),
                 name="ret_proj_qk")
    vg = _matmul(h, w_in, layer, col0=2 * hq, ncols=2 * heads * RET_DV, tm=tm, tn=tn, out_dtype=BF16,
                 epilogue=_ep_store, name="ret_proj_vg")
    y = _retention_core(qk, vg, batch, seq, heads)
    tm_o = _tile(seq, 512)
    tn_o = _tile(d, 512)
    return _matmul(y, _cast_bf16(w_o, layer), 0, col0=0, ncols=d, tm=tm_o, tn=tn_o, out_dtype=F32,
                   epilogue=_ep_residual, extra=((x2, (tm_o, tn_o), lambda j, i: (i, j)),),
                   name="ret_out_proj")


def _pack_bf16_pairs(lo, hi):
    lo_bits = pltpu.bitcast(lo.astype(BF16).astype(F32), jnp.uint32)
    hi_bits = pltpu.bitcast(hi.astype(BF16).astype(F32), jnp.uint32)
    return hi_bits | lax.shift_right_logical(lo_bits, jnp.uint32(16))


def _unpack_bf16_pairs(words):
    lo = pltpu.bitcast(lax.shift_left(words, jnp.uint32(16)), F32)
    hi = pltpu.bitcast(words & jnp.uint32(0xFFFF0000), F32)
    return lo, hi


def _route_kernel(x_ref, g_ref, wcat_ref, h_ref, r_ref):
    x = x_ref[...]
    hn = x * lax.rsqrt(jnp.mean(x * x, axis=-1, keepdims=True) + RMS_EPS) * g_ref[...]
    half = hn.shape[1] // 2
    h_ref[...] = _pack_bf16_pairs(hn[:, :half], hn[:, half:])
    hn_hi = hn.astype(BF16)
    hn_lo = (hn - hn_hi.astype(F32)).astype(BF16)
    both = jnp.dot(hn_hi, wcat_ref[...], preferred_element_type=F32)
    logits = (both[:, :LANES] + both[:, LANES:]
              + jnp.dot(hn_lo, wcat_ref[:, :LANES], preferred_element_type=F32))
    lane = lax.broadcasted_iota(jnp.int32, logits.shape, 1)
    is_g = lane < MOE_GROUPS
    lg = jnp.where(is_g, logits, -jnp.inf)
    mg = jnp.max(lg, axis=-1, keepdims=True)
    grp = jnp.min(jnp.where(lg == mg, lane, LANES), axis=-1, keepdims=True)
    p_grp = 1.0 / jnp.sum(jnp.where(is_g, jnp.exp(lg - mg), 0.0), axis=-1, keepdims=True)
    e_lane = lane - MOE_GROUPS
    is_e = (e_lane >= 0) & (e_lane < MOE_EXPERTS) & ((e_lane >> 3) == grp)
    le = jnp.where(is_e, logits, -jnp.inf)
    m1 = jnp.max(le, axis=-1, keepdims=True)
    i1 = jnp.min(jnp.where(le == m1, lane, LANES), axis=-1, keepdims=True)
    le2 = jnp.where(lane == i1, -jnp.inf, le)
    m2 = jnp.max(le2, axis=-1, keepdims=True)
    i2 = jnp.min(jnp.where(le2 == m2, lane, LANES), axis=-1, keepdims=True)
    t = jnp.exp(m2 - m1)
    w1 = p_grp / (1.0 + t)
    w2 = p_grp * t / (1.0 + t)
    e1 = (i1 - MOE_GROUPS).astype(F32)
    e2 = (i2 - MOE_GROUPS).astype(F32)
    r_ref[...] = jnp.where(lane == 0, e1, jnp.where(lane == 1, e2,
                           jnp.where(lane == 2, w1, jnp.where(lane == 3, w2, 0.0))))


def _route(x2, norm_g, w_rg, w_re):
    m, d = x2.shape
    tm = _tile(m, 256)
    wr = jnp.concatenate(
        [w_rg, w_re, jnp.zeros((d, LANES - MOE_GROUPS - MOE_EXPERTS), F32)], axis=1)
    wr_hi = wr.astype(BF16)
    wr_lo = (wr - wr_hi.astype(F32)).astype(BF16)
    wr_cat = jnp.concatenate([wr_hi, wr_lo], axis=1)
    return pl.pallas_call(
        _route_kernel,
        grid=(m // tm,),
        in_specs=[pl.BlockSpec((tm, d), lambda i: (i, 0)),
                  pl.BlockSpec((1, d), lambda i: (0, 0)),
                  pl.BlockSpec((d, 2 * LANES), lambda i: (0, 0))],
        out_specs=[pl.BlockSpec((tm, d // 2), lambda i: (i, 0)),
                   pl.BlockSpec((tm, LANES), lambda i: (i, 0))],
        out_shape=[jax.ShapeDtypeStruct((m, d // 2), jnp.uint32),
                   jax.ShapeDtypeStruct((m, LANES), F32)],
        compiler_params=_cparams(("parallel",), V7X_VMEM_LIMIT),
        name="moe_route",
    )(x2, norm_g.reshape(1, d), wr_cat)


def _ffn_kernel(blk_e_ref, seg_ref, nxt_e_ref, nused_ref,
                src_cur_ref, src_nxt_ref, wg_hbm, wu_hbm, wd_hbm, h_hbm,
                y_ref,
                wgs_ref, wus_ref, wds_ref, wgb_ref, wub_ref, wdb_ref, xbuf_ref, gsem, wsem, *, tb, layer):
    i = pl.program_id(0)
    n_used = nused_ref[0]
    slot = i % 2

    def weight_copies(e, ws):
        copies = []
        for hbm, stage in ((wg_hbm, wgs_ref), (wu_hbm, wus_ref), (wd_hbm, wds_ref)):
            rows = stage.shape[1] // MOE_W_SPLIT
            for c in range(MOE_W_SPLIT):
                copies.append(pltpu.make_async_copy(hbm.at[layer, e, pl.ds(c * rows, rows)],
                                                    stage.at[ws, pl.ds(c * rows, rows)], wsem.at[ws]))
        return copies

    def start_weights(e, ws):
        for n, cp in enumerate(weight_copies(e, ws)):
            cp.start(priority=n % 2)

    def gather_copy(src_ref, r, s):
        return pltpu.make_async_copy(h_hbm.at[pl.ds(src_ref[0, 0, r], 1)],
                                     xbuf_ref.at[s, pl.ds(r, 1)], gsem.at[s])

    def start_gather(src_ref, s):
        for r in range(tb):
            gather_copy(src_ref, r, s).start(priority=1)

    def wait_gather(s):
        pltpu.make_async_copy(h_hbm.at[pl.ds(0, tb)], xbuf_ref.at[s], gsem.at[s]).wait()

    @pl.when(i == 0)
    def _():
        start_weights(blk_e_ref[0], 0)
        start_gather(src_cur_ref, 0)

    @pl.when(i >= n_used)
    def _():
        y_ref[...] = jnp.zeros(y_ref.shape, y_ref.dtype)

    @pl.when(i < n_used)
    def _():
        @pl.when(i + 1 < n_used)
        def _():
            start_gather(src_nxt_ref, 1 - slot)

        first = jnp.logical_or(i == 0, blk_e_ref[i] != blk_e_ref[jnp.maximum(i - 1, 0)])

        @pl.when(first)
        def _():
            ws = seg_ref[i] % 2
            for cp in weight_copies(blk_e_ref[i], ws):
                cp.wait()
            wgb_ref[...] = wgs_ref[ws].astype(BF16)
            wub_ref[...] = wus_ref[ws].astype(BF16)
            wdb_ref[...] = wds_ref[ws].astype(BF16)
            nxt = nxt_e_ref[i]

            @pl.when(nxt >= 0)
            def _():
                start_weights(nxt, 1 - ws)

        wait_gather(slot)
        x_lo, x_hi = _unpack_bf16_pairs(xbuf_ref[slot])
        xb = jnp.concatenate([x_lo, x_hi], axis=1).astype(BF16)
        hid = (jax.nn.silu(jnp.dot(xb, wgb_ref[...], preferred_element_type=F32))
               * jnp.dot(xb, wub_ref[...], preferred_element_type=F32))
        y = jnp.dot(hid.astype(BF16), wdb_ref[...], preferred_element_type=F32)
        half = y.shape[1] // 2
        y_ref[...] = _pack_bf16_pairs(y[:, :half], y[:, half:])


def _moe_ffn(h, route, w_gate, w_up, w_down, layer, tb, tm):
    n = h.shape[0]
    _, e, d, f = w_gate.shape
    k = MOE_TOPK
    mcopies = n * k
    flat_e = route[:, 0:2].astype(jnp.int32).reshape(mcopies)
    se, order = lax.sort((flat_e, jnp.arange(mcopies, dtype=jnp.int32)), num_keys=1)
    counts = jnp.sum((flat_e[:, None] == jnp.arange(e)[None, :]).astype(jnp.int32), axis=0)
    padded = (counts + tb - 1) // tb * tb
    pad_end = jnp.cumsum(padded)
    pad_start = pad_end - padded
    start = jnp.cumsum(counts) - counts
    nb = (mcopies + e * (tb - 1) + tb - 1) // tb
    p = nb * tb
    blk_start = jnp.arange(nb) * tb
    blk_e = jnp.minimum(jnp.sum((blk_start[:, None] >= pad_end[None, :]).astype(jnp.int32), axis=1),
                        e - 1).astype(jnp.int32)
    blk_off = blk_start - pad_start[blk_e]
    nvalid = jnp.clip(counts[blk_e] - blk_off, 0, tb).astype(jnp.int32)
    n_used = (pad_end[-1] // tb).astype(jnp.int32).reshape(1)
    rows = jnp.arange(tb)[None, :]
    valid = rows < nvalid[:, None]
    sorted_idx = jnp.clip((start[blk_e] + blk_off)[:, None] + rows, 0, mcopies - 1)
    copy_id = order[sorted_idx]
    src3 = jnp.where(valid, copy_id // k, 0).reshape(nb, 1, tb)
    pos_sorted = pad_start[se] + jnp.arange(mcopies, dtype=jnp.int32) - start[se]
    pos = pos_sorted[jnp.argsort(order)].astype(jnp.int32)
    pos3 = pos.reshape(n // tm, tm, k).transpose(0, 2, 1).reshape(n // tm, 1, k * tm)

    change = jnp.concatenate([jnp.ones((1,), jnp.int32), (blk_e[1:] != blk_e[:-1]).astype(jnp.int32)])
    seg = (jnp.cumsum(change) - 1).astype(jnp.int32)
    owner = jnp.where(counts > 0, jnp.arange(e), e)
    later = lax.cummin(owner, axis=0, reverse=True)
    nxt_of = jnp.concatenate([later[1:], jnp.full((1,), e)])
    nxt_e = jnp.where(nxt_of[blk_e] < e, nxt_of[blk_e], -1).astype(jnp.int32)

    smem = functools.partial(pl.BlockSpec, memory_space=pltpu.SMEM)
    any_spec = pl.BlockSpec(memory_space=pl.ANY)
    grid_spec = pltpu.PrefetchScalarGridSpec(
        num_scalar_prefetch=4,
        grid=(nb,),
        in_specs=[
            smem((1, 1, tb), lambda i, *_: (i, 0, 0)),
            smem((1, 1, tb), lambda i, *_: (jnp.minimum(i + 1, nb - 1), 0, 0)),
            any_spec, any_spec, any_spec, any_spec,
        ],
        out_specs=pl.BlockSpec((tb, d // 2), lambda i, *_: (i, 0)),
        scratch_shapes=[
            pltpu.VMEM((2, d, f), F32), pltpu.VMEM((2, d, f), F32), pltpu.VMEM((2, f, d), F32),
            pltpu.VMEM((d, f), BF16), pltpu.VMEM((d, f), BF16), pltpu.VMEM((f, d), BF16),
            pltpu.VMEM((2, tb, d // 2), jnp.uint32),
            pltpu.SemaphoreType.DMA((2,)), pltpu.SemaphoreType.DMA((2,)),
        ],
    )
    y = pl.pallas_call(
        functools.partial(_ffn_kernel, tb=tb, layer=layer),
        grid_spec=grid_spec,
        out_shape=jax.ShapeDtypeStruct((p, d // 2), jnp.uint32),
        compiler_params=_cparams(("arbitrary",), V7X_VMEM_LIMIT),
        name="moe_ffn",
    )(blk_e, seg, nxt_e, n_used, src3, src3, w_gate, w_up, w_down, h)
    return y, pos3


def _combine_kernel(x_ref, r_ref, pos_cur_ref, pos_nxt_ref, y_hbm, *rest, n_norm, want_x):
    g_refs = rest[:n_norm]
    outs = rest[n_norm:-2]
    pbuf_ref, sem = rest[-2:]
    i = pl.program_id(0)
    slot = i % 2
    tm = x_ref.shape[0]

    def start_rows(pos_ref, s):
        for kk in range(MOE_TOPK):
            for r in range(tm):
                pltpu.make_async_copy(y_hbm.at[pl.ds(pos_ref[0, 0, kk * tm + r], 1)],
                                      pbuf_ref.at[s, kk, pl.ds(r, 1)], sem.at[s]).start(priority=r % 2)

    @pl.when(i == 0)
    def _():
        start_rows(pos_cur_ref, 0)

    @pl.when(i + 1 < pl.num_programs(0))
    def _():
        start_rows(pos_nxt_ref, 1 - slot)

    for kk in range(MOE_TOPK):
        pltpu.make_async_copy(y_hbm.at[pl.ds(0, tm)], pbuf_ref.at[slot, kk], sem.at[slot]).wait()

    w0 = r_ref[:, 2:3]
    w1 = r_ref[:, 3:4]
    lo0, hi0 = _unpack_bf16_pairs(pbuf_ref[slot, 0])
    lo1, hi1 = _unpack_bf16_pairs(pbuf_ref[slot, 1])
    x = x_ref[...] + jnp.concatenate([lo0 * w0 + lo1 * w1, hi0 * w0 + hi1 * w1], axis=1)
    oi = 0
    if want_x:
        outs[0][...] = x
        oi = 1
    r = lax.rsqrt(jnp.mean(x * x, axis=-1, keepdims=True) + RMS_EPS)
    y = x * r
    for g_ref, o_ref in zip(g_refs, outs[oi:]):
        o_ref[...] = (y * g_ref[...]).astype(o_ref.dtype)


def _combine_tile(m):
    return _tile(m, 256)


def _combine(x2, route, y_sorted, pos3, gains, out_dtypes, want_x):
    m, d = x2.shape
    tm = _combine_tile(m)
    nrow = m // tm
    smem = functools.partial(pl.BlockSpec, memory_space=pltpu.SMEM)
    in_specs = [pl.BlockSpec((tm, d), lambda i: (i, 0)),
                pl.BlockSpec((tm, LANES), lambda i: (i, 0)),
                smem((1, 1, MOE_TOPK * tm), lambda i: (i, 0, 0)),
                smem((1, 1, MOE_TOPK * tm), lambda i: (jnp.minimum(i + 1, nrow - 1), 0, 0)),
                pl.BlockSpec(memory_space=pl.ANY)]
    in_specs += [pl.BlockSpec((1, d), lambda i: (0, 0)) for _ in gains]
    out_shape = ([jax.ShapeDtypeStruct((m, d), F32)] if want_x else []) + [
        jax.ShapeDtypeStruct((m, d), dt) for dt in out_dtypes]
    out_specs = [pl.BlockSpec((tm, d), lambda i: (i, 0)) for _ in out_shape]
    return pl.pallas_call(
        functools.partial(_combine_kernel, n_norm=len(gains), want_x=want_x),
        grid=(nrow,),
        in_specs=in_specs,
        out_specs=out_specs,
        out_shape=out_shape,
        scratch_shapes=[pltpu.VMEM((2, MOE_TOPK, tm, d // 2), jnp.uint32), pltpu.SemaphoreType.DMA((2,))],
        compiler_params=_cparams(("arbitrary",), V7X_VMEM_LIMIT),
        name="moe_combine",
    )(x2, route, pos3, pos3, y_sorted, *[g.reshape(1, d) for g in gains])


def _t5_bucket(dist):
    n = jnp.maximum(dist, 0)
    exact = REL_BUCKETS // 2
    nf = jnp.maximum(n, 1).astype(F32)
    large = exact + (jnp.log(nf / exact) / math.log(REL_MAX_DIST / exact)
                     * (REL_BUCKETS - exact)).astype(jnp.int32)
    return jnp.where(n < exact, n, jnp.minimum(large, REL_BUCKETS - 1))


def _bias_lookup(table, bucket):
    onehot = jax.nn.one_hot(bucket, REL_BUCKETS, dtype=F32)
    return jnp.einsum('...b,bh->...h', onehot, table, precision=lax.Precision.HIGHEST)


def _cmp_kernel(rk_ref, rv_ref, pe_ref, w1_ref, w2_ref, o_ref):
    half = w1_ref.shape[1] // 2
    for s, r_ref in enumerate((rk_ref, rv_ref)):
        r = r_ref[0, 0].astype(F32)
        ra = (r + pe_ref[s, :, :half]).astype(BF16)
        rb = (r + pe_ref[s, :, half:]).astype(BF16)
        ha = jnp.dot(ra, w1_ref[s, :half, :], preferred_element_type=F32)
        hb = jnp.dot(rb, w1_ref[s, half:, :], preferred_element_type=F32)
        hb_next = jnp.concatenate([hb[1:], hb[:1]], axis=0)
        hid = jax.nn.gelu(ha + hb_next)
        o_ref[0, 0, s] = jnp.dot(hid.astype(BF16), w2_ref[s], preferred_element_type=F32).astype(o_ref.dtype)


def _compress(kv, batch, seq, pes, w1s, w2s):
    g, dk = NSA_GROUPS, NSA_DK
    nr = seq // CMP_STRIDE
    assert nr == dk
    ct = kv[:, :2 * g * dk].reshape(batch, nr, CMP_STRIDE, 2, g, dk)
    r = ct.transpose(3, 0, 4, 1, 2, 5).reshape(2, batch, g, nr, CMP_STRIDE * dk)
    pe = jnp.stack(pes).reshape(2, 1, CMP_LEN * dk)
    w1 = jnp.stack(w1s).reshape(2, CMP_LEN * dk, dk).astype(BF16)
    w2 = jnp.stack(w2s).astype(BF16)
    rspec = pl.BlockSpec((1, 1, nr, CMP_STRIDE * dk), lambda b, gi: (b, gi, 0, 0))
    return pl.pallas_call(
        _cmp_kernel,
        grid=(batch, g),
        in_specs=[rspec, rspec,
                  pl.BlockSpec((2, 1, CMP_LEN * dk), lambda b, gi: (0, 0, 0)),
                  pl.BlockSpec((2, CMP_LEN * dk, dk), lambda b, gi: (0, 0, 0)),
                  pl.BlockSpec((2, dk, dk), lambda b, gi: (0, 0, 0))],
        out_specs=pl.BlockSpec((1, 1, 2, nr, dk), lambda b, gi: (b, gi, 0, 0, 0)),
        out_shape=jax.ShapeDtypeStruct((batch, g, 2, nr, dk), BF16),
        compiler_params=_cparams(("parallel", "parallel"), V7X_VMEM_LIMIT),
        name="nsa_compress",
    )(r[0], r[1], pe, w1, w2)


BIAS_DIAG = 0
BIAS_NEXT = 1
BIAS_CONST = 2
BIAS_WIN_EDGE = 3
BIAS_MASKED = 4
N_BIAS_TILES = 5
CMP_PAD = 16
CMP_BAND = CMP_PAD + LANES // CMP_STRIDE
FAR_GROUP = 4


def _nsa_kernel(q_ref, gate_ref, kvc_ref, ks_ref, vs_ref, kw_ref, vw_ref, biasc_ref, bias_ref,
                ovt_ref, ext_ref, o_ref,
                selpen_ref, m_ref, l_ref, acc_ref, sfar_ref, snear_ref, oc_ref, ow_ref, scmp_ref,
                *, hg, tq, n_cmp, n_sblk):
    qb = pl.program_id(2)
    dk = NSA_DK
    nt_dims = (((1,), (1,)), ((), ()))
    q8 = jnp.concatenate([q_ref[:, h * dk:(h + 1) * dk] for h in range(hg)], axis=0)
    kj = lax.broadcasted_iota(jnp.int32, (LANES, tq), 0)
    qi = lax.broadcasted_iota(jnp.int32, (LANES, tq), 1)
    t_abs = qb * tq + qi

    def tile_h(a):
        return jnp.concatenate([a] * hg, axis=1)

    win_tiles = WINDOW // LANES
    n_near = win_tiles + 1
    lo = jnp.maximum(qb - win_tiles, 0)
    c_far = bias_ref[BIAS_CONST, 0:1, :]

    def scores(k_ref, row0, n, addend):
        return lax.dot_general(k_ref[pl.ds(row0, n), :], q8, nt_dims, preferred_element_type=F32) + addend

    def colmax(sc):
        return jnp.max(sc, axis=0, keepdims=True)

    def pv(v_ref, row0, n, p):
        return lax.dot_general(v_ref[pl.ds(row0, n), :], p.astype(BF16), (((0,), (0,)), ((), ())),
                               preferred_element_type=F32)

    near_row0, near_ok = [], []
    for j in range(n_near):
        kb = qb - (win_tiles - j)
        near_row0.append(pl.multiple_of(jnp.maximum(kb, 0) * LANES, LANES))
        near_ok.append(kb >= 0)

    def near_bias(j, window_branch):
        d = win_tiles - j
        tile = BIAS_DIAG if d == 0 else (BIAS_NEXT if d == 1 else BIAS_CONST)
        if window_branch and d == win_tiles:
            tile = BIAS_WIN_EDGE
        return bias_ref[jnp.where(near_ok[j], tile, BIAS_MASKED)]

    m_w = jnp.full(c_far.shape, 0.5 * NEG, F32)
    for j in range(n_near):
        s_w = scores(kw_ref, near_row0[j], LANES, near_bias(j, True))
        snear_ref[1, j] = s_w
        m_w = jnp.maximum(m_w, colmax(s_w))
    l_w = jnp.zeros(c_far.shape, F32)
    acc_w = jnp.zeros((dk, hg * tq), F32)
    for j in range(n_near):
        p_w = jnp.exp2(snear_ref[1, j] - m_w)
        l_w = l_w + jnp.sum(p_w, axis=0, keepdims=True)
        acc_w = acc_w + pv(vw_ref, near_row0[j], LANES, p_w)
    ow_ref[...] = acc_w * (1.0 / jnp.maximum(l_w, 1e-30))

    s = lax.dot_general(kvc_ref[0], q8, nt_dims, preferred_element_type=F32) + c_far
    scmp_ref[pl.ds(0, CMP_PAD), :] = jnp.zeros((CMP_PAD, hg * tq), F32)
    scmp_ref[pl.ds(CMP_PAD, LANES), :] = s
    band0 = pl.multiple_of(qb * (tq // CMP_STRIDE), 8)
    scmp_ref[pl.ds(band0, CMP_BAND), :] = scmp_ref[pl.ds(band0, CMP_BAND), :] + biasc_ref[...]
    s = scmp_ref[pl.ds(CMP_PAD, LANES), :]
    valid_c = (kj * CMP_STRIDE + (CMP_LEN - 1) <= t_abs) & (kj < n_cmp)
    s = s + tile_h(jnp.where(valid_c, 0.0, NEG))
    mx = jnp.maximum(jnp.max(s, axis=0, keepdims=True), 0.5 * NEG)
    p = jnp.exp2(s - mx)
    p = p * (1.0 / jnp.maximum(jnp.sum(p, axis=0, keepdims=True), 1e-30))
    oc_ref[...] = lax.dot_general(kvc_ref[1], p.astype(BF16), (((0,), (0,)), ((), ())),
                                  preferred_element_type=F32)
    psum = p[:, 0:tq]
    for h in range(1, hg):
        psum = psum + p[:, h * tq:(h + 1) * tq]
    p_hi = psum.astype(BF16)
    p_lo = (psum - p_hi.astype(F32)).astype(BF16)
    imp = (jnp.dot(ovt_ref[...], p_hi, preferred_element_type=F32)
           + jnp.dot(ovt_ref[...], p_lo, preferred_element_type=F32))

    imp = imp[:n_sblk]
    kb_i = lax.broadcasted_iota(jnp.int32, (n_sblk, tq), 0)
    t_b = qb * tq + lax.broadcasted_iota(jnp.int32, (n_sblk, tq), 1)
    cur = lax.shift_right_logical(t_b, int(math.log2(SEL_BLOCK)))
    valid_b = kb_i * SEL_BLOCK <= t_b
    forced = (kb_i == 0) | (kb_i == cur) | (kb_i == cur - 1)
    score = jnp.where(valid_b, jnp.where(forced, FORCED_SCORE, imp), INVALID_SCORE)
    rank = jnp.zeros((n_sblk, tq), F32)
    for j in range(n_sblk):
        cj = score[j:j + 1, :]
        ahead = (cj > score) | ((cj == score) & (kb_i > j))
        rank = rank + jnp.where(ahead, 1.0, 0.0)
    sel = jnp.where(rank < min(SEL_TOPK, n_sblk), 0.0, NEG)
    if n_sblk < LANES:
        sel = jnp.concatenate([sel, jnp.zeros((LANES - n_sblk, tq), F32)], axis=0)
    selpen = jnp.dot(ext_ref[...], sel.astype(BF16), preferred_element_type=F32)
    selpen_ref[...] = selpen.reshape(selpen_ref.shape)

    m_near = jnp.full(c_far.shape, 0.5 * NEG, F32)
    for j in range(n_near):
        kb_c = lax.shift_right_logical(near_row0[j], int(math.log2(LANES)))
        s_s = scores(ks_ref, near_row0[j], LANES, near_bias(j, False) + tile_h(selpen_ref[kb_c]))
        snear_ref[0, j] = s_s
        m_near = jnp.maximum(m_near, colmax(s_s))

    assert FAR_GROUP - 1 <= win_tiles
    n_grp = (lo + FAR_GROUP - 1) // FAR_GROUP
    m_ref[...] = jnp.full(m_ref.shape, 0.5 * NEG, F32)

    def far_scores(gi, carry):
        kb = FAR_GROUP * gi
        pen = jnp.concatenate(
            [selpen_ref[kb + t] + jnp.where(kb + t < lo, 0.0, NEG) for t in range(FAR_GROUP)], axis=0)
        sc = scores(ks_ref, pl.multiple_of(kb * LANES, LANES), FAR_GROUP * LANES, tile_h(pen))
        sfar_ref[gi] = sc
        m_ref[...] = jnp.maximum(m_ref[...], colmax(sc))
        return carry
    lax.fori_loop(0, n_grp, far_scores, 0)

    m_s = jnp.maximum(m_near, m_ref[...] + c_far)
    l_s = jnp.zeros(m_s.shape, F32)
    acc_s = jnp.zeros((dk, hg * tq), F32)
    for j in range(n_near):
        p_s = jnp.exp2(snear_ref[0, j] - m_s)
        l_s = l_s + jnp.sum(p_s, axis=0, keepdims=True)
        acc_s = acc_s + pv(vs_ref, near_row0[j], LANES, p_s)
    l_ref[...] = l_s
    acc_ref[...] = acc_s

    m_far = m_s - c_far

    def far_pv(gi, carry):
        row0 = pl.multiple_of(FAR_GROUP * gi * LANES, LANES)
        p = jnp.exp2(sfar_ref[gi] - m_far)
        l_ref[...] = l_ref[...] + jnp.sum(p, axis=0, keepdims=True)
        acc_ref[...] = acc_ref[...] + pv(vs_ref, row0, FAR_GROUP * LANES, p)
        return carry
    lax.fori_loop(0, n_grp, far_pv, 0)

    o_s = acc_ref[...] * (1.0 / jnp.maximum(l_ref[...], 1e-30))
    o_w = ow_ref[...]
    o_c = oc_ref[...]
    gt = gate_ref[...].T
    for h in range(hg):
        cols = slice(h * tq, (h + 1) * tq)
        out_t = (gt[3 * h:3 * h + 1] * o_c[:, cols] + gt[3 * h + 1:3 * h + 2] * o_s[:, cols]
                 + gt[3 * h + 2:3 * h + 3] * o_w[:, cols])
        o_ref[:, h * dk:(h + 1) * dk] = out_t.T.astype(o_ref.dtype)


def _nsa_attention(q, gates, kv, kvc, rel_bias, batch, seq, heads):
    g, dk = NSA_GROUPS, NSA_DK
    hg = heads // g
    tq = LANES
    nqb = seq // tq
    n_cmp = (seq - CMP_LEN) // CMP_STRIDE + 1
    n_sblk = seq // SEL_BLOCK
    nr = seq // CMP_STRIDE
    assert nr == LANES and n_sblk <= LANES and n_sblk % 8 == 0 and tq == LANES
    nt = seq // LANES
    rows = hg * tq
    table = rel_bias.astype(F32) * LOG2E
    assert (REL_MAX_DIST + CMP_LEN - 1 + CMP_STRIDE - 1) // CMP_STRIDE <= CMP_PAD and tq % CMP_STRIDE == 0
    band_w = jnp.arange(CMP_BAND)[:, None]
    dist_c = jnp.arange(tq)[None, :] - (band_w - CMP_PAD) * CMP_STRIDE - (CMP_LEN - 1)
    bias_c = _bias_lookup(table, _t5_bucket(dist_c)) - table[REL_BUCKETS - 1]
    bias_c = bias_c.reshape(CMP_BAND, tq, g, hg).transpose(2, 0, 3, 1).reshape(g, CMP_BAND, rows)
    ii = jnp.arange(tq)
    n_far = -(-(REL_MAX_DIST + tq - 1) // tq)
    dist = jnp.arange(n_far + 1)[:, None, None] * tq + ii[None, None, :] - ii[None, :, None]
    bias_t = _bias_lookup(table, _t5_bucket(dist))
    bias_t = bias_t.reshape(n_far + 1, tq, tq, g, hg).transpose(0, 3, 1, 4, 2).reshape(n_far + 1, g, tq, rows)
    assert n_far == BIAS_CONST and WINDOW % tq == 0 and WINDOW // tq >= n_far
    key_j = jnp.arange(tq)[:, None]
    qry_i = jnp.tile(jnp.arange(tq), hg)[None, :]
    future = key_j > qry_i
    bias_t = jnp.stack([
        jnp.where(future, NEG, bias_t[0]),
        bias_t[1],
        bias_t[2],
        jnp.where(future, bias_t[2], NEG),
        jnp.full_like(bias_t[2], NEG),
    ])
    cs = np.arange(LANES) * CMP_STRIDE
    ss = np.arange(LANES) * SEL_BLOCK
    ov = np.clip(np.minimum(cs[:, None] + CMP_LEN, ss[None, :] + SEL_BLOCK)
                 - np.maximum(cs[:, None], ss[None, :]), 0, None) / CMP_LEN
    ov[n_cmp:, :] = 0.0
    ov[:, n_sblk:] = 0.0
    overlap_t = jnp.asarray(ov.T, BF16)
    ex = ((np.arange(seq)[:, None] // SEL_BLOCK) == np.arange(LANES)[None, :]).astype(np.float32)
    expand_t = jnp.asarray(ex, BF16)

    kspec = lambda blk: pl.BlockSpec((seq, dk), lambda b, gi, qb: (b, blk * g + gi))
    return pl.pallas_call(
        functools.partial(_nsa_kernel, hg=hg, tq=tq, n_cmp=n_cmp, n_sblk=n_sblk),
        grid=(batch, g, nqb),
        in_specs=[
            pl.BlockSpec((tq, hg * dk), lambda b, gi, qb: (b * nqb + qb, gi)),
            pl.BlockSpec((tq, LANES), lambda b, gi, qb: (b * nqb + qb, gi)),
            pl.BlockSpec((None, None, 2, nr, dk), lambda b, gi, qb: (b, gi, 0, 0, 0)),
            kspec(2), kspec(3), kspec(4), kspec(5),
            pl.BlockSpec((None, CMP_BAND, rows), lambda b, gi, qb: (gi, 0, 0)),
            pl.BlockSpec((N_BIAS_TILES, None, tq, rows), lambda b, gi, qb: (0, gi, 0, 0)),
            pl.BlockSpec((LANES, LANES), lambda b, gi, qb: (0, 0)),
            pl.BlockSpec((seq, LANES), lambda b, gi, qb: (0, 0)),
        ],
        out_specs=pl.BlockSpec((tq, hg * dk), lambda b, gi, qb: (b * nqb + qb, gi)),
        out_shape=jax.ShapeDtypeStruct((batch * seq, heads * dk), BF16),
        scratch_shapes=[pltpu.VMEM((nt, LANES, tq), F32),
                        pltpu.VMEM((1, rows), F32), pltpu.VMEM((1, rows), F32),
                        pltpu.VMEM((dk, rows), F32),
                        pltpu.VMEM((-(-nt // FAR_GROUP), FAR_GROUP * LANES, rows), F32),
                        pltpu.VMEM((2, WINDOW // LANES + 1, LANES, rows), F32),
                        pltpu.VMEM((dk, rows), F32), pltpu.VMEM((dk, rows), F32),
                        pltpu.VMEM((CMP_PAD + LANES, rows), F32)],
        compiler_params=_cparams(("parallel", "parallel", "arbitrary"), V7X_VMEM_LIMIT),
        name="nsa_attention",
    )(q, gates, kvc, kv, kv, kv, kv, bias_c, bias_t, overlap_t, expand_t)


def _nsa_layer(x2, h_kv, h_q, w_kv, cmp_params, rel_bias, w_q, w_o, layer, batch, seq):
    m, d = x2.shape
    g, dk = NSA_GROUPS, NSA_DK
    heads = d // dk
    hg = heads // g
    tm = _tile(seq, 1024)
    kv = _matmul(h_kv, w_kv[None], 0, col0=0, ncols=6 * g * dk, tm=tm, tn=_tile(6 * g * dk, 512),
                 out_dtype=BF16, epilogue=_ep_store, name="nsa_proj_kv")
    (k_pe, k_w1, k_w2, v_pe, v_w1, v_w2) = cmp_params
    kvc = _compress(kv, batch, seq, (k_pe, v_pe), (k_w1, v_w1), (k_w2, v_w2))
    q = _matmul(h_q, w_q, layer, col0=0, ncols=heads * dk, tm=tm, tn=_tile(heads * dk, 512),
                out_dtype=BF16, epilogue=functools.partial(_ep_scale, scale=dk ** -0.5 * LOG2E),
                name="nsa_proj_q")
    wg = w_q[layer, :, heads * dk:].reshape(d, g, hg * 3)
    wg = jnp.pad(wg, ((0, 0), (0, 0), (0, LANES - hg * 3))).reshape(1, d, g * LANES)
    gates = _matmul(h_q, wg, 0, col0=0, ncols=g * LANES, tm=tm, tn=g * LANES, out_dtype=F32,
                    epilogue=_ep_sigmoid, name="nsa_proj_gate")
    o = _nsa_attention(q, gates, kv, kvc, rel_bias, batch, seq, heads)
    tn_o = _tile(d, 512)
    return _matmul(o, w_o, layer, col0=0, ncols=d, tm=tm, tn=tn_o, out_dtype=F32, epilogue=_ep_residual,
                   extra=((x2, (tm, tn_o), lambda j, i: (i, j)),), name="nsa_out_proj")


def kernel(x, ret_norm_g, ret_w_in, ret_w_o, kv_norm_g, nsa_w_kv, cmp_k_pe, cmp_k_w1, cmp_k_w2, cmp_v_pe, cmp_v_w1, cmp_v_w2, rel_bias, nsa_norm_g, nsa_w_q, nsa_w_o, moe_norm_g, moe_w_router_group, moe_w_router_expert, moe_w_gate, moe_w_up, moe_w_down, final_norm_g):
    batch, seq, d = x.shape
    depth = moe_norm_g.shape[0]
    n_a = ret_norm_g.shape[0]
    x2 = x.reshape(batch * seq, d)
    cmp_params = (cmp_k_pe, cmp_k_w1, cmp_k_w2, cmp_v_pe, cmp_v_w1, cmp_v_w2)
    h_kv = h_q = None
    out = None
    for layer in range(depth):
        if layer < n_a:
            x2 = _retention_layer(x2, ret_norm_g[layer], ret_w_in, ret_w_o, layer, batch, seq)
        else:
            j = layer - n_a
            if h_q is None:
                h_kv = _rmsnorm(x2, kv_norm_g)
                h_q = _rmsnorm(x2, nsa_norm_g[j])
            x2 = _nsa_layer(x2, h_kv, h_q, nsa_w_kv, cmp_params, rel_bias, nsa_w_q, nsa_w_o, j,
                            batch, seq)
        hm, route = _route(x2, moe_norm_g[layer], moe_w_router_group[layer], moe_w_router_expert[layer])
        ys, pos = _moe_ffn(hm, route, moe_w_gate, moe_w_up, moe_w_down, layer, tb=MOE_ROWS,
                           tm=_combine_tile(batch * seq))
        if layer == depth - 1:
            (out,) = _combine(x2, route, ys, pos, [final_norm_g], [F32], want_x=False)
        elif layer + 1 == n_a:
            j = layer + 1 - n_a
            x2, h_kv, h_q = _combine(x2, route, ys, pos, [kv_norm_g, nsa_norm_g[j]], [BF16, BF16],
                                     want_x=True)
        elif layer + 1 < n_a:
            (x2,) = _combine(x2, route, ys, pos, [], [], want_x=True)
        else:
            j = layer + 1 - n_a
            x2, h_q = _combine(x2, route, ys, pos, [nsa_norm_g[j]], [BF16], want_x=True)
    return out.reshape(batch, seq, d)
```
